```python
import math
import jax, jax.numpy as jnp
from jax import lax
import numpy as np

D_MODEL = 1024
BATCH = 2
SEQ = 8192
DEPTH = 1
DEC_BATCH = 32
DEC_SEQ = 32
PAST_LEN = 2048

CHUNK = 64
Q_BLOCK = 128
EPS = 1e-6

DA_HEADS = 4
DA_DK = 64
DA_DV = 2 * DA_DK
DA_QK = DA_HEADS * 2 * DA_DK
DA_WIDTH = DA_HEADS * DA_DV

GLA_HEADS = 4
GLA_DK = 64
GLA_DV = 128
GLA_KW = GLA_HEADS * GLA_DK
GLA_VW = GLA_HEADS * GLA_DV
GLA_GATE_RANK = 16
GLA_TAU = 16.0

IN_SIZES = (DA_QK, DA_QK, DA_WIDTH, DA_WIDTH, GLA_KW, GLA_KW, GLA_VW, GLA_VW, GLA_GATE_RANK, 2 * D_MODEL)
D_IN = sum(IN_SIZES)

kernel_name = "hybrid_diffattn_gla_streaming_step"


def rmsnorm(x, g):
    xf = x.astype(jnp.float32)
    y = xf * lax.rsqrt(jnp.mean(xf * xf, axis=-1, keepdims=True) + EPS)
    return (y * g.astype(jnp.float32)).astype(x.dtype)


def project(x, g_in, w_in, w_alpha_up, b_alpha):
    bsz, t, _ = x.shape
    xn = rmsnorm(x, g_in)
    h = xn @ w_in
    split_points = [int(o) for o in np.cumsum(IN_SIZES)[:-1]]
    dq, dk, dv, dg, gq, gk, gv, gg, ga, mg = jnp.split(h, split_points, axis=-1)
    dq = dq.reshape(bsz, t, DA_HEADS, 2, DA_DK)
    dk = dk.reshape(bsz, t, DA_HEADS, 2, DA_DK)
    dv = dv.reshape(bsz, t, DA_HEADS, DA_DV)
    gq = gq.reshape(bsz, t, GLA_HEADS, GLA_DK) * (GLA_DK ** -0.5)
    gk = gk.reshape(bsz, t, GLA_HEADS, GLA_DK)
    gv = gv.reshape(bsz, t, GLA_HEADS, GLA_DV)
    log_a = jax.nn.log_sigmoid((ga @ w_alpha_up + b_alpha).astype(jnp.float32)) / GLA_TAU
    log_a = log_a.reshape(bsz, t, GLA_HEADS, GLA_DK)
    return dq, dk, dv, dg, gq, gk, gv, gg, log_a, mg


def diff_lambda(lq1, lk1, lq2, lk2, lam_init):
    f32 = jnp.float32
    return (jnp.exp(jnp.sum(lq1.astype(f32) * lk1.astype(f32)))
            - jnp.exp(jnp.sum(lq2.astype(f32) * lk2.astype(f32))) + lam_init)


def diff_attend(q, k, v, lam, mask):
    s = jnp.einsum('bqhmd,bkhmd->bhmqk', q, k).astype(jnp.float32) * (DA_DK ** -0.5)
    if mask is not None:
        s = jnp.where(mask, s, -jnp.inf)
    p = jax.nn.softmax(s, axis=-1)
    a = p[:, :, 0] - lam * p[:, :, 1]
    return jnp.einsum('bhqk,bkhv->bqhv', a.astype(v.dtype), v)


def diff_attend_prompt(q, k, v, lam):
    bsz, t = q.shape[0], q.shape[1]
    nqb = t // Q_BLOCK
    qb = jnp.moveaxis(q.reshape(bsz, nqb, Q_BLOCK, DA_HEADS, 2, DA_DK), 1, 0)
    key_chunk = jnp.arange(t) // CHUNK

    def one_block(args):
        qblk, i = args
        q_chunk = (i * Q_BLOCK + jnp.arange(Q_BLOCK)) // CHUNK
        mask = key_chunk[None, :] <= q_chunk[:, None]
        return diff_attend(qblk, k, v, lam, mask)

    o = lax.map(one_block, (qb, jnp.arange(nqb)))
    return jnp.moveaxis(o, 0, 1).reshape(bsz, t, DA_HEADS, DA_DV)


def gla_chunk(S, q, k, v, log_a):
    f32 = jnp.float32
    qf, kf, vf = q.astype(f32), k.astype(f32), v.astype(f32)
    c = q.shape[1]
    b = jnp.cumsum(log_a, axis=1)
    causal = jnp.tril(jnp.ones((c, c), dtype=bool))
    rel = b[:, :, None] - b[:, None, :]
    decay = jnp.exp(jnp.where(causal[None, :, :, None, None], rel, -jnp.inf))
    att = jnp.einsum('bthk,bshk,btshk->bhts', qf, kf, decay)
    o = (jnp.einsum('bhts,bshv->bthv', att, vf)
         + jnp.einsum('bthk,bhkv->bthv', qf * jnp.exp(b), S))
    b_last = b[:, -1]
    k_dec = kf * jnp.exp(b_last[:, None] - b)
    S_new = jnp.exp(b_last)[..., None] * S + jnp.einsum('bshk,bshv->bhkv', k_dec, vf)
    return S_new, o


def gla_prompt(q, k, v, log_a):
    bsz, t = q.shape[0], q.shape[1]
    n = t // CHUNK

    def to_chunks(a):
        return jnp.moveaxis(a.reshape(bsz, n, CHUNK, *a.shape[2:]), 1, 0)

    S0 = jnp.zeros((bsz, GLA_HEADS, GLA_DK, GLA_DV), jnp.float32)
    S_fin, o = lax.scan(lambda S, xs: gla_chunk(S, *xs), S0,
                        (to_chunks(q), to_chunks(k), to_chunks(v), to_chunks(log_a)))
    return S_fin, jnp.moveaxis(o, 0, 1).reshape(bsz, t, GLA_HEADS, GLA_DV)


def merge_branches(x, oa, ob, da_gate, gla_gate, merge_logits, da_norm_g, gla_norm_g,
                   lam_init, w_branch_a, w_branch_b, w_out):
    bsz, t, _ = x.shape
    oa = rmsnorm(oa.astype(x.dtype), da_norm_g) * (1.0 - lam_init)
    oa = oa.reshape(bsz, t, DA_WIDTH) * jax.nn.silu(da_gate)
    ob = rmsnorm(ob.astype(x.dtype), gla_norm_g).reshape(bsz, t, GLA_VW) * jax.nn.silu(gla_gate)
    g_a, g_b = jnp.split(jax.nn.sigmoid(merge_logits), 2, axis=-1)
    mixed = g_a * (oa @ w_branch_a) + g_b * (ob @ w_branch_b)
    return x + mixed @ w_out


def setup_inputs(seed: int = 0) -> dict:
    key = jax.random.key(seed)
    ks = jax.random.split(key, 20)
    f32 = jnp.float32
    nrm = lambda k, shape, s: (jax.random.normal(k, shape, f32) * s)
    return {
        "x_prompt": nrm(ks[0], (BATCH, SEQ, D_MODEL), 1.0),
        "x_sample": nrm(ks[1], (DEC_BATCH, DEC_SEQ, D_MODEL), 1.0),
        "cache_k": nrm(ks[2], (DEPTH, DEC_BATCH, PAST_LEN, DA_HEADS, 2, DA_DK), 1.0),
        "cache_v": nrm(ks[3], (DEPTH, DEC_BATCH, PAST_LEN, DA_HEADS, DA_DV), 1.0),
        "state_gla": nrm(ks[4], (DEPTH, DEC_BATCH, GLA_HEADS, GLA_DK, GLA_DV), 1.0),
        "norm_in_g": 1.0 + nrm(ks[5], (DEPTH, D_MODEL), 0.02),
        "w_in": nrm(ks[6], (DEPTH, D_MODEL, D_IN), D_MODEL ** -0.5),
        "w_alpha_up": nrm(ks[7], (DEPTH, GLA_GATE_RANK, GLA_KW), GLA_GATE_RANK ** -0.5),
        "b_alpha": nrm(ks[8], (DEPTH, GLA_KW), 0.1),
        "lambda_q1": nrm(ks[9], (DEPTH, DA_DK), 0.1),
        "lambda_k1": nrm(ks[10], (DEPTH, DA_DK), 0.1),
        "lambda_q2": nrm(ks[11], (DEPTH, DA_DK), 0.1),
        "lambda_k2": nrm(ks[12], (DEPTH, DA_DK), 0.1),
        "da_norm_g": 1.0 + nrm(ks[13], (DEPTH, DA_DV), 0.02),
        "gla_norm_g": 1.0 + nrm(ks[14], (DEPTH, GLA_DV), 0.02),
        "w_branch_a": nrm(ks[15], (DEPTH, DA_WIDTH, D_MODEL), DA_WIDTH ** -0.5),
        "w_branch_b": nrm(ks[16], (DEPTH, GLA_VW, D_MODEL), GLA_VW ** -0.5),
        "w_out": nrm(ks[17], (DEPTH, D_MODEL, D_MODEL), D_MODEL ** -0.5),
        "norm_final_g": 1.0 + nrm(ks[18], (D_MODEL,), 0.02),
    }


def reference(x_prompt, x_sample, cache_k, cache_v, state_gla, norm_in_g, w_in, w_alpha_up, b_alpha,
              lambda_q1, lambda_k1, lambda_q2, lambda_k2, da_norm_g, gla_norm_g,
              w_branch_a, w_branch_b, w_out, norm_final_g):
    xp, xs = x_prompt, x_sample
    kp_all, vp_all, sp_all, ks_all, vs_all, ss_all = [], [], [], [], [], []
    for l in range(DEPTH):
        lam_init = 0.8 - 0.6 * math.exp(-0.3 * l)
        lam = diff_lambda(lambda_q1[l], lambda_k1[l], lambda_q2[l], lambda_k2[l], lam_init)

        dq, dk, dv, dg, gq, gk, gv, gg, log_a, mg = project(xp, norm_in_g[l], w_in[l], w_alpha_up[l], b_alpha[l])
        oa = diff_attend_prompt(dq, dk, dv, lam)
        s_p, ob = gla_prompt(gq, gk, gv, log_a)
        xp = merge_branches(xp, oa, ob, dg, gg, mg, da_norm_g[l], gla_norm_g[l], lam_init,
                            w_branch_a[l], w_branch_b[l], w_out[l])
        kp_all.append(dk)
        vp_all.append(dv)
        sp_all.append(s_p.astype(x_prompt.dtype))

        dq, dk, dv, dg, gq, gk, gv, gg, log_a, mg = project(xs, norm_in_g[l], w_in[l], w_alpha_up[l], b_alpha[l])
        k_full = jnp.concatenate([cache_k[l], dk], axis=1)
        v_full = jnp.concatenate([cache_v[l], dv], axis=1)
        oa = diff_attend(dq, k_full, v_full, lam, None)
        s_s, ob = gla_chunk(state_gla[l].astype(jnp.float32), gq, gk, gv, log_a)
        xs = merge_branches(xs, oa, ob, dg, gg, mg, da_norm_g[l], gla_norm_g[l], lam_init,
                            w_branch_a[l], w_branch_b[l], w_out[l])
        ks_all.append(dk)
        vs_all.append(dv)
        ss_all.append(s_s.astype(state_gla.dtype))

    y_prompt = rmsnorm(xp, norm_final_g)
    y_sample = rmsnorm(xs, norm_final_g)
    new_k_prompt = jnp.stack(kp_all)
    new_v_prompt = jnp.stack(vp_all)
    new_gla_prompt = jnp.stack(sp_all)
    new_k_sample = jnp.stack(ks_all)
    new_v_sample = jnp.stack(vs_all)
    new_gla_sample = jnp.stack(ss_all)
    return (y_prompt, y_sample, new_k_prompt, new_v_prompt, new_gla_prompt, new_k_sample, new_v_sample, new_gla_sample)
```

```python
import functools
import math

import numpy as np
import jax
import jax.numpy as jnp
from jax import lax
from jax.experimental import pallas as pl
from jax.experimental.pallas import tpu as pltpu

F32 = jnp.float32
BF16 = jnp.bfloat16

EPS = 1e-6
CHUNK = 64
GLA_TAU = 16.0
LANES = 128
VMEM_LIMIT_BYTES = 56 * 1024 * 1024

_NT = (((1,), (1,)), ((), ()))
_TN = (((0,), (0,)), ((), ()))


def _sigmoid(x):
    return 1.0 / (1.0 + jnp.exp(-x))


def _log_sigmoid(x):
    return jnp.minimum(x, 0.0) - jnp.log1p(jnp.exp(-jnp.abs(x)))


def _cparams(sem):
    return pltpu.CompilerParams(dimension_semantics=sem, vmem_limit_bytes=VMEM_LIMIT_BYTES)


def _proj_kernel(segs, x_ref, g_ref, w_ref, wa_ref, ba_ref,
                 q_ref, kf_ref, kb_ref, vf_ref, vb_ref, dg_ref, gq_ref, gk_ref, gv_ref, gg_ref,
                 mg_ref, la_ref):
    x = x_ref[...]
    ms = jnp.mean(x * x, axis=-1, keepdims=True)
    xn = ((x * lax.rsqrt(ms + EPS)) * g_ref[...]).astype(BF16)

    def seg(name):
        lo, hi = segs[name]
        return jnp.dot(xn, w_ref[:, lo:hi], preferred_element_type=F32)

    q_ref[...] = (seg("dq") * segs["da_scale"]).astype(BF16)
    k = seg("dk")
    kf_ref[...] = k
    kb_ref[...] = k.astype(BF16)
    v = seg("dv")
    vf_ref[...] = v
    vb_ref[...] = v.astype(BF16)
    dg_ref[...] = seg("dg")
    gq_ref[...] = seg("gq") * segs["gla_scale"]
    gk_ref[...] = seg("gk")
    gv_ref[...] = seg("gv").astype(BF16)
    gg_ref[...] = seg("gg")
    lo, hi = segs["mg"]
    step = 512
    for c in range(lo, hi, step):
        mg_ref[:, c - lo:c - lo + step] = jnp.dot(xn, w_ref[:, c:c + step], preferred_element_type=F32)
    ga = seg("ga").astype(BF16)
    z = jnp.dot(ga, wa_ref[...], preferred_element_type=F32) + ba_ref[...]
    la_ref[...] = _log_sigmoid(z) / GLA_TAU


def _proj(x2, g_in, wcat, wa_pad, b_alpha, segs, tm):
    tok, d = x2.shape
    n = wcat.shape[1]
    widths = {k: v[1] - v[0] for k, v in segs.items() if isinstance(v, tuple)}
    kw = wa_pad.shape[1]
    grid = (tok // tm,)

    def row_spec(w):
        return pl.BlockSpec((tm, w), lambda i: (i, 0))

    def full_spec(shape):
        return pl.BlockSpec(shape, lambda i: (0, 0))

    out_defs = [
        ("q", widths["dq"], BF16), ("kf", widths["dk"], F32), ("kb", widths["dk"], BF16),
        ("vf", widths["dv"], F32), ("vb", widths["dv"], BF16), ("dg", widths["dg"], F32),
        ("gq", widths["gq"], F32), ("gk", widths["gk"], F32), ("gv", widths["gv"], BF16),
        ("gg", widths["gg"], F32), ("mg", widths["mg"], F32), ("la", kw, F32),
    ]
    outs = pl.pallas_call(
        functools.partial(_proj_kernel, segs),
        grid=grid,
        in_specs=[row_spec(d), full_spec((1, d)),
                  pl.BlockSpec((d, n), lambda i: (0, 0), pipeline_mode=pl.Buffered(1)),
                  full_spec(wa_pad.shape), full_spec((1, kw))],
        out_specs=[row_spec(w) for _, w, _ in out_defs],
        out_shape=[jax.ShapeDtypeStruct((tok, w), dt) for _, w, dt in out_defs],
        compiler_params=_cparams(("arbitrary",)),
        name="proj",
    )(x2, g_in.reshape(1, d), wcat, wa_pad, b_alpha.reshape(1, kw))
    return dict(zip([nm for nm, _, _ in out_defs], outs))


def _diff_lambda(lamv, lam_init):
    a = jnp.sum(lamv[0:1, :] * lamv[1:2, :], axis=1, keepdims=True)
    b = jnp.sum(lamv[2:3, :] * lamv[3:4, :], axis=1, keepdims=True)
    return jnp.exp(a) - jnp.exp(b) + lam_init


def _stack_maps(q):
    lane = lax.broadcasted_iota(jnp.int32, q.shape, 1)
    half = q.shape[1] // 2
    zero = jnp.zeros_like(q)
    return jnp.concatenate([jnp.where(lane < half, q, zero), jnp.where(lane >= half, q, zero)], axis=0)


def _attn_prompt_kernel(lam_init, tq, lamv_ref, q_ref, k_ref, v_ref, o_ref, m_ref, acc_ref):
    i = pl.program_id(2)
    qx = _stack_maps(q_ref[0])
    m_ref[...] = jnp.full(m_ref.shape, -jnp.inf, F32)
    acc_ref[...] = jnp.zeros(acc_ref.shape, F32)
    ones = jnp.ones((tq, LANES), BF16)

    def step(j, masked):
        start = pl.multiple_of(j * tq, tq)
        kt = k_ref[0, pl.ds(start, tq), :]
        vt = v_ref[0, pl.ds(start, tq), :]
        s = lax.dot_general(qx, kt, _NT, preferred_element_type=F32)
        if masked:
            row = lax.broadcasted_iota(jnp.int32, s.shape, 0)
            col = lax.broadcasted_iota(jnp.int32, s.shape, 1)
            vis = (col // CHUNK) <= ((row % tq) // CHUNK)
            s = jnp.where(vis, s, -jnp.inf)
        m_prev = m_ref[...]
        m_new = jnp.maximum(m_prev, jnp.max(s, axis=1, keepdims=True))
        alpha = jnp.exp(m_prev - m_new)
        p = jnp.exp(s - jnp.tile(m_new, (1, tq // LANES)))
        pv = jnp.dot(p.astype(BF16), jnp.concatenate([vt, ones], axis=1), preferred_element_type=F32)
        acc_ref[...] = acc_ref[...] * jnp.tile(alpha, (1, 2)) + pv
        m_ref[...] = m_new

    lax.fori_loop(0, i, lambda j, c: (step(j, False), c)[1], 0)
    step(i, True)

    acc = acc_ref[...]
    o1 = acc[:tq, :LANES] / acc[:tq, LANES:]
    o2 = acc[tq:, :LANES] / acc[tq:, LANES:]
    o_ref[0] = o1 - _diff_lambda(lamv_ref[...], lam_init) * o2


def _attn_prompt(q, k, v, lamv, lam_init, heads):
    bsz, t, w = q.shape
    tq = min(512, t)
    nq = t // tq
    return pl.pallas_call(
        functools.partial(_attn_prompt_kernel, lam_init, tq),
        grid=(bsz, heads, nq),
        in_specs=[pl.BlockSpec(lamv.shape, lambda b, h, i: (0, 0)),
                  pl.BlockSpec((1, tq, LANES), lambda b, h, i: (b, i, h)),
                  pl.BlockSpec((1, t, LANES), lambda b, h, i: (b, 0, h)),
                  pl.BlockSpec((1, t, LANES), lambda b, h, i: (b, 0, h))],
        out_specs=pl.BlockSpec((1, tq, LANES), lambda b, h, i: (b, i, h)),
        out_shape=jax.ShapeDtypeStruct((bsz, t, w), F32),
        scratch_shapes=[pltpu.VMEM((2 * tq, LANES), F32), pltpu.VMEM((2 * tq, 2 * LANES), F32)],
        compiler_params=_cparams(("arbitrary", "arbitrary", "arbitrary")),
        name="attn_prompt",
    )(lamv, q, k, v)


def _attn_sample_kernel(lam_init, heads, lamv_ref, q_ref, ck_ref, cv_ref, kn_ref, vn_ref, o_ref):
    tq = q_ref.shape[1]
    lam = _diff_lambda(lamv_ref[...], lam_init)
    for h in range(heads):
        sl = slice(h * LANES, (h + 1) * LANES)
        qx = _stack_maps(q_ref[0, :, sl])
        kc = ck_ref[0, :, sl].astype(BF16)
        vc = cv_ref[0, :, sl].astype(BF16)
        kn = kn_ref[0, :, sl]
        vn = vn_ref[0, :, sl]
        sc = lax.dot_general(qx, kc, _NT, preferred_element_type=F32)
        sn = lax.dot_general(qx, kn, _NT, preferred_element_type=F32)
        m = jnp.maximum(jnp.max(sc, axis=1, keepdims=True), jnp.max(sn, axis=1, keepdims=True))
        ec = jnp.exp(sc - m)
        en = jnp.exp(sn - m)
        l = jnp.sum(ec, axis=1, keepdims=True) + jnp.sum(en, axis=1, keepdims=True)
        o = (jnp.dot(ec.astype(BF16), vc, preferred_element_type=F32)
             + jnp.dot(en.astype(BF16), vn, preferred_element_type=F32)) / l
        o_ref[0, :, sl] = o[:tq] - lam * o[tq:]


def _attn_sample(q, ck, cv, kn, vn, lamv, lam_init, heads):
    bsz, tq, w = q.shape
    p = ck.shape[1]
    new_spec = pl.BlockSpec((1, tq, w), lambda b: (b, 0, 0))
    old_spec = pl.BlockSpec((1, p, w), lambda b: (b, 0, 0))
    return pl.pallas_call(
        functools.partial(_attn_sample_kernel, lam_init, heads),
        grid=(bsz,),
        in_specs=[pl.BlockSpec(lamv.shape, lambda b: (0, 0)), new_spec, old_spec, old_spec, new_spec, new_spec],
        out_specs=new_spec,
        out_shape=jax.ShapeDtypeStruct((bsz, tq, w), F32),
        compiler_params=_cparams(("arbitrary",)),
        name="attn_sample",
    )(lamv, q, ck, cv, kn, vn)


def _gla_tables(c, heads):
    nlev = int(math.log2(c))
    t = np.arange(c)[:, None]
    u = np.arange(c)[None, :]
    mats = [(u <= t)]
    lev = np.where(t == u, 0, -1)
    for l in range(1, nlev + 1):
        n = c >> l
        r = (t // (2 * n)) * (2 * n) + n - 1
        upper = (t % (2 * n)) >= n
        mats.append(upper & (u > r) & (u <= t))
        mats.append((~upper) & (u > t) & (u <= r))
        same_block = (t // (2 * n)) == (u // (2 * n))
        lev = np.where(same_block & upper & ((u % (2 * n)) < n), l, lev)
    sel = np.concatenate(mats, axis=0).astype(np.float32)
    sel3 = np.concatenate([sel, sel, sel], axis=1)
    return sel3, np.tile(lev, (1, heads)).astype(np.int32), nlev


def _gla_kernel(c, nchunk, heads, nlev, sel_ref, lev_ref, q_ref, k_ref, v_ref, la_ref, s0_ref,
                o_ref, sout_ref, s_ref):
    ti = pl.program_id(1)
    kw = q_ref.shape[-1]
    vw = v_ref.shape[-1]
    dk = kw // heads
    dv = vw // heads

    row_h = lax.broadcasted_iota(jnp.int32, (kw, vw), 0) // dk
    col_h = lax.broadcasted_iota(jnp.int32, (kw, vw), 1) // dv
    diag = row_h == col_h

    @pl.when(ti == 0)
    def _():
        s_ref[...] = jnp.zeros(s_ref.shape, F32)
        for h in range(heads):
            s_ref[h * dk:(h + 1) * dk, h * dv:(h + 1) * dv] = s0_ref[0, h]

    sel = sel_ref[...]
    lev = lev_ref[...]
    klane_h = lax.broadcasted_iota(jnp.int32, (c, kw), 1) // dk
    vlane_h = lax.broadcasted_iota(jnp.int32, (c, vw), 1) // dv
    ones = jnp.ones((3 * c, LANES), BF16)

    for ci in range(nchunk):
        rows = slice(ci * c, (ci + 1) * c)
        q = q_ref[0, rows, :]
        k = k_ref[0, rows, :]
        v = v_ref[0, rows, :]
        la = la_ref[0, rows, :]
        p0 = la.astype(BF16)
        r0 = la - p0.astype(F32)
        p1 = r0.astype(BF16)
        p2 = (r0 - p1.astype(F32)).astype(BF16)
        pieces = jnp.concatenate([p0, p1, p2], axis=0)
        sums = jnp.dot(sel, pieces, preferred_element_type=F32)
        b = sums[0:c]
        b_last = b[c - 1:c, :]

        att = None
        for l in range(nlev + 1):
            if l == 0:
                qt, kt = q.astype(BF16), k.astype(BF16)
            else:
                qt = (q * jnp.exp(sums[(2 * l - 1) * c:2 * l * c])).astype(BF16)
                kt = (k * jnp.exp(sums[2 * l * c:(2 * l + 1) * c])).astype(BF16)
            kstack = jnp.concatenate(
                [jnp.where(klane_h == h, kt, jnp.zeros_like(kt)) for h in range(heads)], axis=0)
            a = lax.dot_general(qt, kstack, _NT, preferred_element_type=F32)
            a = jnp.where(lev == l, a, 0.0)
            att = a if att is None else att + a
        vbd = jnp.concatenate(
            [jnp.where(vlane_h == h, v, jnp.zeros_like(v)) for h in range(heads)], axis=0)
        o = jnp.dot(att.astype(BF16), vbd, preferred_element_type=F32)
        s_old = s_ref[...]
        o = o + jnp.dot((q * jnp.exp(b)).astype(BF16), s_old.astype(BF16), preferred_element_type=F32)
        o_ref[0, rows, :] = o

        kd = (k * jnp.exp(b_last - b)).astype(BF16)
        upd = lax.dot_general(kd, v, _TN, preferred_element_type=F32)
        b_col = lax.dot_general(pieces, ones, _TN, preferred_element_type=F32)
        decay = jnp.tile(jnp.exp(b_col), (1, vw // LANES))
        s_ref[...] = decay * s_old + jnp.where(diag, upd, 0.0)

    @pl.when(ti == pl.num_programs(1) - 1)
    def _():
        for h in range(heads):
            sout_ref[0, h] = s_ref[h * dk:(h + 1) * dk, h * dv:(h + 1) * dv]


def _gla(q, k, v, la, s0, chunk):
    bsz, t, kw = q.shape
    vw = v.shape[-1]
    heads, dk, dv = s0.shape[1:]
    sel3, lev, nlev = _gla_tables(chunk, heads)
    nchunk = min(4, t // chunk)
    tg = nchunk * chunk
    tok = lambda w: pl.BlockSpec((1, tg, w), lambda b, i: (b, i, 0))
    st = pl.BlockSpec((1, heads, dk, dv), lambda b, i: (b, 0, 0, 0))
    return pl.pallas_call(
        functools.partial(_gla_kernel, chunk, nchunk, heads, nlev),
        grid=(bsz, t // tg),
        in_specs=[pl.BlockSpec(sel3.shape, lambda b, i: (0, 0)), pl.BlockSpec(lev.shape, lambda b, i: (0, 0)),
                  tok(kw), tok(kw), tok(vw), tok(kw), st],
        out_specs=[tok(vw), st],
        out_shape=[jax.ShapeDtypeStruct((bsz, t, vw), F32), jax.ShapeDtypeStruct(s0.shape, F32)],
        scratch_shapes=[pltpu.VMEM((kw, vw), F32)],
        compiler_params=_cparams(("arbitrary", "arbitrary")),
        name="gla",
    )(jnp.asarray(sel3, BF16), jnp.asarray(lev), q, k, v, la, s0)


def _head_rmsnorm(x, g, heads):
    outs = []
    for h in range(heads):
        blk = x[:, h * LANES:(h + 1) * LANES]
        ms = jnp.mean(blk * blk, axis=-1, keepdims=True)
        outs.append((blk * lax.rsqrt(ms + EPS)) * g)
    return jnp.concatenate(outs, axis=1)


def _merge_kernel(lam_init, final, da_heads, gla_heads, x_ref, oa_ref, ob_ref, dg_ref, gg_ref, mg_ref,
                  ga_ref, gb_ref, wa_ref, wb_ref, wo_ref, gf_ref, y_ref):
    d = x_ref.shape[-1]
    dg = dg_ref[...]
    gg = gg_ref[...]
    oa = (_head_rmsnorm(oa_ref[...], ga_ref[...], da_heads) * (1.0 - lam_init)) * (dg * _sigmoid(dg))
    ob = _head_rmsnorm(ob_ref[...], gb_ref[...], gla_heads) * (gg * _sigmoid(gg))
    ya = jnp.dot(oa.astype(BF16), wa_ref[...], preferred_element_type=F32)
    yb = jnp.dot(ob.astype(BF16), wb_ref[...], preferred_element_type=F32)
    mixed = _sigmoid(mg_ref[:, :d]) * ya + _sigmoid(mg_ref[:, d:]) * yb
    y = x_ref[...] + jnp.dot(mixed.astype(BF16), wo_ref[...], preferred_element_type=F32)
    if final:
        ms = jnp.mean(y * y, axis=-1, keepdims=True)
        y = (y * lax.rsqrt(ms + EPS)) * gf_ref[...]
    y_ref[...] = y


def _merge(x2, oa, ob, dg, gg, mg, da_g, gla_g, wa, wb, wo, gf, lam_init, final, da_heads, gla_heads, tm):
    tok, d = x2.shape
    row = lambda w: pl.BlockSpec((tm, w), lambda i: (i, 0))
    full = lambda a: pl.BlockSpec(a.shape, lambda i: (0, 0))
    da_g = da_g.reshape(1, -1)
    gla_g = gla_g.reshape(1, -1)
    gf = gf.reshape(1, d)
    return pl.pallas_call(
        functools.partial(_merge_kernel, lam_init, final, da_heads, gla_heads),
        grid=(tok // tm,),
        in_specs=[row(d), row(oa.shape[1]), row(ob.shape[1]), row(dg.shape[1]), row(gg.shape[1]), row(mg.shape[1]),
                  full(da_g), full(gla_g), full(wa), full(wb), full(wo), full(gf)],
        out_specs=row(d),
        out_shape=jax.ShapeDtypeStruct((tok, d), F32),
        compiler_params=_cparams(("arbitrary",)),
        name="merge",
    )(x2, oa, ob, dg, gg, mg, da_g, gla_g, wa, wb, wo, gf)


def _layout(da_heads, da_dk, gla_heads, gla_dk, gla_dv, rank, d_model):
    da_qk = da_heads * 2 * da_dk
    da_w = da_heads * 2 * da_dk
    kwid = gla_heads * gla_dk
    vwid = gla_heads * gla_dv
    names = ["dq", "dk", "dv", "dg", "gq", "gk", "gv", "gg", "mg", "ga"]
    sizes = [da_qk, da_qk, da_w, da_w, kwid, kwid, vwid, vwid, 2 * d_model, LANES]
    segs, off = {}, 0
    for nm, sz in zip(names, sizes):
        segs[nm] = (off, off + sz)
        off += sz
    segs["da_scale"] = da_dk ** -0.5
    segs["gla_scale"] = gla_dk ** -0.5
    return segs, off


def kernel(x_prompt, x_sample, cache_k, cache_v, state_gla, norm_in_g, w_in, w_alpha_up, b_alpha,
           lambda_q1, lambda_k1, lambda_q2, lambda_k2, da_norm_g, gla_norm_g,
           w_branch_a, w_branch_b, w_out, norm_final_g):
    depth = w_in.shape[0]
    bsz, seq, d = x_prompt.shape
    dbsz, dseq, _ = x_sample.shape
    past = cache_k.shape[2]
    da_heads, da_dk = cache_k.shape[3], cache_k.shape[5]
    gla_heads, gla_dk, gla_dv = state_gla.shape[2:]
    rank = w_alpha_up.shape[1]
    kw = gla_heads * gla_dk
    assert 2 * da_dk == LANES and cache_v.shape[4] == LANES and gla_dv == LANES
    segs, ncat = _layout(da_heads, da_dk, gla_heads, gla_dk, gla_dv, rank, d)
    ga_lo = segs["gg"][1]

    xp = x_prompt.reshape(bsz * seq, d)
    xs = x_sample.reshape(dbsz * dseq, d)
    tm_p = min(512, bsz * seq)
    tm_s = min(512, dbsz * dseq)
    outs = [[] for _ in range(6)]
    for l in range(depth):
        lam_init = 0.8 - 0.6 * math.exp(-0.3 * l)
        final = l == depth - 1
        w = w_in[l]
        wcat = jnp.concatenate(
            [w[:, :ga_lo], w[:, ga_lo + rank:], w[:, ga_lo:ga_lo + rank], jnp.zeros((d, LANES - rank), w.dtype)],
            axis=1).astype(BF16)
        wa_pad = jnp.concatenate([w_alpha_up[l], jnp.zeros((LANES - rank, kw), F32)], axis=0).astype(BF16)
        lamv = jnp.stack([lambda_q1[l], lambda_k1[l], lambda_q2[l], lambda_k2[l]])
        wa, wb, wo = w_branch_a[l].astype(BF16), w_branch_b[l].astype(BF16), w_out[l].astype(BF16)

        p = _proj(xp, norm_in_g[l], wcat, wa_pad, b_alpha[l], segs, tm_p)
        sh = lambda a: a.reshape(bsz, seq, a.shape[-1])
        oa = _attn_prompt(sh(p["q"]), sh(p["kb"]), sh(p["vb"]), lamv, lam_init, da_heads)
        ob, s_p = _gla(sh(p["gq"]), sh(p["gk"]), sh(p["gv"]), sh(p["la"]),
                       jnp.zeros((bsz, gla_heads, gla_dk, gla_dv), F32), CHUNK)
        xp = _merge(xp, oa.reshape(bsz * seq, -1), ob.reshape(bsz * seq, -1), p["dg"], p["gg"], p["mg"],
                    da_norm_g[l], gla_norm_g[l], wa, wb, wo, norm_final_g, lam_init, final,
                    da_heads, gla_heads, min(256, tm_p))
        outs[0].append(p["kf"].reshape(bsz, seq, da_heads, 2, da_dk))
        outs[1].append(p["vf"].reshape(bsz, seq, da_heads, 2 * da_dk))
        outs[2].append(s_p)

        p = _proj(xs, norm_in_g[l], wcat, wa_pad, b_alpha[l], segs, tm_s)
        sh = lambda a: a.reshape(dbsz, dseq, a.shape[-1])
        oa = _attn_sample(sh(p["q"]), cache_k[l].reshape(dbsz, past, -1), cache_v[l].reshape(dbsz, past, -1),
                          sh(p["kb"]), sh(p["vb"]), lamv, lam_init, da_heads)
        ob, s_s = _gla(sh(p["gq"]), sh(p["gk"]), sh(p["gv"]), sh(p["la"]), state_gla[l], dseq)
        xs = _merge(xs, oa.reshape(dbsz * dseq, -1), ob.reshape(dbsz * dseq, -1), p["dg"], p["gg"], p["mg"],
                    da_norm_g[l], gla_norm_g[l], wa, wb, wo, norm_final_g, lam_init, final,
                    da_heads, gla_heads, min(256, tm_s))
        outs[3].append(p["kf"].reshape(dbsz, dseq, da_heads, 2, da_dk))
        outs[4].append(p["vf"].reshape(dbsz, dseq, da_heads, 2 * da_dk))
        outs[5].append(s_s)

    return (xp.reshape(bsz, seq, d), xs.reshape(dbsz, dseq, d),
            jnp.stack(outs[0]), jnp.stack(outs[1]), jnp.stack(outs[2]),
            jnp.stack(outs[3]), jnp.stack(outs[4]), jnp.stack(outs[5]))
```

```python
import functools
import math

import numpy as np
import jax
import jax.numpy as jnp
from jax import lax
from jax.experimental import pallas as pl
from jax.experimental.pallas import tpu as pltpu

F32 = jnp.float32
BF16 = jnp.bfloat16

EPS = 1e-6
CHUNK = 64
GLA_TAU = 16.0
LANES = 128
VMEM_LIMIT_BYTES = 56 * 1024 * 1024

_NT = (((1,), (1,)), ((), ()))
_TN = (((0,), (0,)), ((), ()))


def _sigmoid(x):
    return 1.0 / (1.0 + jnp.exp(-x))


def _log_sigmoid(x):
    return jnp.minimum(x, 0.0) - jnp.log1p(jnp.exp(-jnp.abs(x)))


def _cparams(sem):
    return pltpu.CompilerParams(dimension_semantics=sem, vmem_limit_bytes=VMEM_LIMIT_BYTES)


def _proj_kernel(segs, heads, k_major, x_ref, g_ref, w_ref, wk_ref, wa_ref, ba_ref,
                 q_ref, kf_ref, kb_ref, vf_ref, vb_ref, dg_ref, gq_ref, gk_ref, gv_ref, gg_ref,
                 mg_ref, la_ref):
    x = x_ref[...]
    tm = x.shape[0]
    ms = jnp.mean(x * x, axis=-1, keepdims=True)
    xn = ((x * lax.rsqrt(ms + EPS)) * g_ref[...]).astype(BF16)

    def seg(name):
        lo, hi = segs[name]
        return jnp.dot(xn, w_ref[:, lo:hi], preferred_element_type=F32)

    q_ref[...] = (seg("dq") * segs["da_scale"]).astype(BF16)
    if k_major:
        kt = lax.dot_general(wk_ref[...], xn, _NT, preferred_element_type=F32)
        kf_ref[0] = kt
        kb_ref[0] = kt.astype(BF16)
    else:
        k = seg("dk")
        kf_ref[...] = k
        kb_ref[...] = k.astype(BF16)
    v = seg("dv")
    vb_ref[...] = v.astype(BF16)
    if k_major:
        for h in range(heads):
            vf_ref[pl.ds(h, tm, stride=heads), :] = v[:, h * LANES:(h + 1) * LANES]
    else:
        vf_ref[...] = v
    dg_ref[...] = seg("dg")
    gq_ref[...] = seg("gq") * segs["gla_scale"]
    gk_ref[...] = seg("gk")
    gv_ref[...] = seg("gv").astype(BF16)
    gg_ref[...] = seg("gg")
    lo, hi = segs["mg"]
    step = 512
    for c in range(lo, hi, step):
        mg_ref[:, c - lo:c - lo + step] = jnp.dot(xn, w_ref[:, c:c + step], preferred_element_type=F32)
    ga = seg("ga").astype(BF16)
    z = jnp.dot(ga, wa_ref[...], preferred_element_type=F32) + ba_ref[...]
    la_ref[...] = _log_sigmoid(z) / GLA_TAU


def _proj(x2, g_in, wcat, wkt, wa_pad, b_alpha, segs, tm, heads, seq=None):
    tok, d = x2.shape
    n = wcat.shape[1]
    widths = {k: v[1] - v[0] for k, v in segs.items() if isinstance(v, tuple)}
    kw = wa_pad.shape[1]
    k_major = seq is not None

    def row_spec(w):
        return pl.BlockSpec((tm, w), lambda i: (i, 0))

    def full_spec(shape):
        return pl.BlockSpec(shape, lambda i: (0, 0))

    def const_spec(shape):
        return pl.BlockSpec(shape, lambda i: (0, 0), pipeline_mode=pl.Buffered(1))

    out_defs = [
        ("q", widths["dq"], BF16), ("kf", widths["dk"], F32), ("kb", widths["dk"], BF16),
        ("vf", widths["dv"], F32), ("vb", widths["dv"], BF16), ("dg", widths["dg"], F32),
        ("gq", widths["gq"], F32), ("gk", widths["gk"], F32), ("gv", widths["gv"], BF16),
        ("gg", widths["gg"], F32), ("mg", widths["mg"], F32), ("la", kw, F32),
    ]
    out_specs = {nm: row_spec(w) for nm, w, _ in out_defs}
    out_shape = {nm: jax.ShapeDtypeStruct((tok, w), dt) for nm, w, dt in out_defs}
    if k_major:
        nt = seq // tm
        for nm, dt in (("kf", F32), ("kb", BF16)):
            out_specs[nm] = pl.BlockSpec((1, widths["dk"], tm), lambda i: (i // nt, 0, i % nt))
            out_shape[nm] = jax.ShapeDtypeStruct((tok // seq, widths["dk"], seq), dt)
        out_specs["vf"] = pl.BlockSpec((tm * heads, LANES), lambda i: (i, 0))
        out_shape["vf"] = jax.ShapeDtypeStruct((tok * heads, LANES), F32)
    names = [nm for nm, _, _ in out_defs]
    outs = pl.pallas_call(
        functools.partial(_proj_kernel, segs, heads, k_major),
        grid=(tok // tm,),
        in_specs=[row_spec(d), full_spec((1, d)), const_spec((d, n)), const_spec(wkt.shape),
                  full_spec(wa_pad.shape), full_spec((1, kw))],
        out_specs=[out_specs[nm] for nm in names],
        out_shape=[out_shape[nm] for nm in names],
        compiler_params=_cparams(("arbitrary",)),
        name="proj",
    )(x2, g_in.reshape(1, d), wcat, wkt, wa_pad, b_alpha.reshape(1, kw))
    return dict(zip(names, outs))


def _diff_lambda(lamv, lam_init):
    a = jnp.sum(lamv[0:1, :] * lamv[1:2, :], axis=1, keepdims=True)
    b = jnp.sum(lamv[2:3, :] * lamv[3:4, :], axis=1, keepdims=True)
    return jnp.exp(a) - jnp.exp(b) + lam_init


def _stack_maps(q):
    lane = lax.broadcasted_iota(jnp.int32, q.shape, 1)
    half = q.shape[1] // 2
    zero = jnp.zeros_like(q)
    return jnp.concatenate([jnp.where(lane < half, q, zero), jnp.where(lane >= half, q, zero)], axis=0)


def _attn_prompt_kernel(lam_init, tq, tk, rb, lamv_ref, q_ref, kt_ref, v_ref, o_ref,
                        qx_ref, m_ref, al_ref, acc_ref, s0_ref, s1_ref):
    i = pl.program_id(2)
    qx_ref[...] = _stack_maps(q_ref[0])
    m_ref[...] = jnp.full(m_ref.shape, -jnp.inf, F32)
    acc_ref[...] = jnp.zeros(acc_ref.shape, F32)
    ones = jnp.ones((tk, LANES), BF16)
    assert tq == tk
    sbuf = (s0_ref, s1_ref)
    nblk = 2 * tq // rb

    def scores(j, slot):
        start = pl.multiple_of(j * tk, tk)
        kt = kt_ref[0, :, pl.ds(start, tk)]
        sbuf[slot][...] = jnp.dot(qx_ref[...], kt, preferred_element_type=F32)

    def accumulate(j, slot, masked):
        start = pl.multiple_of(j * tk, tk)
        vx = jnp.concatenate([v_ref[0, pl.ds(start, tk), :], ones], axis=1)

        def load(r):
            s = sbuf[slot][r * rb:(r + 1) * rb, :]
            if masked:
                qc = ((r * rb) % tq + lax.broadcasted_iota(jnp.int32, s.shape, 0)) // CHUNK
                kc = lax.broadcasted_iota(jnp.int32, s.shape, 1) // CHUNK
                s = jnp.where(kc <= qc, s, -jnp.inf)
            return s

        for r in range(nblk):
            rows = slice(r * rb, (r + 1) * rb)
            m_prev = m_ref[rows, :]
            m_new = jnp.maximum(m_prev, jnp.max(load(r), axis=1, keepdims=True))
            al_ref[rows, :] = jnp.exp2(m_prev - m_new)
            m_ref[rows, :] = m_new
        for r in range(nblk):
            rows = slice(r * rb, (r + 1) * rb)
            p = jnp.exp2(load(r) - jnp.tile(m_ref[rows, :], (1, tk // LANES)))
            pv = jnp.dot(p.astype(BF16), vx, preferred_element_type=F32)
            acc_ref[rows, :] = acc_ref[rows, :] * jnp.tile(al_ref[rows, :], (1, 2)) + pv

    def pair(jj, c):
        scores(2 * jj + 1, 1)
        accumulate(2 * jj, 0, False)
        scores(2 * jj + 2, 0)
        accumulate(2 * jj + 1, 1, False)
        return c

    scores(0, 0)
    lax.fori_loop(0, i // 2, pair, 0)

    @pl.when(i % 2 == 0)
    def _():
        accumulate(i, 0, True)

    @pl.when(i % 2 == 1)
    def _():
        scores(i, 1)
        accumulate(i - 1, 0, False)
        accumulate(i, 1, True)

    acc = acc_ref[...]
    o1 = acc[:tq, :LANES] / acc[:tq, LANES:]
    o2 = acc[tq:, :LANES] / acc[tq:, LANES:]
    o_ref[0] = o1 - _diff_lambda(lamv_ref[...], lam_init) * o2


def _attn_prompt(q, kt, v, lamv, lam_init, heads):
    bsz, t, w = q.shape
    tq = tk = min(512, t)
    rb = min(256, tq)
    return pl.pallas_call(
        functools.partial(_attn_prompt_kernel, lam_init, tq, tk, rb),
        grid=(bsz, heads, t // tq),
        in_specs=[pl.BlockSpec(lamv.shape, lambda b, h, i: (0, 0)),
                  pl.BlockSpec((1, tq, LANES), lambda b, h, i: (b, i, h)),
                  pl.BlockSpec((1, LANES, t), lambda b, h, i: (b, h, 0)),
                  pl.BlockSpec((1, t, LANES), lambda b, h, i: (b, 0, h))],
        out_specs=pl.BlockSpec((1, tq, LANES), lambda b, h, i: (b, i, h)),
        out_shape=jax.ShapeDtypeStruct((bsz, t, w), F32),
        scratch_shapes=[pltpu.VMEM((2 * tq, LANES), BF16), pltpu.VMEM((2 * tq, LANES), F32),
                        pltpu.VMEM((2 * tq, LANES), F32), pltpu.VMEM((2 * tq, 2 * LANES), F32),
                        pltpu.VMEM((2 * tq, tk), F32), pltpu.VMEM((2 * tq, tk), F32)],
        compiler_params=_cparams(("arbitrary", "arbitrary", "arbitrary")),
        name="attn_prompt",
    )(lamv, q, kt, v)


def _attn_sample_kernel(lam_init, heads, lamv_ref, q_ref, ckt_ref, cv_ref, kn_ref, vn_ref, o_ref):
    tq = q_ref.shape[1]
    past = ckt_ref.shape[2]
    lam = _diff_lambda(lamv_ref[...], lam_init)
    for h in range(heads):
        sl = slice(h * LANES, (h + 1) * LANES)
        qx = _stack_maps(q_ref[0, :, sl])
        kc = ckt_ref[0, sl, :].astype(BF16)
        vc = cv_ref[0, pl.ds(h, past, stride=heads), :].astype(BF16)
        kn = kn_ref[0, :, sl]
        vn = vn_ref[0, :, sl]
        sc = jnp.dot(qx, kc, preferred_element_type=F32)
        sn = lax.dot_general(qx, kn, _NT, preferred_element_type=F32)
        m = jnp.maximum(jnp.max(sc, axis=1, keepdims=True), jnp.max(sn, axis=1, keepdims=True))
        ec = jnp.exp2(sc - m)
        en = jnp.exp2(sn - m)
        l = jnp.sum(ec, axis=1, keepdims=True) + jnp.sum(en, axis=1, keepdims=True)
        o = (jnp.dot(ec.astype(BF16), vc, preferred_element_type=F32)
             + jnp.dot(en.astype(BF16), vn, preferred_element_type=F32)) / l
        o_ref[0, :, sl] = o[:tq] - lam * o[tq:]


def _attn_sample(q, ckt, cv, kn, vn, lamv, lam_init, heads):
    bsz, tq, w = q.shape
    new_spec = pl.BlockSpec((1, tq, w), lambda b: (b, 0, 0))
    return pl.pallas_call(
        functools.partial(_attn_sample_kernel, lam_init, heads),
        grid=(bsz,),
        in_specs=[pl.BlockSpec(lamv.shape, lambda b: (0, 0)), new_spec,
                  pl.BlockSpec((1,) + ckt.shape[1:], lambda b: (b, 0, 0)),
                  pl.BlockSpec((1,) + cv.shape[1:], lambda b: (b, 0, 0)), new_spec, new_spec],
        out_specs=new_spec,
        out_shape=jax.ShapeDtypeStruct((bsz, tq, w), F32),
        compiler_params=_cparams(("arbitrary",)),
        name="attn_sample",
    )(lamv, q, ckt, cv, kn, vn)


def _gla_tables(c, heads):
    nlev = int(math.log2(c))
    t = np.arange(c)[:, None]
    u = np.arange(c)[None, :]
    mats = [(u <= t)]
    lev = np.where(t == u, 0, -1)
    for l in range(1, nlev + 1):
        n = c >> l
        r = (t // (2 * n)) * (2 * n) + n - 1
        upper = (t % (2 * n)) >= n
        mats.append(upper & (u > r) & (u <= t))
        mats.append((~upper) & (u > t) & (u <= r))
        same_block = (t // (2 * n)) == (u // (2 * n))
        lev = np.where(same_block & upper & ((u % (2 * n)) < n), l, lev)
    sel = np.concatenate(mats, axis=0).astype(np.float32)
    sel3 = np.concatenate([sel, sel, sel], axis=1)
    return sel3, np.tile(lev, (1, heads)).astype(np.int32), nlev


def _gla_kernel(c, nchunk, heads, nlev, sel_ref, lev_ref, q_ref, k_ref, v_ref, la_ref, s0_ref,
                o_ref, sout_ref, s_ref):
    ti = pl.program_id(1)
    kw = q_ref.shape[-1]
    vw = v_ref.shape[-1]
    dk = kw // heads
    dv = vw // heads

    row_h = lax.broadcasted_iota(jnp.int32, (kw, vw), 0) // dk
    col_h = lax.broadcasted_iota(jnp.int32, (kw, vw), 1) // dv
    diag = row_h == col_h

    @pl.when(ti == 0)
    def _():
        s_ref[...] = jnp.zeros(s_ref.shape, F32)
        for h in range(heads):
            s_ref[h * dk:(h + 1) * dk, h * dv:(h + 1) * dv] = s0_ref[0, h]

    sel = sel_ref[...]
    lev = lev_ref[...]
    klane_h = lax.broadcasted_iota(jnp.int32, (c, kw), 1) // dk
    vlane_h = lax.broadcasted_iota(jnp.int32, (c, vw), 1) // dv
    ones = jnp.ones((3 * c, LANES), BF16)

    for ci in range(nchunk):
        rows = slice(ci * c, (ci + 1) * c)
        q = q_ref[0, rows, :]
        k = k_ref[0, rows, :]
        v = v_ref[0, rows, :]
        la = la_ref[0, rows, :]
        p0 = la.astype(BF16)
        r0 = la - p0.astype(F32)
        p1 = r0.astype(BF16)
        p2 = (r0 - p1.astype(F32)).astype(BF16)
        pieces = jnp.concatenate([p0, p1, p2], axis=0)
        sums = jnp.dot(sel, pieces, preferred_element_type=F32)
        b = sums[0:c]
        b_last = b[c - 1:c, :]

        att = None
        for l in range(nlev + 1):
            if l == 0:
                qt, kt = q.astype(BF16), k.astype(BF16)
            else:
                qt = (q * jnp.exp(sums[(2 * l - 1) * c:2 * l * c])).astype(BF16)
                kt = (k * jnp.exp(sums[2 * l * c:(2 * l + 1) * c])).astype(BF16)
            kstack = jnp.concatenate(
                [jnp.where(klane_h == h, kt, jnp.zeros_like(kt)) for h in range(heads)], axis=0)
            a = lax.dot_general(qt, kstack, _NT, preferred_element_type=F32)
            a = jnp.where(lev == l, a, 0.0)
            att = a if att is None else att + a
        vbd = jnp.concatenate(
            [jnp.where(vlane_h == h, v, jnp.zeros_like(v)) for h in range(heads)], axis=0)
        o = jnp.dot(att.astype(BF16), vbd, preferred_element_type=F32)
        s_old = s_ref[...]
        o = o + jnp.dot((q * jnp.exp(b)).astype(BF16), s_old.astype(BF16), preferred_element_type=F32)
        o_ref[0, rows, :] = o

        kd = (k * jnp.exp(b_last - b)).astype(BF16)
        upd = lax.dot_general(kd, v, _TN, preferred_element_type=F32)
        b_col = lax.dot_general(pieces, ones, _TN, preferred_element_type=F32)
        decay = jnp.tile(jnp.exp(b_col), (1, vw // LANES))
        s_ref[...] = decay * s_old + jnp.where(diag, upd, 0.0)

    @pl.when(ti == pl.num_programs(1) - 1)
    def _():
        for h in range(heads):
            sout_ref[0, h] = s_ref[h * dk:(h + 1) * dk, h * dv:(h + 1) * dv]


def _gla(q, k, v, la, s0, chunk):
    bsz, t, kw = q.shape
    vw = v.shape[-1]
    heads, dk, dv = s0.shape[1:]
    sel3, lev, nlev = _gla_tables(chunk, heads)
    nchunk = min(4, t // chunk)
    tg = nchunk * chunk
    tok = lambda w: pl.BlockSpec((1, tg, w), lambda b, i: (b, i, 0))
    st = pl.BlockSpec((1, heads, dk, dv), lambda b, i: (b, 0, 0, 0))
    return pl.pallas_call(
        functools.partial(_gla_kernel, chunk, nchunk, heads, nlev),
        grid=(bsz, t // tg),
        in_specs=[pl.BlockSpec(sel3.shape, lambda b, i: (0, 0)), pl.BlockSpec(lev.shape, lambda b, i: (0, 0)),
                  tok(kw), tok(kw), tok(vw), tok(kw), st],
        out_specs=[tok(vw), st],
        out_shape=[jax.ShapeDtypeStruct((bsz, t, vw), F32), jax.ShapeDtypeStruct(s0.shape, F32)],
        scratch_shapes=[pltpu.VMEM((kw, vw), F32)],
        compiler_params=_cparams(("arbitrary", "arbitrary")),
        name="gla",
    )(jnp.asarray(sel3, BF16), jnp.asarray(lev), q, k, v, la, s0)


def _head_rmsnorm(x, g, heads):
    outs = []
    for h in range(heads):
        blk = x[:, h * LANES:(h + 1) * LANES]
        ms = jnp.mean(blk * blk, axis=-1, keepdims=True)
        outs.append((blk * lax.rsqrt(ms + EPS)) * g)
    return jnp.concatenate(outs, axis=1)


def _merge_kernel(lam_init, final, da_heads, gla_heads, x_ref, oa_ref, ob_ref, dg_ref, gg_ref, mg_ref,
                  ga_ref, gb_ref, wa_ref, wb_ref, wo_ref, gf_ref, y_ref):
    d = x_ref.shape[-1]
    dg = dg_ref[...]
    gg = gg_ref[...]
    oa = (_head_rmsnorm(oa_ref[...], ga_ref[...], da_heads) * (1.0 - lam_init)) * (dg * _sigmoid(dg))
    ob = _head_rmsnorm(ob_ref[...], gb_ref[...], gla_heads) * (gg * _sigmoid(gg))
    ya = jnp.dot(oa.astype(BF16), wa_ref[...], preferred_element_type=F32)
    yb = jnp.dot(ob.astype(BF16), wb_ref[...], preferred_element_type=F32)
    mixed = _sigmoid(mg_ref[:, :d]) * ya + _sigmoid(mg_ref[:, d:]) * yb
    y = x_ref[...] + jnp.dot(mixed.astype(BF16), wo_ref[...], preferred_element_type=F32)
    if final:
        ms = jnp.mean(y * y, axis=-1, keepdims=True)
        y = (y * lax.rsqrt(ms + EPS)) * gf_ref[...]
    y_ref[...] = y


def _merge(x2, oa, ob, dg, gg, mg, da_g, gla_g, wa, wb, wo, gf, lam_init, final, da_heads, gla_heads, tm):
    tok, d = x2.shape
    row = lambda w: pl.BlockSpec((tm, w), lambda i: (i, 0))
    full = lambda a: pl.BlockSpec(a.shape, lambda i: (0, 0))
    da_g = da_g.reshape(1, -1)
    gla_g = gla_g.reshape(1, -1)
    gf = gf.reshape(1, d)
    return pl.pallas_call(
        functools.partial(_merge_kernel, lam_init, final, da_heads, gla_heads),
        grid=(tok // tm,),
        in_specs=[row(d), row(oa.shape[1]), row(ob.shape[1]), row(dg.shape[1]), row(gg.shape[1]), row(mg.shape[1]),
                  full(da_g), full(gla_g), full(wa), full(wb), full(wo), full(gf)],
        out_specs=row(d),
        out_shape=jax.ShapeDtypeStruct((tok, d), F32),
        compiler_params=_cparams(("arbitrary",)),
        name="merge",
    )(x2, oa, ob, dg, gg, mg, da_g, gla_g, wa, wb, wo, gf)


def _layout(da_heads, da_dk, gla_heads, gla_dk, gla_dv, rank, d_model):
    da_qk = da_heads * 2 * da_dk
    da_w = da_heads * 2 * da_dk
    kwid = gla_heads * gla_dk
    vwid = gla_heads * gla_dv
    names = ["dq", "dk", "dv", "dg", "gq", "gk", "gv", "gg", "mg", "ga"]
    sizes = [da_qk, da_qk, da_w, da_w, kwid, kwid, vwid, vwid, 2 * d_model, LANES]
    segs, off = {}, 0
    for nm, sz in zip(names, sizes):
        segs[nm] = (off, off + sz)
        off += sz
    segs["da_scale"] = da_dk ** -0.5 * math.log2(math.e)
    segs["gla_scale"] = gla_dk ** -0.5
    return segs, off


def kernel(x_prompt, x_sample, cache_k, cache_v, state_gla, norm_in_g, w_in, w_alpha_up, b_alpha,
           lambda_q1, lambda_k1, lambda_q2, lambda_k2, da_norm_g, gla_norm_g,
           w_branch_a, w_branch_b, w_out, norm_final_g):
    depth = w_in.shape[0]
    bsz, seq, d = x_prompt.shape
    dbsz, dseq, _ = x_sample.shape
    past = cache_k.shape[2]
    da_heads, da_dk = cache_k.shape[3], cache_k.shape[5]
    gla_heads, gla_dk, gla_dv = state_gla.shape[2:]
    rank = w_alpha_up.shape[1]
    kw = gla_heads * gla_dk
    assert 2 * da_dk == LANES and cache_v.shape[4] == LANES and gla_dv == LANES
    segs, ncat = _layout(da_heads, da_dk, gla_heads, gla_dk, gla_dv, rank, d)
    ga_lo = segs["gg"][1]

    xp = x_prompt.reshape(bsz * seq, d)
    xs = x_sample.reshape(dbsz * dseq, d)
    tm_p = min(512, bsz * seq)
    tm_s = min(512, dbsz * dseq)
    outs = [[] for _ in range(6)]
    for l in range(depth):
        lam_init = 0.8 - 0.6 * math.exp(-0.3 * l)
        final = l == depth - 1
        w = w_in[l]
        wcat = jnp.concatenate(
            [w[:, :ga_lo], w[:, ga_lo + rank:], w[:, ga_lo:ga_lo + rank], jnp.zeros((d, LANES - rank), w.dtype)],
            axis=1).astype(BF16)
        wkt = w[:, segs["dk"][0]:segs["dk"][1]].T.astype(BF16)
        wa_pad = jnp.concatenate([w_alpha_up[l], jnp.zeros((LANES - rank, kw), F32)], axis=0).astype(BF16)
        lamv = jnp.stack([lambda_q1[l], lambda_k1[l], lambda_q2[l], lambda_k2[l]])
        wa, wb, wo = w_branch_a[l].astype(BF16), w_branch_b[l].astype(BF16), w_out[l].astype(BF16)

        p = _proj(xp, norm_in_g[l], wcat, wkt, wa_pad, b_alpha[l], segs, tm_p, da_heads, seq=seq)
        sh = lambda a: a.reshape(bsz, seq, a.shape[-1])
        oa = _attn_prompt(sh(p["q"]), p["kb"], sh(p["vb"]), lamv, lam_init, da_heads)
        ob, s_p = _gla(sh(p["gq"]), sh(p["gk"]), sh(p["gv"]), sh(p["la"]),
                       jnp.zeros((bsz, gla_heads, gla_dk, gla_dv), F32), CHUNK)
        xp = _merge(xp, oa.reshape(bsz * seq, -1), ob.reshape(bsz * seq, -1), p["dg"], p["gg"], p["mg"],
                    da_norm_g[l], gla_norm_g[l], wa, wb, wo, norm_final_g, lam_init, final,
                    da_heads, gla_heads, min(256, tm_p))
        outs[0].append(jnp.transpose(p["kf"].reshape(bsz, da_heads, 2, da_dk, seq), (0, 4, 1, 2, 3)))
        outs[1].append(p["vf"].reshape(bsz, seq, da_heads, 2 * da_dk))
        outs[2].append(s_p)

        p = _proj(xs, norm_in_g[l], wcat, wkt, wa_pad, b_alpha[l], segs, tm_s, da_heads)
        sh = lambda a: a.reshape(dbsz, dseq, a.shape[-1])
        ckt = jnp.transpose(cache_k[l], (0, 2, 3, 4, 1)).reshape(dbsz, da_heads * 2 * da_dk, past)
        oa = _attn_sample(sh(p["q"]), ckt, cache_v[l].reshape(dbsz, past * da_heads, 2 * da_dk),
                          sh(p["kb"]), sh(p["vb"]), lamv, lam_init, da_heads)
        ob, s_s = _gla(sh(p["gq"]), sh(p["gk"]), sh(p["gv"]), sh(p["la"]), state_gla[l], dseq)
        xs = _merge(xs, oa.reshape(dbsz * dseq, -1), ob.reshape(dbsz * dseq, -1), p["dg"], p["gg"], p["mg"],
                    da_norm_g[l], gla_norm_g[l], wa, wb, wo, norm_final_g, lam_init, final,
                    da_heads, gla_heads, min(256, tm_s))
        outs[3].append(p["kf"].reshape(dbsz, dseq, da_heads, 2, da_dk))
        outs[4].append(p["vf"].reshape(dbsz, dseq, da_heads, 2 * da_dk))
        outs[5].append(s_s)

    return (xp.reshape(bsz, seq, d), xs.reshape(dbsz, dseq, d),
            jnp.stack(outs[0]), jnp.stack(outs[1]), jnp.stack(outs[2]),
            jnp.stack(outs[3]), jnp.stack(outs[4]), jnp.stack(outs[5]))
```

```python
import functools
import math

import numpy as np
import jax
import jax.numpy as jnp
from jax import lax
from jax.experimental import pallas as pl
from jax.experimental.pallas import tpu as pltpu

F32 = jnp.float32
BF16 = jnp.bfloat16

EPS = 1e-6
CHUNK = 64
GLA_TAU = 16.0
LANES = 128
VMEM_LIMIT_BYTES = 56 * 1024 * 1024

_NT = (((1,), (1,)), ((), ()))
_TN = (((0,), (0,)), ((), ()))


def _sigmoid(x):
    return 1.0 / (1.0 + jnp.exp(-x))


def _log_sigmoid(x):
    return jnp.minimum(x, 0.0) - jnp.log1p(jnp.exp(-jnp.abs(x)))


def _cparams(sem):
    return pltpu.CompilerParams(dimension_semantics=sem, vmem_limit_bytes=VMEM_LIMIT_BYTES)


def _proj_kernel(segs, heads, k_major, x_ref, g_ref, w_ref, wm_ref, wg_ref, wk_ref, wa_ref, ba_ref,
                 q_ref, kf_ref, kb_ref, vf_ref, vb_ref, dg_ref, gq_ref, gk_ref, gv_ref, gg_ref,
                 mg_ref, la_ref):
    x = x_ref[...]
    tm = x.shape[0]
    ms = jnp.mean(x * x, axis=-1, keepdims=True)
    xn = ((x * lax.rsqrt(ms + EPS)) * g_ref[...]).astype(BF16)

    def seg(name):
        lo, hi = segs[name]
        return jnp.dot(xn, w_ref[:, lo:hi], preferred_element_type=F32)

    q_ref[...] = (seg("dq") * segs["da_scale"]).astype(BF16)
    if k_major:
        kt = lax.dot_general(wk_ref[...], xn, _NT, preferred_element_type=F32)
        kf_ref[0] = kt
        kb_ref[0] = kt.astype(BF16)
    else:
        k = seg("dk")
        kf_ref[...] = k
        kb_ref[...] = k.astype(BF16)
    v = seg("dv")
    vb_ref[...] = v.astype(BF16)
    if k_major:
        for h in range(heads):
            vf_ref[pl.ds(h, tm, stride=heads), :] = v[:, h * LANES:(h + 1) * LANES]
    else:
        vf_ref[...] = v
    dg_ref[...] = seg("dg").astype(BF16)
    gq_ref[...] = seg("gq") * segs["gla_scale"]
    gk_ref[...] = seg("gk")
    gv_ref[...] = seg("gv").astype(BF16)
    gg_ref[...] = seg("gg").astype(BF16)
    step = 512
    for c in range(0, wm_ref.shape[1], step):
        mg_ref[:, c:c + step] = jnp.dot(xn, wm_ref[:, c:c + step], preferred_element_type=F32).astype(BF16)
    ga = jnp.dot(xn, wg_ref[...], preferred_element_type=F32).astype(BF16)
    z = jnp.dot(ga, wa_ref[...], preferred_element_type=F32) + ba_ref[...]
    la_ref[...] = _log_sigmoid(z) / GLA_TAU


def _proj(x2, g_in, weights, b_alpha, segs, tm, heads, seq=None):
    tok, d = x2.shape
    w_main, w_merge, w_gate, wkt, wa_pad = weights
    widths = {k: v[1] - v[0] for k, v in segs.items() if isinstance(v, tuple)}
    widths["mg"] = w_merge.shape[1]
    kw = wa_pad.shape[1]
    k_major = seq is not None

    def row_spec(w):
        return pl.BlockSpec((tm, w), lambda i: (i, 0))

    def full_spec(shape):
        return pl.BlockSpec(shape, lambda i: (0, 0))

    def const_spec(shape):
        return pl.BlockSpec(shape, lambda i: (0, 0), pipeline_mode=pl.Buffered(1))

    out_defs = [
        ("q", widths["dq"], BF16), ("kf", widths["dk"], F32), ("kb", widths["dk"], BF16),
        ("vf", widths["dv"], F32), ("vb", widths["dv"], BF16), ("dg", widths["dg"], BF16),
        ("gq", widths["gq"], F32), ("gk", widths["gk"], F32), ("gv", widths["gv"], BF16),
        ("gg", widths["gg"], BF16), ("mg", widths["mg"], BF16), ("la", kw, F32),
    ]
    out_specs = {nm: row_spec(w) for nm, w, _ in out_defs}
    out_shape = {nm: jax.ShapeDtypeStruct((tok, w), dt) for nm, w, dt in out_defs}
    if k_major:
        nt = seq // tm
        for nm, dt in (("kf", F32), ("kb", BF16)):
            out_specs[nm] = pl.BlockSpec((1, widths["dk"], tm), lambda i: (i // nt, 0, i % nt))
            out_shape[nm] = jax.ShapeDtypeStruct((tok // seq, widths["dk"], seq), dt)
        out_specs["vf"] = pl.BlockSpec((tm * heads, LANES), lambda i: (i, 0))
        out_shape["vf"] = jax.ShapeDtypeStruct((tok * heads, LANES), F32)
    names = [nm for nm, _, _ in out_defs]
    outs = pl.pallas_call(
        functools.partial(_proj_kernel, segs, heads, k_major),
        grid=(tok // tm,),
        in_specs=[row_spec(d), full_spec((1, d)), const_spec(w_main.shape), const_spec(w_merge.shape),
                  const_spec(w_gate.shape), const_spec(wkt.shape), full_spec(wa_pad.shape), full_spec((1, kw))],
        out_specs=[out_specs[nm] for nm in names],
        out_shape=[out_shape[nm] for nm in names],
        compiler_params=_cparams(("arbitrary",)),
        name="proj",
    )(x2, g_in.reshape(1, d), w_main, w_merge, w_gate, wkt, wa_pad, b_alpha.reshape(1, kw))
    return dict(zip(names, outs))


def _diff_lambda(lamv, lam_init):
    a = jnp.sum(lamv[0:1, :] * lamv[1:2, :], axis=1, keepdims=True)
    b = jnp.sum(lamv[2:3, :] * lamv[3:4, :], axis=1, keepdims=True)
    return jnp.exp(a) - jnp.exp(b) + lam_init


def _stack_maps(q):
    lane = lax.broadcasted_iota(jnp.int32, q.shape, 1)
    half = q.shape[1] // 2
    zero = jnp.zeros_like(q)
    return jnp.concatenate([jnp.where(lane < half, q, zero), jnp.where(lane >= half, q, zero)], axis=0)


def _attn_prompt_kernel(lam_init, tq, rb, lamv_ref, q_ref, kt_ref, v_ref, o_ref,
                        qx_ref, acc_ref, s0_ref, s1_ref, m0_ref, m1_ref, a0_ref, a1_ref):
    i = pl.program_id(2)
    qx_ref[...] = _stack_maps(q_ref[0])
    acc_ref[...] = jnp.zeros(acc_ref.shape, F32)
    m1_ref[...] = jnp.full(m1_ref.shape, -jnp.inf, F32)
    ones = jnp.ones((tq, LANES), BF16)
    sbuf, mbuf, abuf = (s0_ref, s1_ref), (m0_ref, m1_ref), (a0_ref, a1_ref)
    nblk = 2 * tq // rb

    def stage_a(j, slot, masked):
        start = pl.multiple_of(j * tq, tq)
        kt = kt_ref[0, :, pl.ds(start, tq)]
        sbuf[slot][...] = jnp.dot(qx_ref[...], kt, preferred_element_type=F32)
        for r in range(nblk):
            rows = slice(r * rb, (r + 1) * rb)
            s = sbuf[slot][rows, :]
            if masked:
                qc = ((r * rb) % tq + lax.broadcasted_iota(jnp.int32, s.shape, 0)) // CHUNK
                kc = lax.broadcasted_iota(jnp.int32, s.shape, 1) // CHUNK
                s = jnp.where(kc <= qc, s, -jnp.inf)
                sbuf[slot][rows, :] = s
            m_prev = mbuf[1 - slot][rows, :]
            m_new = jnp.maximum(m_prev, jnp.max(s, axis=1, keepdims=True))
            abuf[slot][rows, :] = jnp.exp2(m_prev - m_new)
            mbuf[slot][rows, :] = m_new

    def stage_b(j, slot):
        start = pl.multiple_of(j * tq, tq)
        vx = jnp.concatenate([v_ref[0, pl.ds(start, tq), :], ones], axis=1)
        for r in range(nblk):
            rows = slice(r * rb, (r + 1) * rb)
            p = jnp.exp2(sbuf[slot][rows, :] - jnp.tile(mbuf[slot][rows, :], (1, tq // LANES)))
            pv = jnp.dot(p.astype(BF16), vx, preferred_element_type=F32)
            acc_ref[rows, :] = acc_ref[rows, :] * jnp.tile(abuf[slot][rows, :], (1, 2)) + pv

    stage_a(i, 0, True)

    def pair(u, c):
        stage_a(2 * u, 1, False)
        stage_b(jnp.where(u == 0, i, 2 * u - 1), 0)
        stage_a(2 * u + 1, 0, False)
        stage_b(2 * u, 1)
        return c

    lax.fori_loop(0, i // 2, pair, 0)
    last = jnp.where(i < 2, i, 2 * (i // 2) - 1)

    @pl.when(i % 2 == 0)
    def _():
        stage_b(last, 0)

    @pl.when(i % 2 == 1)
    def _():
        stage_a(i - 1, 1, False)
        stage_b(last, 0)
        stage_b(i - 1, 1)

    acc = acc_ref[...]
    o1 = acc[:tq, :LANES] / acc[:tq, LANES:]
    o2 = acc[tq:, :LANES] / acc[tq:, LANES:]
    o_ref[0] = o1 - _diff_lambda(lamv_ref[...], lam_init) * o2


def _attn_prompt(q, kt, v, lamv, lam_init, heads):
    bsz, t, w = q.shape
    tq = min(512, t)
    rb = min(512, tq)
    rowbuf = pltpu.VMEM((2 * tq, LANES), F32)
    return pl.pallas_call(
        functools.partial(_attn_prompt_kernel, lam_init, tq, rb),
        grid=(bsz, heads, t // tq),
        in_specs=[pl.BlockSpec(lamv.shape, lambda b, h, i: (0, 0)),
                  pl.BlockSpec((1, tq, LANES), lambda b, h, i: (b, i, h)),
                  pl.BlockSpec((1, LANES, t), lambda b, h, i: (b, h, 0)),
                  pl.BlockSpec((1, t, LANES), lambda b, h, i: (b, 0, h))],
        out_specs=pl.BlockSpec((1, tq, LANES), lambda b, h, i: (b, i, h)),
        out_shape=jax.ShapeDtypeStruct((bsz, t, w), F32),
        scratch_shapes=[pltpu.VMEM((2 * tq, LANES), BF16), pltpu.VMEM((2 * tq, 2 * LANES), F32),
                        pltpu.VMEM((2 * tq, tq), F32), pltpu.VMEM((2 * tq, tq), F32),
                        rowbuf, rowbuf, rowbuf, rowbuf],
        compiler_params=_cparams(("arbitrary", "arbitrary", "arbitrary")),
        name="attn_prompt",
    )(lamv, q, kt, v)


def _attn_sample_kernel(lam_init, heads, lamv_ref, q_ref, ckt_ref, cv_ref, kn_ref, vn_ref, o_ref):
    tq = q_ref.shape[1]
    past = ckt_ref.shape[2]
    lam = _diff_lambda(lamv_ref[...], lam_init)
    for h in range(heads):
        sl = slice(h * LANES, (h + 1) * LANES)
        qx = _stack_maps(q_ref[0, :, sl])
        kc = ckt_ref[0, sl, :].astype(BF16)
        vc = cv_ref[0, pl.ds(h, past, stride=heads), :].astype(BF16)
        kn = kn_ref[0, :, sl]
        vn = vn_ref[0, :, sl]
        sc = jnp.dot(qx, kc, preferred_element_type=F32)
        sn = lax.dot_general(qx, kn, _NT, preferred_element_type=F32)
        m = jnp.maximum(jnp.max(sc, axis=1, keepdims=True), jnp.max(sn, axis=1, keepdims=True))
        ec = jnp.exp2(sc - m)
        en = jnp.exp2(sn - m)
        l = jnp.sum(ec, axis=1, keepdims=True) + jnp.sum(en, axis=1, keepdims=True)
        o = (jnp.dot(ec.astype(BF16), vc, preferred_element_type=F32)
             + jnp.dot(en.astype(BF16), vn, preferred_element_type=F32)) / l
        o_ref[0, :, sl] = o[:tq] - lam * o[tq:]


def _attn_sample(q, ckt, cv, kn, vn, lamv, lam_init, heads):
    bsz, tq, w = q.shape
    new_spec = pl.BlockSpec((1, tq, w), lambda b: (b, 0, 0))
    return pl.pallas_call(
        functools.partial(_attn_sample_kernel, lam_init, heads),
        grid=(bsz,),
        in_specs=[pl.BlockSpec(lamv.shape, lambda b: (0, 0)), new_spec,
                  pl.BlockSpec((1,) + ckt.shape[1:], lambda b: (b, 0, 0)),
                  pl.BlockSpec((1,) + cv.shape[1:], lambda b: (b, 0, 0)), new_spec, new_spec],
        out_specs=new_spec,
        out_shape=jax.ShapeDtypeStruct((bsz, tq, w), F32),
        compiler_params=_cparams(("arbitrary",)),
        name="attn_sample",
    )(lamv, q, ckt, cv, kn, vn)


def _gla_tables(c, heads):
    nlev = int(math.log2(c))
    t = np.arange(c)[:, None]
    u = np.arange(c)[None, :]
    mats = [(u <= t)]
    lev = np.where(t == u, 0, -1)
    for l in range(1, nlev + 1):
        n = c >> l
        r = (t // (2 * n)) * (2 * n) + n - 1
        upper = (t % (2 * n)) >= n
        mats.append(upper & (u > r) & (u <= t))
        mats.append((~upper) & (u > t) & (u <= r))
        same_block = (t // (2 * n)) == (u // (2 * n))
        lev = np.where(same_block & upper & ((u % (2 * n)) < n), l, lev)
    sel = np.concatenate(mats, axis=0).astype(np.float32)
    sel3 = np.concatenate([sel, sel, sel], axis=1)
    return sel3, np.tile(lev, (1, heads)).astype(np.int32), nlev


def _gla_kernel(c, nchunk, heads, nlev, sel_ref, lev_ref, q_ref, k_ref, v_ref, la_ref, s0_ref,
                o_ref, sout_ref, s_ref):
    ti = pl.program_id(1)
    kw = q_ref.shape[-1]
    vw = v_ref.shape[-1]
    dk = kw // heads
    dv = vw // heads

    row_h = lax.broadcasted_iota(jnp.int32, (kw, vw), 0) // dk
    col_h = lax.broadcasted_iota(jnp.int32, (kw, vw), 1) // dv
    diag = row_h == col_h

    @pl.when(ti == 0)
    def _():
        s_ref[...] = jnp.zeros(s_ref.shape, F32)
        for h in range(heads):
            s_ref[h * dk:(h + 1) * dk, h * dv:(h + 1) * dv] = s0_ref[0, h]

    sel = sel_ref[...]
    lev = lev_ref[...]
    klane_h = lax.broadcasted_iota(jnp.int32, (c, kw), 1) // dk
    vlane_h = lax.broadcasted_iota(jnp.int32, (c, vw), 1) // dv
    ones = jnp.ones((3 * c, LANES), BF16)

    for ci in range(nchunk):
        rows = slice(ci * c, (ci + 1) * c)
        q = q_ref[0, rows, :]
        k = k_ref[0, rows, :]
        v = v_ref[0, rows, :]
        la = la_ref[0, rows, :]
        p0 = la.astype(BF16)
        r0 = la - p0.astype(F32)
        p1 = r0.astype(BF16)
        p2 = (r0 - p1.astype(F32)).astype(BF16)
        pieces = jnp.concatenate([p0, p1, p2], axis=0)
        sums = jnp.dot(sel, pieces, preferred_element_type=F32)
        b = sums[0:c]
        b_last = b[c - 1:c, :]

        att = None
        for l in range(nlev + 1):
            if l == 0:
                qt, kt = q.astype(BF16), k.astype(BF16)
            else:
                qt = (q * jnp.exp(sums[(2 * l - 1) * c:2 * l * c])).astype(BF16)
                kt = (k * jnp.exp(sums[2 * l * c:(2 * l + 1) * c])).astype(BF16)
            kstack = jnp.concatenate(
                [jnp.where(klane_h == h, kt, jnp.zeros_like(kt)) for h in range(heads)], axis=0)
            a = lax.dot_general(qt, kstack, _NT, preferred_element_type=F32)
            a = jnp.where(lev == l, a, 0.0)
            att = a if att is None else att + a
        vbd = jnp.concatenate(
            [jnp.where(vlane_h == h, v, jnp.zeros_like(v)) for h in range(heads)], axis=0)
        o = jnp.dot(att.astype(BF16), vbd, preferred_element_type=F32)
        s_old = s_ref[...]
        o = o + jnp.dot((q * jnp.exp(b)).astype(BF16), s_old.astype(BF16), preferred_element_type=F32)
        o_ref[0, rows, :] = o

        kd = (k * jnp.exp(b_last - b)).astype(BF16)
        upd = lax.dot_general(kd, v, _TN, preferred_element_type=F32)
        b_col = lax.dot_general(pieces, ones, _TN, preferred_element_type=F32)
        decay = jnp.tile(jnp.exp(b_col), (1, vw // LANES))
        s_ref[...] = decay * s_old + jnp.where(diag, upd, 0.0)

    @pl.when(ti == pl.num_programs(1) - 1)
    def _():
        for h in range(heads):
            sout_ref[0, h] = s_ref[h * dk:(h + 1) * dk, h * dv:(h + 1) * dv]


def _gla(q, k, v, la, s0, chunk):
    bsz, t, kw = q.shape
    vw = v.shape[-1]
    heads, dk, dv = s0.shape[1:]
    sel3, lev, nlev = _gla_tables(chunk, heads)
    nchunk = min(4, t // chunk)
    tg = nchunk * chunk
    tok = lambda w: pl.BlockSpec((1, tg, w), lambda b, i: (b, i, 0))
    st = pl.BlockSpec((1, heads, dk, dv), lambda b, i: (b, 0, 0, 0))
    return pl.pallas_call(
        functools.partial(_gla_kernel, chunk, nchunk, heads, nlev),
        grid=(bsz, t // tg),
        in_specs=[pl.BlockSpec(sel3.shape, lambda b, i: (0, 0)), pl.BlockSpec(lev.shape, lambda b, i: (0, 0)),
                  tok(kw), tok(kw), tok(vw), tok(kw), st],
        out_specs=[tok(vw), st],
        out_shape=[jax.ShapeDtypeStruct((bsz, t, vw), F32), jax.ShapeDtypeStruct(s0.shape, F32)],
        scratch_shapes=[pltpu.VMEM((kw, vw), F32)],
        compiler_params=_cparams(("arbitrary", "arbitrary")),
        name="gla",
    )(jnp.asarray(sel3, BF16), jnp.asarray(lev), q, k, v, la, s0)


def _head_rmsnorm(x, g, heads):
    outs = []
    for h in range(heads):
        blk = x[:, h * LANES:(h + 1) * LANES]
        ms = jnp.mean(blk * blk, axis=-1, keepdims=True)
        outs.append((blk * lax.rsqrt(ms + EPS)) * g)
    return jnp.concatenate(outs, axis=1)


def _merge_kernel(lam_init, final, da_heads, gla_heads, x_ref, oa_ref, ob_ref, dg_ref, gg_ref, mg_ref,
                  ga_ref, gb_ref, wa_ref, wb_ref, wo_ref, gf_ref, y_ref):
    d = x_ref.shape[-1]
    dg = dg_ref[...].astype(F32)
    gg = gg_ref[...].astype(F32)
    oa = (_head_rmsnorm(oa_ref[...], ga_ref[...], da_heads) * (1.0 - lam_init)) * (dg * _sigmoid(dg))
    ob = _head_rmsnorm(ob_ref[...], gb_ref[...], gla_heads) * (gg * _sigmoid(gg))
    ya = jnp.dot(oa.astype(BF16), wa_ref[...], preferred_element_type=F32)
    yb = jnp.dot(ob.astype(BF16), wb_ref[...], preferred_element_type=F32)
    mixed = _sigmoid(mg_ref[:, :d].astype(F32)) * ya + _sigmoid(mg_ref[:, d:].astype(F32)) * yb
    y = x_ref[...] + jnp.dot(mixed.astype(BF16), wo_ref[...], preferred_element_type=F32)
    if final:
        ms = jnp.mean(y * y, axis=-1, keepdims=True)
        y = (y * lax.rsqrt(ms + EPS)) * gf_ref[...]
    y_ref[...] = y


def _merge(x2, oa, ob, dg, gg, mg, da_g, gla_g, wa, wb, wo, gf, lam_init, final, da_heads, gla_heads, tm):
    tok, d = x2.shape
    row = lambda w: pl.BlockSpec((tm, w), lambda i: (i, 0))
    full = lambda a: pl.BlockSpec(a.shape, lambda i: (0, 0))
    da_g = da_g.reshape(1, -1)
    gla_g = gla_g.reshape(1, -1)
    gf = gf.reshape(1, d)
    return pl.pallas_call(
        functools.partial(_merge_kernel, lam_init, final, da_heads, gla_heads),
        grid=(tok // tm,),
        in_specs=[row(d), row(oa.shape[1]), row(ob.shape[1]), row(dg.shape[1]), row(gg.shape[1]), row(mg.shape[1]),
                  full(da_g), full(gla_g), full(wa), full(wb), full(wo), full(gf)],
        out_specs=row(d),
        out_shape=jax.ShapeDtypeStruct((tok, d), F32),
        compiler_params=_cparams(("arbitrary",)),
        name="merge",
    )(x2, oa, ob, dg, gg, mg, da_g, gla_g, wa, wb, wo, gf)


def _layout(da_heads, da_dk, gla_heads, gla_dk, gla_dv):
    da_qk = da_heads * 2 * da_dk
    da_w = da_heads * 2 * da_dk
    kwid = gla_heads * gla_dk
    vwid = gla_heads * gla_dv
    names = ["dq", "dk", "dv", "dg", "gq", "gk", "gv", "gg"]
    sizes = [da_qk, da_qk, da_w, da_w, kwid, kwid, vwid, vwid]
    segs, off = {}, 0
    for nm, sz in zip(names, sizes):
        segs[nm] = (off, off + sz)
        off += sz
    segs["da_scale"] = da_dk ** -0.5 * math.log2(math.e)
    segs["gla_scale"] = gla_dk ** -0.5
    return segs, off


def kernel(x_prompt, x_sample, cache_k, cache_v, state_gla, norm_in_g, w_in, w_alpha_up, b_alpha,
           lambda_q1, lambda_k1, lambda_q2, lambda_k2, da_norm_g, gla_norm_g,
           w_branch_a, w_branch_b, w_out, norm_final_g):
    depth = w_in.shape[0]
    bsz, seq, d = x_prompt.shape
    dbsz, dseq, _ = x_sample.shape
    past = cache_k.shape[2]
    da_heads, da_dk = cache_k.shape[3], cache_k.shape[5]
    gla_heads, gla_dk, gla_dv = state_gla.shape[2:]
    rank = w_alpha_up.shape[1]
    kw = gla_heads * gla_dk
    assert 2 * da_dk == LANES and cache_v.shape[4] == LANES and gla_dv == LANES
    segs, ga_lo = _layout(da_heads, da_dk, gla_heads, gla_dk, gla_dv)

    xp = x_prompt.reshape(bsz * seq, d)
    xs = x_sample.reshape(dbsz * dseq, d)
    tm_p = min(512, bsz * seq)
    tm_s = min(512, dbsz * dseq)
    outs = [[] for _ in range(6)]
    for l in range(depth):
        lam_init = 0.8 - 0.6 * math.exp(-0.3 * l)
        final = l == depth - 1
        w = w_in[l]
        w_gate = jnp.pad(w[:, ga_lo:ga_lo + rank], ((0, 0), (0, LANES - rank))).astype(BF16)
        wkt = w[:, segs["dk"][0]:segs["dk"][1]].T.astype(BF16)
        wa_pad = jnp.pad(w_alpha_up[l], ((0, LANES - rank), (0, 0))).astype(BF16)
        weights = (w[:, :ga_lo].astype(BF16), w[:, ga_lo + rank:].astype(BF16), w_gate, wkt, wa_pad)
        lamv = jnp.stack([lambda_q1[l], lambda_k1[l], lambda_q2[l], lambda_k2[l]])
        wa, wb, wo = w_branch_a[l].astype(BF16), w_branch_b[l].astype(BF16), w_out[l].astype(BF16)

        p = _proj(xp, norm_in_g[l], weights, b_alpha[l], segs, tm_p, da_heads, seq=seq)
        sh = lambda a: a.reshape(bsz, seq, a.shape[-1])
        oa = _attn_prompt(sh(p["q"]), p["kb"], sh(p["vb"]), lamv, lam_init, da_heads)
        ob, s_p = _gla(sh(p["gq"]), sh(p["gk"]), sh(p["gv"]), sh(p["la"]),
                       jnp.zeros((bsz, gla_heads, gla_dk, gla_dv), F32), CHUNK)
        xp = _merge(xp, oa.reshape(bsz * seq, -1), ob.reshape(bsz * seq, -1), p["dg"], p["gg"], p["mg"],
                    da_norm_g[l], gla_norm_g[l], wa, wb, wo, norm_final_g, lam_init, final,
                    da_heads, gla_heads, min(256, tm_p))
        outs[0].append(jnp.transpose(p["kf"].reshape(bsz, da_heads, 2, da_dk, seq), (0, 4, 1, 2, 3)))
        outs[1].append(p["vf"].reshape(bsz, seq, da_heads, 2 * da_dk))
        outs[2].append(s_p)

        p = _proj(xs, norm_in_g[l], weights, b_alpha[l], segs, tm_s, da_heads)
        sh = lambda a: a.reshape(dbsz, dseq, a.shape[-1])
        ckt = jnp.transpose(cache_k[l], (0, 2, 3, 4, 1)).reshape(dbsz, da_heads * 2 * da_dk, past)
        oa = _attn_sample(sh(p["q"]), ckt, cache_v[l].reshape(dbsz, past * da_heads, 2 * da_dk),
                          sh(p["kb"]), sh(p["vb"]), lamv, lam_init, da_heads)
        ob, s_s = _gla(sh(p["gq"]), sh(p["gk"]), sh(p["gv"]), sh(p["la"]), state_gla[l], dseq)
        xs = _merge(xs, oa.reshape(dbsz * dseq, -1), ob.reshape(dbsz * dseq, -1), p["dg"], p["gg"], p["mg"],
                    da_norm_g[l], gla_norm_g[l], wa, wb, wo, norm_final_g, lam_init, final,
                    da_heads, gla_heads, min(256, tm_s))
        outs[3].append(p["kf"].reshape(dbsz, dseq, da_heads, 2, da_dk))
        outs[4].append(p["vf"].reshape(dbsz, dseq, da_heads, 2 * da_dk))
        outs[5].append(s_s)

    return (xp.reshape(bsz, seq, d), xs.reshape(dbsz, dseq, d),
            jnp.stack(outs[0]), jnp.stack(outs[1]), jnp.stack(outs[2]),
            jnp.stack(outs[3]), jnp.stack(outs[4]), jnp.stack(outs[5]))
```

```python
import functools
import math

import numpy as np
import jax
import jax.numpy as jnp
from jax import lax
from jax.experimental import pallas as pl
from jax.experimental.pallas import tpu as pltpu

F32 = jnp.float32
BF16 = jnp.bfloat16

EPS = 1e-6
CHUNK = 64
GLA_TAU = 16.0
LANES = 128
VMEM_LIMIT_BYTES = 56 * 1024 * 1024

_NT = (((1,), (1,)), ((), ()))
_TN = (((0,), (0,)), ((), ()))


def _sigmoid(x):
    return 1.0 / (1.0 + jnp.exp(-x))


def _log_sigmoid(x):
    return jnp.minimum(x, 0.0) - jnp.log1p(jnp.exp(-jnp.abs(x)))


def _cparams(sem):
    return pltpu.CompilerParams(dimension_semantics=sem, vmem_limit_bytes=VMEM_LIMIT_BYTES)


def _proj_kernel(segs, heads, k_major, x_ref, g_ref, w_ref, wm_ref, wg_ref, wk_ref, wa_ref, ba_ref,
                 q_ref, kf_ref, kb_ref, vf_ref, vb_ref, dg_ref, gq_ref, gk_ref, gv_ref, gg_ref,
                 mg_ref, la_ref):
    x = x_ref[...]
    tm = x.shape[0]
    ms = jnp.mean(x * x, axis=-1, keepdims=True)
    xn = ((x * lax.rsqrt(ms + EPS)) * g_ref[...]).astype(BF16)

    def seg(name):
        lo, hi = segs[name]
        return jnp.dot(xn, w_ref[:, lo:hi], preferred_element_type=F32)

    q_ref[...] = (seg("dq") * segs["da_scale"]).astype(BF16)
    if k_major:
        kt = lax.dot_general(wk_ref[...], xn, _NT, preferred_element_type=F32)
        kf_ref[0] = kt
        kb_ref[0] = kt.astype(BF16)
    else:
        k = seg("dk")
        kf_ref[...] = k
        kb_ref[...] = k.astype(BF16)
    v = seg("dv")
    vb_ref[...] = v.astype(BF16)
    if k_major:
        for h in range(heads):
            vf_ref[pl.ds(h, tm, stride=heads), :] = v[:, h * LANES:(h + 1) * LANES]
    else:
        vf_ref[...] = v
    dg_ref[...] = seg("dg").astype(BF16)
    gq_ref[...] = seg("gq") * segs["gla_scale"]
    gk_ref[...] = seg("gk")
    gv_ref[...] = seg("gv").astype(BF16)
    gg_ref[...] = seg("gg").astype(BF16)
    step = 512
    for c in range(0, wm_ref.shape[1], step):
        mg_ref[:, c:c + step] = jnp.dot(xn, wm_ref[:, c:c + step], preferred_element_type=F32).astype(BF16)
    ga = jnp.dot(xn, wg_ref[...], preferred_element_type=F32).astype(BF16)
    z = jnp.dot(ga, wa_ref[...], preferred_element_type=F32) + ba_ref[...]
    la_ref[...] = _log_sigmoid(z) / GLA_TAU


def _proj(x2, g_in, weights, b_alpha, segs, tm, heads, seq=None):
    tok, d = x2.shape
    w_main, w_merge, w_gate, wkt, wa_pad = weights
    widths = {k: v[1] - v[0] for k, v in segs.items() if isinstance(v, tuple)}
    widths["mg"] = w_merge.shape[1]
    kw = wa_pad.shape[1]
    k_major = seq is not None

    def row_spec(w):
        return pl.BlockSpec((tm, w), lambda i: (i, 0))

    def full_spec(shape):
        return pl.BlockSpec(shape, lambda i: (0, 0))

    def const_spec(shape):
        return pl.BlockSpec(shape, lambda i: (0, 0), pipeline_mode=pl.Buffered(1))

    out_defs = [
        ("q", widths["dq"], BF16), ("kf", widths["dk"], F32), ("kb", widths["dk"], BF16),
        ("vf", widths["dv"], F32), ("vb", widths["dv"], BF16), ("dg", widths["dg"], BF16),
        ("gq", widths["gq"], F32), ("gk", widths["gk"], F32), ("gv", widths["gv"], BF16),
        ("gg", widths["gg"], BF16), ("mg", widths["mg"], BF16), ("la", kw, F32),
    ]
    out_specs = {nm: row_spec(w) for nm, w, _ in out_defs}
    out_shape = {nm: jax.ShapeDtypeStruct((tok, w), dt) for nm, w, dt in out_defs}
    if k_major:
        nt = seq // tm
        for nm, dt in (("kf", F32), ("kb", BF16)):
            out_specs[nm] = pl.BlockSpec((1, widths["dk"], tm), lambda i: (i // nt, 0, i % nt))
            out_shape[nm] = jax.ShapeDtypeStruct((tok // seq, widths["dk"], seq), dt)
        out_specs["vf"] = pl.BlockSpec((tm * heads, LANES), lambda i: (i, 0))
        out_shape["vf"] = jax.ShapeDtypeStruct((tok * heads, LANES), F32)
    names = [nm for nm, _, _ in out_defs]
    outs = pl.pallas_call(
        functools.partial(_proj_kernel, segs, heads, k_major),
        grid=(tok // tm,),
        in_specs=[row_spec(d), full_spec((1, d)), const_spec(w_main.shape), const_spec(w_merge.shape),
                  const_spec(w_gate.shape), const_spec(wkt.shape), full_spec(wa_pad.shape), full_spec((1, kw))],
        out_specs=[out_specs[nm] for nm in names],
        out_shape=[out_shape[nm] for nm in names],
        compiler_params=_cparams(("arbitrary",)),
        name="proj",
    )(x2, g_in.reshape(1, d), w_main, w_merge, w_gate, wkt, wa_pad, b_alpha.reshape(1, kw))
    return dict(zip(names, outs))


def _diff_lambda(lamv, lam_init):
    a = jnp.sum(lamv[0:1, :] * lamv[1:2, :], axis=1, keepdims=True)
    b = jnp.sum(lamv[2:3, :] * lamv[3:4, :], axis=1, keepdims=True)
    return jnp.exp(a) - jnp.exp(b) + lam_init


def _stack_maps(q):
    lane = lax.broadcasted_iota(jnp.int32, q.shape, 1)
    half = q.shape[1] // 2
    zero = jnp.zeros_like(q)
    return jnp.concatenate([jnp.where(lane < half, q, zero), jnp.where(lane >= half, q, zero)], axis=0)


def _attn_prompt_kernel(lam_init, tq, rb, lamv_ref, q_ref, kt_ref, v_ref, o_ref,
                        qx_ref, acc_ref, s0_ref, s1_ref, m0_ref, m1_ref, a0_ref, a1_ref):
    i = pl.program_id(2)
    qx_ref[...] = _stack_maps(q_ref[0])
    acc_ref[...] = jnp.zeros(acc_ref.shape, F32)
    m1_ref[...] = jnp.full(m1_ref.shape, -jnp.inf, F32)
    ones = jnp.ones((tq, LANES), BF16)
    sbuf, mbuf, abuf = (s0_ref, s1_ref), (m0_ref, m1_ref), (a0_ref, a1_ref)
    nblk = 2 * tq // rb

    def stage_a(j, slot, masked):
        start = pl.multiple_of(j * tq, tq)
        kt = kt_ref[0, :, pl.ds(start, tq)]
        sbuf[slot][...] = jnp.dot(qx_ref[...], kt, preferred_element_type=F32)
        for r in range(nblk):
            rows = slice(r * rb, (r + 1) * rb)
            s = sbuf[slot][rows, :]
            if masked:
                qc = ((r * rb) % tq + lax.broadcasted_iota(jnp.int32, s.shape, 0)) // CHUNK
                kc = lax.broadcasted_iota(jnp.int32, s.shape, 1) // CHUNK
                s = jnp.where(kc <= qc, s, -jnp.inf)
                sbuf[slot][rows, :] = s
            m_prev = mbuf[1 - slot][rows, :]
            m_new = jnp.maximum(m_prev, jnp.max(s, axis=1, keepdims=True))
            abuf[slot][rows, :] = jnp.exp2(m_prev - m_new)
            mbuf[slot][rows, :] = m_new

    def stage_b(j, slot):
        start = pl.multiple_of(j * tq, tq)
        vx = jnp.concatenate([v_ref[0, pl.ds(start, tq), :], ones], axis=1)
        for r in range(nblk):
            rows = slice(r * rb, (r + 1) * rb)
            p = jnp.exp2(sbuf[slot][rows, :] - jnp.tile(mbuf[slot][rows, :], (1, tq // LANES)))
            pv = jnp.dot(p.astype(BF16), vx, preferred_element_type=F32)
            acc_ref[rows, :] = acc_ref[rows, :] * jnp.tile(abuf[slot][rows, :], (1, 2)) + pv

    stage_a(i, 0, True)

    def pair(u, c):
        stage_a(2 * u, 1, False)
        stage_b(jnp.where(u == 0, i, 2 * u - 1), 0)
        stage_a(2 * u + 1, 0, False)
        stage_b(2 * u, 1)
        return c

    lax.fori_loop(0, i // 2, pair, 0)
    last = jnp.where(i < 2, i, 2 * (i // 2) - 1)

    @pl.when(i % 2 == 0)
    def _():
        stage_b(last, 0)

    @pl.when(i % 2 == 1)
    def _():
        stage_a(i - 1, 1, False)
        stage_b(last, 0)
        stage_b(i - 1, 1)

    acc = acc_ref[...]
    o1 = acc[:tq, :LANES] / acc[:tq, LANES:]
    o2 = acc[tq:, :LANES] / acc[tq:, LANES:]
    o_ref[0] = o1 - _diff_lambda(lamv_ref[...], lam_init) * o2


def _attn_prompt(q, kt, v, lamv, lam_init, heads):
    bsz, t, w = q.shape
    tq = min(512, t)
    rb = min(512, tq)
    rowbuf = pltpu.VMEM((2 * tq, LANES), F32)
    return pl.pallas_call(
        functools.partial(_attn_prompt_kernel, lam_init, tq, rb),
        grid=(bsz, heads, t // tq),
        in_specs=[pl.BlockSpec(lamv.shape, lambda b, h, i: (0, 0)),
                  pl.BlockSpec((1, tq, LANES), lambda b, h, i: (b, i, h)),
                  pl.BlockSpec((1, LANES, t), lambda b, h, i: (b, h, 0)),
                  pl.BlockSpec((1, t, LANES), lambda b, h, i: (b, 0, h))],
        out_specs=pl.BlockSpec((1, tq, LANES), lambda b, h, i: (b, i, h)),
        out_shape=jax.ShapeDtypeStruct((bsz, t, w), F32),
        scratch_shapes=[pltpu.VMEM((2 * tq, LANES), BF16), pltpu.VMEM((2 * tq, 2 * LANES), F32),
                        pltpu.VMEM((2 * tq, tq), F32), pltpu.VMEM((2 * tq, tq), F32),
                        rowbuf, rowbuf, rowbuf, rowbuf],
        compiler_params=_cparams(("arbitrary", "arbitrary", "arbitrary")),
        name="attn_prompt",
    )(lamv, q, kt, v)


def _attn_sample_kernel(lam_init, heads, lamv_ref, q_ref, ckt_ref, cv_ref, kn_ref, vn_ref, o_ref):
    tq = q_ref.shape[1]
    past = ckt_ref.shape[2]
    lam = _diff_lambda(lamv_ref[...], lam_init)
    for h in range(heads):
        sl = slice(h * LANES, (h + 1) * LANES)
        qx = _stack_maps(q_ref[0, :, sl])
        kc = ckt_ref[0, sl, :].astype(BF16)
        vc = cv_ref[0, pl.ds(h, past, stride=heads), :].astype(BF16)
        kn = kn_ref[0, :, sl]
        vn = vn_ref[0, :, sl]
        sc = jnp.dot(qx, kc, preferred_element_type=F32)
        sn = lax.dot_general(qx, kn, _NT, preferred_element_type=F32)
        m = jnp.maximum(jnp.max(sc, axis=1, keepdims=True), jnp.max(sn, axis=1, keepdims=True))
        ec = jnp.exp2(sc - m)
        en = jnp.exp2(sn - m)
        l = jnp.sum(ec, axis=1, keepdims=True) + jnp.sum(en, axis=1, keepdims=True)
        o = (jnp.dot(ec.astype(BF16), vc, preferred_element_type=F32)
             + jnp.dot(en.astype(BF16), vn, preferred_element_type=F32)) / l
        o_ref[0, :, sl] = o[:tq] - lam * o[tq:]


def _attn_sample(q, ckt, cv, kn, vn, lamv, lam_init, heads):
    bsz, tq, w = q.shape
    new_spec = pl.BlockSpec((1, tq, w), lambda b: (b, 0, 0))
    return pl.pallas_call(
        functools.partial(_attn_sample_kernel, lam_init, heads),
        grid=(bsz,),
        in_specs=[pl.BlockSpec(lamv.shape, lambda b: (0, 0)), new_spec,
                  pl.BlockSpec((1,) + ckt.shape[1:], lambda b: (b, 0, 0)),
                  pl.BlockSpec((1,) + cv.shape[1:], lambda b: (b, 0, 0)), new_spec, new_spec],
        out_specs=new_spec,
        out_shape=jax.ShapeDtypeStruct((bsz, tq, w), F32),
        compiler_params=_cparams(("arbitrary",)),
        name="attn_sample",
    )(lamv, q, ckt, cv, kn, vn)


def _gla_tables(c, heads, nchunk):
    nlev = int(math.log2(c))
    assert 1 << nlev == c and c % 8 == 0
    t = np.arange(c)[:, None]
    u = np.arange(c)[None, :]
    lev = np.where(t == u, 0, -1)
    for l in range(1, nlev + 1):
        n = c >> l
        same_block = (t // (2 * n)) == (u // (2 * n))
        lev = np.where(same_block & ((t % (2 * n)) >= n) & ((u % (2 * n)) < n), l, lev)
    lt = (u <= t).astype(np.float32)
    return np.concatenate([lt, lt, lt], axis=1), np.tile(lev, (nchunk, heads)).astype(np.int32), nlev


def _block_reference(b, n):
    rows, w = b.shape
    if n >= 8:
        parts = [jnp.broadcast_to(b[g + n - 1:g + n, :], (2 * n, w)) for g in range(0, rows, 2 * n)]
        return parts[0] if len(parts) == 1 else jnp.concatenate(parts, axis=0)
    b3 = b.reshape(rows // 8, 8, w)
    sub = lax.broadcasted_iota(jnp.int32, b3.shape, 1)
    pick = lambda j: jnp.broadcast_to(b3[:, j:j + 1, :], b3.shape)
    if n == 4:
        ref = pick(3)
    elif n == 2:
        ref = jnp.where(sub < 4, pick(1), pick(5))
    else:
        ref = jnp.where(sub < 2, pick(0), jnp.where(sub < 4, pick(2), jnp.where(sub < 6, pick(4), pick(6))))
    return ref.reshape(rows, w)


def _gla_kernel(c, nchunk, heads, nlev, lt_ref, lev_ref, q_ref, k_ref, v_ref, la_ref, s0_ref,
                o_ref, sout_ref, s_ref):
    ti = pl.program_id(1)
    kw = q_ref.shape[-1]
    vw = v_ref.shape[-1]
    dk = kw // heads
    dv = vw // heads
    pad = LANES - c
    assert dv == LANES and pad >= 0

    @pl.when(ti == 0)
    def _():
        s_ref[...] = s0_ref[0]

    q = q_ref[0]
    k = k_ref[0]
    v = v_ref[0]
    chunk = lambda a, ci: a[ci * c:(ci + 1) * c]

    la2 = la_ref[0] * math.log2(math.e)
    p0 = la2.astype(BF16)
    r0 = la2 - p0.astype(F32)
    p1 = r0.astype(BF16)
    p2 = (r0 - p1.astype(F32)).astype(BF16)
    lt = lt_ref[...]
    b = jnp.concatenate(
        [jnp.dot(lt, jnp.concatenate([chunk(p0, ci), chunk(p1, ci), chunk(p2, ci)], axis=0),
                 preferred_element_type=F32) for ci in range(nchunk)], axis=0)

    t = lax.broadcasted_iota(jnp.int32, b.shape, 0) % c
    klane_h = lax.broadcasted_iota(jnp.int32, (c, kw), 1) // dk
    vlane_h = lax.broadcasted_iota(jnp.int32, (c, vw), 1) // dv
    lev = lev_ref[...]

    att = None
    for l in range(nlev + 1):
        if l == 0:
            qt, kt = q.astype(BF16), k.astype(BF16)
        else:
            e = jnp.exp2(-jnp.abs(b - _block_reference(b, c >> l)))
            qt = (q * e).astype(BF16)
            kt = (k * e).astype(BF16)
        parts = []
        for ci in range(nchunk):
            ktc = chunk(kt, ci)
            kstack = jnp.concatenate(
                [jnp.where(klane_h == h, ktc, jnp.zeros_like(ktc)) for h in range(heads)], axis=0)
            parts.append(lax.dot_general(chunk(qt, ci), kstack, _NT, preferred_element_type=F32))
        a = parts[0] if nchunk == 1 else jnp.concatenate(parts, axis=0)
        att = jnp.where(lev == l, a, 0.0 if att is None else att)
    att = att.astype(BF16)
    qe = (q * jnp.exp2(b)).astype(BF16)

    o_intra, upd, decay = [], [], []
    for ci in range(nchunk):
        rows = slice(ci * c, (ci + 1) * c)
        vc = v[rows]
        vbd = jnp.concatenate(
            [jnp.where(vlane_h == h, vc, jnp.zeros_like(vc)) for h in range(heads)], axis=0)
        o_intra.append(jnp.dot(att[rows], vbd, preferred_element_type=F32))
        b_last = b[ci * c + c - 1:ci * c + c, :]
        kd = k[rows] * jnp.exp2(b_last - b[rows])
        x = jnp.concatenate([kd, jnp.zeros((pad, kw), F32), jnp.broadcast_to(b_last, (LANES, kw))], axis=0)
        xt = x.T
        for h in range(heads):
            hk = slice(h * dk, (h + 1) * dk)
            vpad = jnp.concatenate([vc[:, h * dv:(h + 1) * dv], jnp.zeros((pad, dv), BF16)], axis=0)
            upd.append(jnp.dot(xt[hk, :LANES].astype(BF16), vpad, preferred_element_type=F32))
            decay.append(jnp.exp2(xt[hk, LANES:]))

    s = [s_ref[h] for h in range(heads)]
    zero = jnp.zeros((dk, dv), BF16)
    for ci in range(nchunk):
        sbd = jnp.concatenate(
            [jnp.concatenate([s[h].astype(BF16) if g == h else zero for g in range(heads)], axis=1)
             for h in range(heads)], axis=0)
        o_ref[0, ci * c:(ci + 1) * c, :] = o_intra[ci] + jnp.dot(chunk(qe, ci), sbd, preferred_element_type=F32)
        s = [decay[ci * heads + h] * s[h] + upd[ci * heads + h] for h in range(heads)]
    for h in range(heads):
        s_ref[h] = s[h]

    @pl.when(ti == pl.num_programs(1) - 1)
    def _():
        sout_ref[0] = s_ref[...]


def _gla(q, k, v, la, s0, chunk):
    bsz, t, kw = q.shape
    vw = v.shape[-1]
    heads, dk, dv = s0.shape[1:]
    nchunk = min(4, t // chunk)
    tg = nchunk * chunk
    lt3, lev, nlev = _gla_tables(chunk, heads, nchunk)
    tok = lambda w: pl.BlockSpec((1, tg, w), lambda b, i: (b, i, 0))
    st = pl.BlockSpec((1, heads, dk, dv), lambda b, i: (b, 0, 0, 0))
    return pl.pallas_call(
        functools.partial(_gla_kernel, chunk, nchunk, heads, nlev),
        grid=(bsz, t // tg),
        in_specs=[pl.BlockSpec(lt3.shape, lambda b, i: (0, 0)), pl.BlockSpec(lev.shape, lambda b, i: (0, 0)),
                  tok(kw), tok(kw), tok(vw), tok(kw), st],
        out_specs=[tok(vw), st],
        out_shape=[jax.ShapeDtypeStruct((bsz, t, vw), F32), jax.ShapeDtypeStruct(s0.shape, F32)],
        scratch_shapes=[pltpu.VMEM((heads, dk, dv), F32)],
        compiler_params=_cparams(("arbitrary", "arbitrary")),
        name="gla",
    )(jnp.asarray(lt3, BF16), jnp.asarray(lev), q, k, v, la, s0)


def _head_rmsnorm(x, g, heads):
    outs = []
    for h in range(heads):
        blk = x[:, h * LANES:(h + 1) * LANES]
        ms = jnp.mean(blk * blk, axis=-1, keepdims=True)
        outs.append((blk * lax.rsqrt(ms + EPS)) * g)
    return jnp.concatenate(outs, axis=1)


def _merge_kernel(lam_init, final, da_heads, gla_heads, x_ref, oa_ref, ob_ref, dg_ref, gg_ref, mg_ref,
                  ga_ref, gb_ref, wa_ref, wb_ref, wo_ref, gf_ref, y_ref):
    d = x_ref.shape[-1]
    dg = dg_ref[...].astype(F32)
    gg = gg_ref[...].astype(F32)
    oa = (_head_rmsnorm(oa_ref[...], ga_ref[...], da_heads) * (1.0 - lam_init)) * (dg * _sigmoid(dg))
    ob = _head_rmsnorm(ob_ref[...], gb_ref[...], gla_heads) * (gg * _sigmoid(gg))
    ya = jnp.dot(oa.astype(BF16), wa_ref[...], preferred_element_type=F32)
    yb = jnp.dot(ob.astype(BF16), wb_ref[...], preferred_element_type=F32)
    mixed = _sigmoid(mg_ref[:, :d].astype(F32)) * ya + _sigmoid(mg_ref[:, d:].astype(F32)) * yb
    y = x_ref[...] + jnp.dot(mixed.astype(BF16), wo_ref[...], preferred_element_type=F32)
    if final:
        ms = jnp.mean(y * y, axis=-1, keepdims=True)
        y = (y * lax.rsqrt(ms + EPS)) * gf_ref[...]
    y_ref[...] = y


def _merge(x2, oa, ob, dg, gg, mg, da_g, gla_g, wa, wb, wo, gf, lam_init, final, da_heads, gla_heads, tm):
    tok, d = x2.shape
    row = lambda w: pl.BlockSpec((tm, w), lambda i: (i, 0))
    full = lambda a: pl.BlockSpec(a.shape, lambda i: (0, 0))
    da_g = da_g.reshape(1, -1)
    gla_g = gla_g.reshape(1, -1)
    gf = gf.reshape(1, d)
    return pl.pallas_call(
        functools.partial(_merge_kernel, lam_init, final, da_heads, gla_heads),
        grid=(tok // tm,),
        in_specs=[row(d), row(oa.shape[1]), row(ob.shape[1]), row(dg.shape[1]), row(gg.shape[1]), row(mg.shape[1]),
                  full(da_g), full(gla_g), full(wa), full(wb), full(wo), full(gf)],
        out_specs=row(d),
        out_shape=jax.ShapeDtypeStruct((tok, d), F32),
        compiler_params=_cparams(("arbitrary",)),
        name="merge",
    )(x2, oa, ob, dg, gg, mg, da_g, gla_g, wa, wb, wo, gf)


def _layout(da_heads, da_dk, gla_heads, gla_dk, gla_dv):
    da_qk = da_heads * 2 * da_dk
    da_w = da_heads * 2 * da_dk
    kwid = gla_heads * gla_dk
    vwid = gla_heads * gla_dv
    names = ["dq", "dk", "dv", "dg", "gq", "gk", "gv", "gg"]
    sizes = [da_qk, da_qk, da_w, da_w, kwid, kwid, vwid, vwid]
    segs, off = {}, 0
    for nm, sz in zip(names, sizes):
        segs[nm] = (off, off + sz)
        off += sz
    segs["da_scale"] = da_dk ** -0.5 * math.log2(math.e)
    segs["gla_scale"] = gla_dk ** -0.5
    return segs, off


def kernel(x_prompt, x_sample, cache_k, cache_v, state_gla, norm_in_g, w_in, w_alpha_up, b_alpha,
           lambda_q1, lambda_k1, lambda_q2, lambda_k2, da_norm_g, gla_norm_g,
           w_branch_a, w_branch_b, w_out, norm_final_g):
    depth = w_in.shape[0]
    bsz, seq, d = x_prompt.shape
    dbsz, dseq, _ = x_sample.shape
    past = cache_k.shape[2]
    da_heads, da_dk = cache_k.shape[3], cache_k.shape[5]
    gla_heads, gla_dk, gla_dv = state_gla.shape[2:]
    rank = w_alpha_up.shape[1]
    kw = gla_heads * gla_dk
    assert 2 * da_dk == LANES and cache_v.shape[4] == LANES and gla_dv == LANES
    segs, ga_lo = _layout(da_heads, da_dk, gla_heads, gla_dk, gla_dv)

    xp = x_prompt.reshape(bsz * seq, d)
    xs = x_sample.reshape(dbsz * dseq, d)
    tm_p = min(512, bsz * seq)
    tm_s = min(512, dbsz * dseq)
    outs = [[] for _ in range(6)]
    for l in range(depth):
        lam_init = 0.8 - 0.6 * math.exp(-0.3 * l)
        final = l == depth - 1
        w = w_in[l]
        w_gate = jnp.pad(w[:, ga_lo:ga_lo + rank], ((0, 0), (0, LANES - rank))).astype(BF16)
        wkt = w[:, segs["dk"][0]:segs["dk"][1]].T.astype(BF16)
        wa_pad = jnp.pad(w_alpha_up[l], ((0, LANES - rank), (0, 0))).astype(BF16)
        weights = (w[:, :ga_lo].astype(BF16), w[:, ga_lo + rank:].astype(BF16), w_gate, wkt, wa_pad)
        lamv = jnp.stack([lambda_q1[l], lambda_k1[l], lambda_q2[l], lambda_k2[l]])
        wa, wb, wo = w_branch_a[l].astype(BF16), w_branch_b[l].astype(BF16), w_out[l].astype(BF16)

        p = _proj(xp, norm_in_g[l], weights, b_alpha[l], segs, tm_p, da_heads, seq=seq)
        sh = lambda a: a.reshape(bsz, seq, a.shape[-1])
        oa = _attn_prompt(sh(p["q"]), p["kb"], sh(p["vb"]), lamv, lam_init, da_heads)
        ob, s_p = _gla(sh(p["gq"]), sh(p["gk"]), sh(p["gv"]), sh(p["la"]),
                       jnp.zeros((bsz, gla_heads, gla_dk, gla_dv), F32), CHUNK)
        xp = _merge(xp, oa.reshape(bsz * seq, -1), ob.reshape(bsz * seq, -1), p["dg"], p["gg"], p["mg"],
                    da_norm_g[l], gla_norm_g[l], wa, wb, wo, norm_final_g, lam_init, final,
                    da_heads, gla_heads, min(256, tm_p))
        outs[0].append(jnp.transpose(p["kf"].reshape(bsz, da_heads, 2, da_dk, seq), (0, 4, 1, 2, 3)))
        outs[1].append(p["vf"].reshape(bsz, seq, da_heads, 2 * da_dk))
        outs[2].append(s_p)

        p = _proj(xs, norm_in_g[l], weights, b_alpha[l], segs, tm_s, da_heads)
        sh = lambda a: a.reshape(dbsz, dseq, a.shape[-1])
        ckt = jnp.transpose(cache_k[l], (0, 2, 3, 4, 1)).reshape(dbsz, da_heads * 2 * da_dk, past)
        oa = _attn_sample(sh(p["q"]), ckt, cache_v[l].reshape(dbsz, past * da_heads, 2 * da_dk),
                          sh(p["kb"]), sh(p["vb"]), lamv, lam_init, da_heads)
        ob, s_s = _gla(sh(p["gq"]), sh(p["gk"]), sh(p["gv"]), sh(p["la"]), state_gla[l], dseq)
        xs = _merge(xs, oa.reshape(dbsz * dseq, -1), ob.reshape(dbsz * dseq, -1), p["dg"], p["gg"], p["mg"],
                    da_norm_g[l], gla_norm_g[l], wa, wb, wo, norm_final_g, lam_init, final,
                    da_heads, gla_heads, min(256, tm_s))
        outs[3].append(p["kf"].reshape(dbsz, dseq, da_heads, 2, da_dk))
        outs[4].append(p["vf"].reshape(dbsz, dseq, da_heads, 2 * da_dk))
        outs[5].append(s_s)

    return (xp.reshape(bsz, seq, d), xs.reshape(dbsz, dseq, d),
            jnp.stack(outs[0]), jnp.stack(outs[1]), jnp.stack(outs[2]),
            jnp.stack(outs[3]), jnp.stack(outs[4]), jnp.stack(outs[5]))
```

```python
import functools
import math

import numpy as np
import jax
import jax.numpy as jnp
from jax import lax
from jax.experimental import pallas as pl
from jax.experimental.pallas import tpu as pltpu

F32 = jnp.float32
BF16 = jnp.bfloat16

EPS = 1e-6
CHUNK = 64
GLA_TAU = 16.0
LANES = 128
VMEM_LIMIT_BYTES = 56 * 1024 * 1024

_NT = (((1,), (1,)), ((), ()))
_TN = (((0,), (0,)), ((), ()))


def _sigmoid(x):
    return 0.5 * jnp.tanh(0.5 * x) + 0.5


def _log_sigmoid(x):
    return jnp.minimum(x, 0.0) - jnp.log1p(jnp.exp(-jnp.abs(x)))


def _cparams(sem):
    return pltpu.CompilerParams(dimension_semantics=sem, vmem_limit_bytes=VMEM_LIMIT_BYTES)


def _proj_kernel(segs, heads, k_major, x_ref, g_ref, w_ref, wm_ref, wg_ref, wk_ref, wa_ref, ba_ref,
                 q_ref, kf_ref, kb_ref, vf_ref, vb_ref, dg_ref, gq_ref, gk_ref, gv_ref, gg_ref,
                 mg_ref, la_ref):
    x = x_ref[...]
    tm = x.shape[0]
    ms = jnp.mean(x * x, axis=-1, keepdims=True)
    xn = ((x * lax.rsqrt(ms + EPS)) * g_ref[...]).astype(BF16)

    def seg(name):
        lo, hi = segs[name]
        return jnp.dot(xn, w_ref[:, lo:hi], preferred_element_type=F32)

    q_ref[...] = (seg("dq") * segs["da_scale"]).astype(BF16)
    if k_major:
        kt = lax.dot_general(wk_ref[...], xn, _NT, preferred_element_type=F32)
        kf_ref[0] = kt
        kb_ref[0] = kt.astype(BF16)
    else:
        k = seg("dk")
        kf_ref[...] = k
        kb_ref[...] = k.astype(BF16)
    v = seg("dv")
    vb_ref[...] = v.astype(BF16)
    if k_major:
        for h in range(heads):
            vf_ref[pl.ds(h, tm, stride=heads), :] = v[:, h * LANES:(h + 1) * LANES]
    else:
        vf_ref[...] = v
    dg_ref[...] = seg("dg").astype(BF16)
    gq_ref[...] = seg("gq") * segs["gla_scale"]
    gk_ref[...] = seg("gk")
    gv_ref[...] = seg("gv").astype(BF16)
    gg_ref[...] = seg("gg").astype(BF16)
    step = 512
    for c in range(0, wm_ref.shape[1], step):
        mg_ref[:, c:c + step] = jnp.dot(xn, wm_ref[:, c:c + step], preferred_element_type=F32).astype(BF16)
    ga = jnp.dot(xn, wg_ref[...], preferred_element_type=F32).astype(BF16)
    z = jnp.dot(ga, wa_ref[...], preferred_element_type=F32) + ba_ref[...]
    la_ref[...] = _log_sigmoid(z) / GLA_TAU


def _proj(x2, g_in, weights, b_alpha, segs, tm, heads, seq=None):
    tok, d = x2.shape
    w_main, w_merge, w_gate, wkt, wa_pad = weights
    widths = {k: v[1] - v[0] for k, v in segs.items() if isinstance(v, tuple)}
    widths["mg"] = w_merge.shape[1]
    kw = wa_pad.shape[1]
    k_major = seq is not None

    def row_spec(w):
        return pl.BlockSpec((tm, w), lambda i: (i, 0))

    def full_spec(shape):
        return pl.BlockSpec(shape, lambda i: (0, 0))

    def const_spec(shape):
        return pl.BlockSpec(shape, lambda i: (0, 0), pipeline_mode=pl.Buffered(1))

    out_defs = [
        ("q", widths["dq"], BF16), ("kf", widths["dk"], F32), ("kb", widths["dk"], BF16),
        ("vf", widths["dv"], F32), ("vb", widths["dv"], BF16), ("dg", widths["dg"], BF16),
        ("gq", widths["gq"], F32), ("gk", widths["gk"], F32), ("gv", widths["gv"], BF16),
        ("gg", widths["gg"], BF16), ("mg", widths["mg"], BF16), ("la", kw, F32),
    ]
    out_specs = {nm: row_spec(w) for nm, w, _ in out_defs}
    out_shape = {nm: jax.ShapeDtypeStruct((tok, w), dt) for nm, w, dt in out_defs}
    if k_major:
        nt = seq // tm
        for nm, dt in (("kf", F32), ("kb", BF16)):
            out_specs[nm] = pl.BlockSpec((1, widths["dk"], tm), lambda i: (i // nt, 0, i % nt))
            out_shape[nm] = jax.ShapeDtypeStruct((tok // seq, widths["dk"], seq), dt)
        out_specs["vf"] = pl.BlockSpec((tm * heads, LANES), lambda i: (i, 0))
        out_shape["vf"] = jax.ShapeDtypeStruct((tok * heads, LANES), F32)
    names = [nm for nm, _, _ in out_defs]
    outs = pl.pallas_call(
        functools.partial(_proj_kernel, segs, heads, k_major),
        grid=(tok // tm,),
        in_specs=[row_spec(d), full_spec((1, d)), const_spec(w_main.shape), const_spec(w_merge.shape),
                  const_spec(w_gate.shape), const_spec(wkt.shape), full_spec(wa_pad.shape), full_spec((1, kw))],
        out_specs=[out_specs[nm] for nm in names],
        out_shape=[out_shape[nm] for nm in names],
        compiler_params=_cparams(("arbitrary",)),
        name="proj",
    )(x2, g_in.reshape(1, d), w_main, w_merge, w_gate, wkt, wa_pad, b_alpha.reshape(1, kw))
    return dict(zip(names, outs))


def _diff_lambda(lamv, lam_init):
    a = jnp.sum(lamv[0:1, :] * lamv[1:2, :], axis=1, keepdims=True)
    b = jnp.sum(lamv[2:3, :] * lamv[3:4, :], axis=1, keepdims=True)
    return jnp.exp(a) - jnp.exp(b) + lam_init


def _stack_maps(q):
    lane = lax.broadcasted_iota(jnp.int32, q.shape, 1)
    half = q.shape[1] // 2
    zero = jnp.zeros_like(q)
    return jnp.concatenate([jnp.where(lane < half, q, zero), jnp.where(lane >= half, q, zero)], axis=0)


def _attn_prompt_kernel(lam_init, tq, rb, lamv_ref, q_ref, kt_ref, v_ref, o_ref,
                        qx_ref, acc_ref, s0_ref, s1_ref, m0_ref, m1_ref, a0_ref, a1_ref):
    ones = jnp.ones((tq, LANES), BF16)
    sbuf, mbuf, abuf = (s0_ref, s1_ref), (m0_ref, m1_ref), (a0_ref, a1_ref)
    nblk = 2 * tq // rb

    def stage_a(j, slot, masked):
        start = pl.multiple_of(j * tq, tq)
        kt = kt_ref[0, :, pl.ds(start, tq)]
        sbuf[slot][...] = jnp.dot(qx_ref[...], kt, preferred_element_type=F32)
        for r in range(nblk):
            rows = slice(r * rb, (r + 1) * rb)
            s = sbuf[slot][rows, :]
            if masked:
                qc = ((r * rb) % tq + lax.broadcasted_iota(jnp.int32, s.shape, 0)) // CHUNK
                kc = lax.broadcasted_iota(jnp.int32, s.shape, 1) // CHUNK
                s = jnp.where(kc <= qc, s, -jnp.inf)
                sbuf[slot][rows, :] = s
            m_prev = mbuf[1 - slot][rows, :]
            m_new = jnp.maximum(m_prev, jnp.max(s, axis=1, keepdims=True))
            abuf[slot][rows, :] = jnp.exp2(m_prev - m_new)
            mbuf[slot][rows, :] = m_new

    def stage_b(j, slot):
        start = pl.multiple_of(j * tq, tq)
        vx = jnp.concatenate([v_ref[0, pl.ds(start, tq), :], ones], axis=1)
        for r in range(nblk):
            rows = slice(r * rb, (r + 1) * rb)
            p = jnp.exp2(sbuf[slot][rows, :] - jnp.tile(mbuf[slot][rows, :], (1, tq // LANES)))
            pv = jnp.dot(p.astype(BF16), vx, preferred_element_type=F32)
            acc_ref[rows, :] = acc_ref[rows, :] * jnp.tile(abuf[slot][rows, :], (1, 2)) + pv

    def q_tile(i, carry):
        qrows = pl.ds(pl.multiple_of(i * tq, tq), tq)
        qx_ref[...] = _stack_maps(q_ref[0, qrows, :])
        acc_ref[...] = jnp.zeros(acc_ref.shape, F32)
        m1_ref[...] = jnp.full(m1_ref.shape, -jnp.inf, F32)

        stage_a(i, 0, True)

        def pair(u, c):
            stage_a(2 * u, 1, False)
            stage_b(jnp.where(u == 0, i, 2 * u - 1), 0)
            stage_a(2 * u + 1, 0, False)
            stage_b(2 * u, 1)
            return c

        lax.fori_loop(0, i // 2, pair, 0)
        last = jnp.where(i < 2, i, 2 * (i // 2) - 1)

        @pl.when(i % 2 == 0)
        def _():
            stage_b(last, 0)

        @pl.when(i % 2 == 1)
        def _():
            stage_a(i - 1, 1, False)
            stage_b(last, 0)
            stage_b(i - 1, 1)

        acc = acc_ref[...]
        o1 = acc[:tq, :LANES] / acc[:tq, LANES:]
        o2 = acc[tq:, :LANES] / acc[tq:, LANES:]
        o_ref[0, qrows, :] = o1 - _diff_lambda(lamv_ref[...], lam_init) * o2
        return carry

    lax.fori_loop(0, q_ref.shape[1] // tq, q_tile, 0)


def _attn_prompt(q, kt, v, lamv, lam_init, heads):
    bsz, t, w = q.shape
    tq = min(512, t)
    rb = min(512, tq)
    rowbuf = pltpu.VMEM((2 * tq, LANES), F32)
    return pl.pallas_call(
        functools.partial(_attn_prompt_kernel, lam_init, tq, rb),
        grid=(bsz, heads),
        in_specs=[pl.BlockSpec(lamv.shape, lambda b, h: (0, 0)),
                  pl.BlockSpec((1, t, LANES), lambda b, h: (b, 0, h)),
                  pl.BlockSpec((1, LANES, t), lambda b, h: (b, h, 0)),
                  pl.BlockSpec((1, t, LANES), lambda b, h: (b, 0, h))],
        out_specs=pl.BlockSpec((1, t, LANES), lambda b, h: (b, 0, h)),
        out_shape=jax.ShapeDtypeStruct((bsz, t, w), F32),
        scratch_shapes=[pltpu.VMEM((2 * tq, LANES), BF16), pltpu.VMEM((2 * tq, 2 * LANES), F32),
                        pltpu.VMEM((2 * tq, tq), F32), pltpu.VMEM((2 * tq, tq), F32),
                        rowbuf, rowbuf, rowbuf, rowbuf],
        compiler_params=_cparams(("arbitrary", "arbitrary")),
        name="attn_prompt",
    )(lamv, q, kt, v)


def _attn_sample_kernel(lam_init, heads, lamv_ref, q_ref, ckt_ref, cv_ref, kn_ref, vn_ref, o_ref):
    tq = q_ref.shape[1]
    past = ckt_ref.shape[2]
    lam = _diff_lambda(lamv_ref[...], lam_init)
    for h in range(heads):
        sl = slice(h * LANES, (h + 1) * LANES)
        qx = _stack_maps(q_ref[0, :, sl])
        kc = ckt_ref[0, sl, :].astype(BF16)
        vc = cv_ref[0, pl.ds(h, past, stride=heads), :].astype(BF16)
        kn = kn_ref[0, :, sl]
        vn = vn_ref[0, :, sl]
        sc = jnp.dot(qx, kc, preferred_element_type=F32)
        sn = lax.dot_general(qx, kn, _NT, preferred_element_type=F32)
        m = jnp.maximum(jnp.max(sc, axis=1, keepdims=True), jnp.max(sn, axis=1, keepdims=True))
        ec = jnp.exp2(sc - m)
        en = jnp.exp2(sn - m)
        l = jnp.sum(ec, axis=1, keepdims=True) + jnp.sum(en, axis=1, keepdims=True)
        o = (jnp.dot(ec.astype(BF16), vc, preferred_element_type=F32)
             + jnp.dot(en.astype(BF16), vn, preferred_element_type=F32)) / l
        o_ref[0, :, sl] = o[:tq] - lam * o[tq:]


def _attn_sample(q, ckt, cv, kn, vn, lamv, lam_init, heads):
    bsz, tq, w = q.shape
    new_spec = pl.BlockSpec((1, tq, w), lambda b: (b, 0, 0))
    return pl.pallas_call(
        functools.partial(_attn_sample_kernel, lam_init, heads),
        grid=(bsz,),
        in_specs=[pl.BlockSpec(lamv.shape, lambda b: (0, 0)), new_spec,
                  pl.BlockSpec((1,) + ckt.shape[1:], lambda b: (b, 0, 0)),
                  pl.BlockSpec((1,) + cv.shape[1:], lambda b: (b, 0, 0)), new_spec, new_spec],
        out_specs=new_spec,
        out_shape=jax.ShapeDtypeStruct((bsz, tq, w), F32),
        compiler_params=_cparams(("arbitrary",)),
        name="attn_sample",
    )(lamv, q, ckt, cv, kn, vn)


def _gla_tables(c, heads, nchunk):
    nlev = int(math.log2(c))
    assert 1 << nlev == c and c % 8 == 0
    t = np.arange(c)[:, None]
    u = np.arange(c)[None, :]
    lev = np.where(t == u, 0, -1)
    for l in range(1, nlev + 1):
        n = c >> l
        same_block = (t // (2 * n)) == (u // (2 * n))
        lev = np.where(same_block & ((t % (2 * n)) >= n) & ((u % (2 * n)) < n), l, lev)
    lt = (u <= t).astype(np.float32)
    return np.concatenate([lt, lt, lt], axis=1), np.tile(lev, (nchunk, heads)).astype(np.int32), nlev


def _block_reference(b, n):
    rows, w = b.shape
    if n >= 8:
        parts = [jnp.broadcast_to(b[g + n - 1:g + n, :], (2 * n, w)) for g in range(0, rows, 2 * n)]
        return parts[0] if len(parts) == 1 else jnp.concatenate(parts, axis=0)
    b3 = b.reshape(rows // 8, 8, w)
    sub = lax.broadcasted_iota(jnp.int32, b3.shape, 1)
    pick = lambda j: jnp.broadcast_to(b3[:, j:j + 1, :], b3.shape)
    if n == 4:
        ref = pick(3)
    elif n == 2:
        ref = jnp.where(sub < 4, pick(1), pick(5))
    else:
        ref = jnp.where(sub < 2, pick(0), jnp.where(sub < 4, pick(2), jnp.where(sub < 6, pick(4), pick(6))))
    return ref.reshape(rows, w)


def _gla_kernel(c, nchunk, heads, nlev, lt_ref, lev_ref, q_ref, k_ref, v_ref, la_ref, s0_ref,
                o_ref, sout_ref, s_ref):
    ti = pl.program_id(1)
    kw = q_ref.shape[-1]
    vw = v_ref.shape[-1]
    dk = kw // heads
    dv = vw // heads
    pad = LANES - c
    assert dv == LANES and pad >= 0

    @pl.when(ti == 0)
    def _():
        s_ref[...] = s0_ref[0]

    q = q_ref[0]
    k = k_ref[0]
    v = v_ref[0]
    chunk = lambda a, ci: a[ci * c:(ci + 1) * c]

    la2 = la_ref[0] * math.log2(math.e)
    p0 = la2.astype(BF16)
    r0 = la2 - p0.astype(F32)
    p1 = r0.astype(BF16)
    p2 = (r0 - p1.astype(F32)).astype(BF16)
    lt = lt_ref[...]
    b = jnp.concatenate(
        [jnp.dot(lt, jnp.concatenate([chunk(p0, ci), chunk(p1, ci), chunk(p2, ci)], axis=0),
                 preferred_element_type=F32) for ci in range(nchunk)], axis=0)

    t = lax.broadcasted_iota(jnp.int32, b.shape, 0) % c
    klane_h = lax.broadcasted_iota(jnp.int32, (c, kw), 1) // dk
    vlane_h = lax.broadcasted_iota(jnp.int32, (c, vw), 1) // dv
    lev = lev_ref[...]

    att = None
    for l in range(nlev + 1):
        if l == 0:
            qt, kt = q.astype(BF16), k.astype(BF16)
        else:
            e = jnp.exp2(-jnp.abs(b - _block_reference(b, c >> l)))
            qt = (q * e).astype(BF16)
            kt = (k * e).astype(BF16)
        parts = []
        for ci in range(nchunk):
            ktc = chunk(kt, ci)
            kstack = jnp.concatenate(
                [jnp.where(klane_h == h, ktc, jnp.zeros_like(ktc)) for h in range(heads)], axis=0)
            parts.append(lax.dot_general(chunk(qt, ci), kstack, _NT, preferred_element_type=F32))
        a = parts[0] if nchunk == 1 else jnp.concatenate(parts, axis=0)
        att = jnp.where(lev == l, a, 0.0 if att is None else att)
    att = att.astype(BF16)
    qe = (q * jnp.exp2(b)).astype(BF16)

    o_intra, upd, decay = [], [], []
    for ci in range(nchunk):
        rows = slice(ci * c, (ci + 1) * c)
        vc = v[rows]
        vbd = jnp.concatenate(
            [jnp.where(vlane_h == h, vc, jnp.zeros_like(vc)) for h in range(heads)], axis=0)
        o_intra.append(jnp.dot(att[rows], vbd, preferred_element_type=F32))
        b_last = b[ci * c + c - 1:ci * c + c, :]
        kd = k[rows] * jnp.exp2(b_last - b[rows])
        x = jnp.concatenate([kd, jnp.zeros((pad, kw), F32), jnp.broadcast_to(b_last, (LANES, kw))], axis=0)
        xt = x.T
        for h in range(heads):
            hk = slice(h * dk, (h + 1) * dk)
            vpad = jnp.concatenate([vc[:, h * dv:(h + 1) * dv], jnp.zeros((pad, dv), BF16)], axis=0)
            upd.append(jnp.dot(xt[hk, :LANES].astype(BF16), vpad, preferred_element_type=F32))
            decay.append(jnp.exp2(xt[hk, LANES:]))

    s = [s_ref[h] for h in range(heads)]
    zero = jnp.zeros((dk, dv), BF16)
    for ci in range(nchunk):
        sbd = jnp.concatenate(
            [jnp.concatenate([s[h].astype(BF16) if g == h else zero for g in range(heads)], axis=1)
             for h in range(heads)], axis=0)
        o_ref[0, ci * c:(ci + 1) * c, :] = o_intra[ci] + jnp.dot(chunk(qe, ci), sbd, preferred_element_type=F32)
        s = [decay[ci * heads + h] * s[h] + upd[ci * heads + h] for h in range(heads)]
    for h in range(heads):
        s_ref[h] = s[h]

    @pl.when(ti == pl.num_programs(1) - 1)
    def _():
        sout_ref[0] = s_ref[...]


def _gla(q, k, v, la, s0, chunk):
    bsz, t, kw = q.shape
    vw = v.shape[-1]
    heads, dk, dv = s0.shape[1:]
    nchunk = min(4, t // chunk)
    tg = nchunk * chunk
    lt3, lev, nlev = _gla_tables(chunk, heads, nchunk)
    tok = lambda w: pl.BlockSpec((1, tg, w), lambda b, i: (b, i, 0))
    st = pl.BlockSpec((1, heads, dk, dv), lambda b, i: (b, 0, 0, 0))
    return pl.pallas_call(
        functools.partial(_gla_kernel, chunk, nchunk, heads, nlev),
        grid=(bsz, t // tg),
        in_specs=[pl.BlockSpec(lt3.shape, lambda b, i: (0, 0)), pl.BlockSpec(lev.shape, lambda b, i: (0, 0)),
                  tok(kw), tok(kw), tok(vw), tok(kw), st],
        out_specs=[tok(vw), st],
        out_shape=[jax.ShapeDtypeStruct((bsz, t, vw), F32), jax.ShapeDtypeStruct(s0.shape, F32)],
        scratch_shapes=[pltpu.VMEM((heads, dk, dv), F32)],
        compiler_params=_cparams(("arbitrary", "arbitrary")),
        name="gla",
    )(jnp.asarray(lt3, BF16), jnp.asarray(lev), q, k, v, la, s0)


def _head_rmsnorm(x, g, heads):
    outs = []
    for h in range(heads):
        blk = x[:, h * LANES:(h + 1) * LANES]
        ms = jnp.mean(blk * blk, axis=-1, keepdims=True)
        outs.append((blk * lax.rsqrt(ms + EPS)) * g)
    return jnp.concatenate(outs, axis=1)


def _merge_kernel(lam_init, final, da_heads, gla_heads, x_ref, oa_ref, ob_ref, dg_ref, gg_ref, mg_ref,
                  ga_ref, gb_ref, wa_ref, wb_ref, wo_ref, gf_ref, y_ref):
    d = x_ref.shape[-1]
    dg = dg_ref[...].astype(F32)
    gg = gg_ref[...].astype(F32)
    oa = (_head_rmsnorm(oa_ref[...], ga_ref[...], da_heads) * (1.0 - lam_init)) * (dg * _sigmoid(dg))
    ob = _head_rmsnorm(ob_ref[...], gb_ref[...], gla_heads) * (gg * _sigmoid(gg))
    ya = jnp.dot(oa.astype(BF16), wa_ref[...], preferred_element_type=F32)
    yb = jnp.dot(ob.astype(BF16), wb_ref[...], preferred_element_type=F32)
    mixed = _sigmoid(mg_ref[:, :d].astype(F32)) * ya + _sigmoid(mg_ref[:, d:].astype(F32)) * yb
    y = x_ref[...] + jnp.dot(mixed.astype(BF16), wo_ref[...], preferred_element_type=F32)
    if final:
        ms = jnp.mean(y * y, axis=-1, keepdims=True)
        y = (y * lax.rsqrt(ms + EPS)) * gf_ref[...]
    y_ref[...] = y


def _merge(x2, oa, ob, dg, gg, mg, da_g, gla_g, wa, wb, wo, gf, lam_init, final, da_heads, gla_heads, tm):
    tok, d = x2.shape
    row = lambda w: pl.BlockSpec((tm, w), lambda i: (i, 0))
    full = lambda a: pl.BlockSpec(a.shape, lambda i: (0, 0))
    da_g = da_g.reshape(1, -1)
    gla_g = gla_g.reshape(1, -1)
    gf = gf.reshape(1, d)
    return pl.pallas_call(
        functools.partial(_merge_kernel, lam_init, final, da_heads, gla_heads),
        grid=(tok // tm,),
        in_specs=[row(d), row(oa.shape[1]), row(ob.shape[1]), row(dg.shape[1]), row(gg.shape[1]), row(mg.shape[1]),
                  full(da_g), full(gla_g), full(wa), full(wb), full(wo), full(gf)],
        out_specs=row(d),
        out_shape=jax.ShapeDtypeStruct((tok, d), F32),
        compiler_params=_cparams(("arbitrary",)),
        name="merge",
    )(x2, oa, ob, dg, gg, mg, da_g, gla_g, wa, wb, wo, gf)


def _layout(da_heads, da_dk, gla_heads, gla_dk, gla_dv):
    da_qk = da_heads * 2 * da_dk
    da_w = da_heads * 2 * da_dk
    kwid = gla_heads * gla_dk
    vwid = gla_heads * gla_dv
    names = ["dq", "dk", "dv", "dg", "gq", "gk", "gv", "gg"]
    sizes = [da_qk, da_qk, da_w, da_w, kwid, kwid, vwid, vwid]
    segs, off = {}, 0
    for nm, sz in zip(names, sizes):
        segs[nm] = (off, off + sz)
        off += sz
    segs["da_scale"] = da_dk ** -0.5 * math.log2(math.e)
    segs["gla_scale"] = gla_dk ** -0.5
    return segs, off


def kernel(x_prompt, x_sample, cache_k, cache_v, state_gla, norm_in_g, w_in, w_alpha_up, b_alpha,
           lambda_q1, lambda_k1, lambda_q2, lambda_k2, da_norm_g, gla_norm_g,
           w_branch_a, w_branch_b, w_out, norm_final_g):
    depth = w_in.shape[0]
    bsz, seq, d = x_prompt.shape
    dbsz, dseq, _ = x_sample.shape
    past = cache_k.shape[2]
    da_heads, da_dk = cache_k.shape[3], cache_k.shape[5]
    gla_heads, gla_dk, gla_dv = state_gla.shape[2:]
    rank = w_alpha_up.shape[1]
    kw = gla_heads * gla_dk
    assert 2 * da_dk == LANES and cache_v.shape[4] == LANES and gla_dv == LANES
    segs, ga_lo = _layout(da_heads, da_dk, gla_heads, gla_dk, gla_dv)

    xp = x_prompt.reshape(bsz * seq, d)
    xs = x_sample.reshape(dbsz * dseq, d)
    tm_p = min(512, bsz * seq)
    tm_s = min(512, dbsz * dseq)
    outs = [[] for _ in range(6)]
    for l in range(depth):
        lam_init = 0.8 - 0.6 * math.exp(-0.3 * l)
        final = l == depth - 1
        w = w_in[l]
        w_gate = jnp.pad(w[:, ga_lo:ga_lo + rank], ((0, 0), (0, LANES - rank))).astype(BF16)
        wkt = w[:, segs["dk"][0]:segs["dk"][1]].T.astype(BF16)
        wa_pad = jnp.pad(w_alpha_up[l], ((0, LANES - rank), (0, 0))).astype(BF16)
        weights = (w[:, :ga_lo].astype(BF16), w[:, ga_lo + rank:].astype(BF16), w_gate, wkt, wa_pad)
        lamv = jnp.stack([lambda_q1[l], lambda_k1[l], lambda_q2[l], lambda_k2[l]])
        wa, wb, wo = w_branch_a[l].astype(BF16), w_branch_b[l].astype(BF16), w_out[l].astype(BF16)

        p = _proj(xp, norm_in_g[l], weights, b_alpha[l], segs, tm_p, da_heads, seq=seq)
        sh = lambda a: a.reshape(bsz, seq, a.shape[-1])
        oa = _attn_prompt(sh(p["q"]), p["kb"], sh(p["vb"]), lamv, lam_init, da_heads)
        ob, s_p = _gla(sh(p["gq"]), sh(p["gk"]), sh(p["gv"]), sh(p["la"]),
                       jnp.zeros((bsz, gla_heads, gla_dk, gla_dv), F32), CHUNK)
        xp = _merge(xp, oa.reshape(bsz * seq, -1), ob.reshape(bsz * seq, -1), p["dg"], p["gg"], p["mg"],
                    da_norm_g[l], gla_norm_g[l], wa, wb, wo, norm_final_g, lam_init, final,
                    da_heads, gla_heads, tm_p)
        outs[0].append(jnp.transpose(p["kf"].reshape(bsz, da_heads, 2, da_dk, seq), (0, 4, 1, 2, 3)))
        outs[1].append(p["vf"].reshape(bsz, seq, da_heads, 2 * da_dk))
        outs[2].append(s_p)

        p = _proj(xs, norm_in_g[l], weights, b_alpha[l], segs, tm_s, da_heads)
        sh = lambda a: a.reshape(dbsz, dseq, a.shape[-1])
        ckt = jnp.transpose(cache_k[l], (0, 2, 3, 4, 1)).reshape(dbsz, da_heads * 2 * da_dk, past)
        oa = _attn_sample(sh(p["q"]), ckt, cache_v[l].reshape(dbsz, past * da_heads, 2 * da_dk),
                          sh(p["kb"]), sh(p["vb"]), lamv, lam_init, da_heads)
        ob, s_s = _gla(sh(p["gq"]), sh(p["gk"]), sh(p["gv"]), sh(p["la"]), state_gla[l], dseq)
        xs = _merge(xs, oa.reshape(dbsz * dseq, -1), ob.reshape(dbsz * dseq, -1), p["dg"], p["gg"], p["mg"],
                    da_norm_g[l], gla_norm_g[l], wa, wb, wo, norm_final_g, lam_init, final,
                    da_heads, gla_heads, tm_s)
        outs[3].append(p["kf"].reshape(dbsz, dseq, da_heads, 2, da_dk))
        outs[4].append(p["vf"].reshape(dbsz, dseq, da_heads, 2 * da_dk))
        outs[5].append(s_s)

    return (xp.reshape(bsz, seq, d), xs.reshape(dbsz, dseq, d),
            jnp.stack(outs[0]), jnp.stack(outs[1]), jnp.stack(outs[2]),
            jnp.stack(outs[3]), jnp.stack(outs[4]), jnp.stack(outs[5]))
```

```python
import functools
import math

import numpy as np
import jax
import jax.numpy as jnp
from jax import lax
from jax.experimental import pallas as pl
from jax.experimental.pallas import tpu as pltpu

F32 = jnp.float32
BF16 = jnp.bfloat16

EPS = 1e-6
CHUNK = 64
GLA_TAU = 16.0
LANES = 128
VMEM_LIMIT_BYTES = 56 * 1024 * 1024

_NT = (((1,), (1,)), ((), ()))
_TN = (((0,), (0,)), ((), ()))


def _sigmoid(x):
    return 0.5 * jnp.tanh(0.5 * x) + 0.5


def _log_sigmoid(x):
    return jnp.minimum(x, 0.0) - jnp.log1p(jnp.exp(-jnp.abs(x)))


def _cparams(sem):
    return pltpu.CompilerParams(dimension_semantics=sem, vmem_limit_bytes=VMEM_LIMIT_BYTES)


def _proj_kernel(segs, heads, seq_tiles, gla, in_names, out_names, *refs):
    n_in, n_out = len(in_names), len(out_names)
    r = dict(zip(in_names + out_names, refs[:n_in + n_out]))
    x = r["x"][...]
    tm = x.shape[0]
    ms = jnp.mean(x * x, axis=-1, keepdims=True)
    xn = ((x * lax.rsqrt(ms + EPS)) * r["g"][...]).astype(BF16)

    def seg(name):
        lo, hi = segs[name]
        return jnp.dot(xn, r["w_main"][:, lo:hi], preferred_element_type=F32)

    def put_q():
        r["q"][...] = (seg("dq") * segs["da_scale"]).astype(BF16)

    def put_k():
        if seq_tiles is not None:
            kt = lax.dot_general(r["wkt"][...], xn, _NT, preferred_element_type=F32)
            r["kf"][0] = kt
            r["kb"][0] = kt.astype(BF16)
        else:
            k = seg("dk")
            r["kf"][...] = k
            r["kb"][...] = k.astype(BF16)

    def put_v():
        v = seg("dv")
        r["vb"][...] = v.astype(BF16)
        if seq_tiles is not None:
            for h in range(heads):
                r["vf"][pl.ds(h, tm, stride=heads), :] = v[:, h * LANES:(h + 1) * LANES]
        else:
            r["vf"][...] = v

    def put_gate(name):
        r[name][...] = seg(name).astype(BF16)

    def put_merge(c0, step):
        r["mg"][:, c0:c0 + step] = jnp.dot(xn, r["w_merge"][:, c0:c0 + step],
                                           preferred_element_type=F32).astype(BF16)

    step = 512
    rest = [put_q, put_k, put_v, functools.partial(put_gate, "dg"), functools.partial(put_gate, "gg")]
    rest += [functools.partial(put_merge, c0, step) for c0 in range(0, r["w_merge"].shape[1], step)]

    ga = jnp.dot(xn, r["w_gate"][...], preferred_element_type=F32).astype(BF16)
    z = jnp.dot(ga, r["wa"][...], preferred_element_type=F32) + r["ba"][...]
    la = _log_sigmoid(z) / GLA_TAU
    gq = seg("gq") * segs["gla_scale"]
    gk = seg("gk")
    gv = seg("gv").astype(BF16)
    if gla is None:
        r["gq"][...] = gq
        r["gk"][...] = gk
        r["gv"][...] = gv
        r["la"][...] = la
        for f in rest:
            f()
    else:
        c, nchunk, gheads, nlev = gla
        s_ref = refs[n_in + n_out]
        ti = pl.program_id(0) % seq_tiles

        @pl.when(ti == 0)
        def _():
            s_ref[...] = r["s0"][0]

        outs, s = _gla_block(c, nchunk, gheads, nlev, r["lt"][...], r["lev"][...], gq, gk, gv, la,
                             [s_ref[h] for h in range(gheads)], fillers=rest)
        for ci, o in enumerate(outs):
            r["ob"][ci * c:(ci + 1) * c, :] = o
        for h in range(gheads):
            s_ref[h] = s[h]

        @pl.when(ti == seq_tiles - 1)
        def _():
            r["sout"][0] = s_ref[...]


def _proj(x2, g_in, weights, b_alpha, segs, tm, heads, seq=None, gla_state=None):
    tok, d = x2.shape
    w_main, w_merge, w_gate, wkt, wa_pad = weights
    widths = {k: v[1] - v[0] for k, v in segs.items() if isinstance(v, tuple)}
    widths["mg"] = w_merge.shape[1]
    kw = wa_pad.shape[1]
    seq_tiles = None if seq is None else seq // tm

    def row_spec(w):
        return pl.BlockSpec((tm, w), lambda i: (i, 0))

    def full_spec(shape):
        return pl.BlockSpec(shape, lambda i: (0,) * len(shape))

    def const_spec(shape):
        return pl.BlockSpec(shape, lambda i: (0,) * len(shape), pipeline_mode=pl.Buffered(1))

    ins = [("x", x2, row_spec(d)), ("g", g_in.reshape(1, d), full_spec((1, d))),
           ("w_main", w_main, const_spec(w_main.shape)), ("w_merge", w_merge, const_spec(w_merge.shape)),
           ("w_gate", w_gate, const_spec(w_gate.shape)), ("wkt", wkt, const_spec(wkt.shape)),
           ("wa", wa_pad, full_spec(wa_pad.shape)), ("ba", b_alpha.reshape(1, kw), full_spec((1, kw)))]
    out_defs = [
        ("q", widths["dq"], BF16), ("kf", widths["dk"], F32), ("kb", widths["dk"], BF16),
        ("vf", widths["dv"], F32), ("vb", widths["dv"], BF16), ("dg", widths["dg"], BF16),
        ("gg", widths["gg"], BF16), ("mg", widths["mg"], BF16),
    ]
    gla, scratch = None, []
    if gla_state is None:
        out_defs += [("gq", widths["gq"], F32), ("gk", widths["gk"], F32), ("gv", widths["gv"], BF16),
                     ("la", kw, F32)]
    else:
        out_defs += [("ob", widths["gv"], F32)]
    out_specs = {nm: row_spec(w) for nm, w, _ in out_defs}
    out_shape = {nm: jax.ShapeDtypeStruct((tok, w), dt) for nm, w, dt in out_defs}
    names = [nm for nm, _, _ in out_defs]
    if seq is not None:
        nt = seq_tiles
        for nm, dt in (("kf", F32), ("kb", BF16)):
            out_specs[nm] = pl.BlockSpec((1, widths["dk"], tm), lambda i: (i // nt, 0, i % nt))
            out_shape[nm] = jax.ShapeDtypeStruct((tok // seq, widths["dk"], seq), dt)
        out_specs["vf"] = pl.BlockSpec((tm * heads, LANES), lambda i: (i, 0))
        out_shape["vf"] = jax.ShapeDtypeStruct((tok * heads, LANES), F32)
    if gla_state is not None:
        gheads, dk, dv = gla_state.shape[1:]
        nchunk = tm // CHUNK
        lt3, lev, nlev = _gla_tables(CHUNK, gheads, nchunk)
        gla = (CHUNK, nchunk, gheads, nlev)
        st = pl.BlockSpec((1, gheads, dk, dv), lambda i: (i // nt, 0, 0, 0))
        ins += [("lt", jnp.asarray(lt3, BF16), full_spec(lt3.shape)), ("lev", jnp.asarray(lev), full_spec(lev.shape)),
                ("s0", gla_state, st)]
        names.append("sout")
        out_specs["sout"] = st
        out_shape["sout"] = jax.ShapeDtypeStruct(gla_state.shape, F32)
        scratch = [pltpu.VMEM((gheads, dk, dv), F32)]
    in_names = [nm for nm, _, _ in ins]
    outs = pl.pallas_call(
        functools.partial(_proj_kernel, segs, heads, seq_tiles, gla, in_names, names),
        grid=(tok // tm,),
        in_specs=[sp for _, _, sp in ins],
        out_specs=[out_specs[nm] for nm in names],
        out_shape=[out_shape[nm] for nm in names],
        scratch_shapes=scratch,
        compiler_params=_cparams(("arbitrary",)),
        name="proj",
    )(*[a for _, a, _ in ins])
    return dict(zip(names, outs))


def _diff_lambda(lamv, lam_init):
    a = jnp.sum(lamv[0:1, :] * lamv[1:2, :], axis=1, keepdims=True)
    b = jnp.sum(lamv[2:3, :] * lamv[3:4, :], axis=1, keepdims=True)
    return jnp.exp(a) - jnp.exp(b) + lam_init


def _stack_maps(q):
    lane = lax.broadcasted_iota(jnp.int32, q.shape, 1)
    half = q.shape[1] // 2
    zero = jnp.zeros_like(q)
    return jnp.concatenate([jnp.where(lane < half, q, zero), jnp.where(lane >= half, q, zero)], axis=0)


def _attn_prompt_kernel(lam_init, tq, ra, rb, lamv_ref, q_ref, kt_ref, v_ref, o_ref,
                        qx_ref, acc_ref, s0_ref, s1_ref, m0_ref, m1_ref, a0_ref, a1_ref):
    ones = jnp.ones((tq, LANES), BF16)
    sbuf, mbuf, abuf = (s0_ref, s1_ref), (m0_ref, m1_ref), (a0_ref, a1_ref)

    def stage_a(j, slot, masked):
        start = pl.multiple_of(j * tq, tq)
        kt = kt_ref[0, :, pl.ds(start, tq)]
        sbuf[slot][...] = jnp.dot(qx_ref[...], kt, preferred_element_type=F32)
        for r in range(2 * tq // ra):
            rows = slice(r * ra, (r + 1) * ra)
            s = sbuf[slot][rows, :]
            if masked:
                qc = ((r * ra) % tq + lax.broadcasted_iota(jnp.int32, s.shape, 0)) // CHUNK
                kc = lax.broadcasted_iota(jnp.int32, s.shape, 1) // CHUNK
                s = jnp.where(kc <= qc, s, -jnp.inf)
                sbuf[slot][rows, :] = s
            m_prev = mbuf[1 - slot][rows, :]
            m_new = jnp.maximum(m_prev, jnp.max(s, axis=1, keepdims=True))
            abuf[slot][rows, :] = jnp.exp2(m_prev - m_new)
            mbuf[slot][rows, :] = m_new

    def stage_b(j, slot):
        start = pl.multiple_of(j * tq, tq)
        vx = jnp.concatenate([v_ref[0, pl.ds(start, tq), :], ones], axis=1)
        for r in range(2 * tq // rb):
            rows = slice(r * rb, (r + 1) * rb)
            p = jnp.exp2(sbuf[slot][rows, :] - jnp.tile(mbuf[slot][rows, :], (1, tq // LANES)))
            pv = jnp.dot(p.astype(BF16), vx, preferred_element_type=F32)
            acc_ref[rows, :] = acc_ref[rows, :] * jnp.tile(abuf[slot][rows, :], (1, 2)) + pv

    def q_tile(i, carry):
        qrows = pl.ds(pl.multiple_of(i * tq, tq), tq)
        qx_ref[...] = _stack_maps(q_ref[0, qrows, :])
        m1_ref[...] = jnp.full(m1_ref.shape, -jnp.inf, F32)

        stage_a(i, 0, True)
        acc_ref[...] = jnp.zeros(acc_ref.shape, F32)

        def pair(u, c):
            stage_a(2 * u, 1, False)
            stage_b(jnp.where(u == 0, i, 2 * u - 1), 0)
            stage_a(2 * u + 1, 0, False)
            stage_b(2 * u, 1)
            return c

        lax.fori_loop(0, i // 2, pair, 0)
        last = jnp.where(i < 2, i, 2 * (i // 2) - 1)

        @pl.when(i % 2 == 0)
        def _():
            stage_b(last, 0)

        @pl.when(i % 2 == 1)
        def _():
            stage_a(i - 1, 1, False)
            stage_b(last, 0)
            stage_b(i - 1, 1)

        acc = acc_ref[...]
        o1 = acc[:tq, :LANES] / acc[:tq, LANES:]
        o2 = acc[tq:, :LANES] / acc[tq:, LANES:]
        o_ref[0, qrows, :] = o1 - _diff_lambda(lamv_ref[...], lam_init) * o2
        return carry

    lax.fori_loop(0, q_ref.shape[1] // tq, q_tile, 0)


def _attn_prompt(q, kt, v, lamv, lam_init, heads):
    bsz, t, w = q.shape
    tq = min(512, t)
    ra = min(128, tq)
    rb = min(512, tq)
    rowbuf = pltpu.VMEM((2 * tq, LANES), F32)
    return pl.pallas_call(
        functools.partial(_attn_prompt_kernel, lam_init, tq, ra, rb),
        grid=(bsz, heads),
        in_specs=[pl.BlockSpec(lamv.shape, lambda b, h: (0, 0)),
                  pl.BlockSpec((1, t, LANES), lambda b, h: (b, 0, h)),
                  pl.BlockSpec((1, LANES, t), lambda b, h: (b, h, 0)),
                  pl.BlockSpec((1, t, LANES), lambda b, h: (b, 0, h))],
        out_specs=pl.BlockSpec((1, t, LANES), lambda b, h: (b, 0, h)),
        out_shape=jax.ShapeDtypeStruct((bsz, t, w), F32),
        scratch_shapes=[pltpu.VMEM((2 * tq, LANES), BF16), pltpu.VMEM((2 * tq, 2 * LANES), F32),
                        pltpu.VMEM((2 * tq, tq), F32), pltpu.VMEM((2 * tq, tq), F32),
                        rowbuf, rowbuf, rowbuf, rowbuf],
        compiler_params=_cparams(("arbitrary", "arbitrary")),
        name="attn_prompt",
    )(lamv, q, kt, v)


def _attn_sample_kernel(lam_init, heads, lamv_ref, q_ref, ckt_ref, cv_ref, kn_ref, vn_ref, o_ref):
    tq = q_ref.shape[1]
    past = ckt_ref.shape[2]
    lam = _diff_lambda(lamv_ref[...], lam_init)
    for h in range(heads):
        sl = slice(h * LANES, (h + 1) * LANES)
        qx = _stack_maps(q_ref[0, :, sl])
        kc = ckt_ref[0, sl, :].astype(BF16)
        vc = cv_ref[0, pl.ds(h, past, stride=heads), :].astype(BF16)
        kn = kn_ref[0, :, sl]
        vn = vn_ref[0, :, sl]
        sc = jnp.dot(qx, kc, preferred_element_type=F32)
        sn = lax.dot_general(qx, kn, _NT, preferred_element_type=F32)
        m = jnp.maximum(jnp.max(sc, axis=1, keepdims=True), jnp.max(sn, axis=1, keepdims=True))
        ec = jnp.exp2(sc - m)
        en = jnp.exp2(sn - m)
        l = jnp.sum(ec, axis=1, keepdims=True) + jnp.sum(en, axis=1, keepdims=True)
        o = (jnp.dot(ec.astype(BF16), vc, preferred_element_type=F32)
             + jnp.dot(en.astype(BF16), vn, preferred_element_type=F32)) / l
        o_ref[0, :, sl] = o[:tq] - lam * o[tq:]


def _attn_sample(q, ckt, cv, kn, vn, lamv, lam_init, heads):
    bsz, tq, w = q.shape
    new_spec = pl.BlockSpec((1, tq, w), lambda b: (b, 0, 0))
    return pl.pallas_call(
        functools.partial(_attn_sample_kernel, lam_init, heads),
        grid=(bsz,),
        in_specs=[pl.BlockSpec(lamv.shape, lambda b: (0, 0)), new_spec,
                  pl.BlockSpec((1,) + ckt.shape[1:], lambda b: (b, 0, 0)),
                  pl.BlockSpec((1,) + cv.shape[1:], lambda b: (b, 0, 0)), new_spec, new_spec],
        out_specs=new_spec,
        out_shape=jax.ShapeDtypeStruct((bsz, tq, w), F32),
        compiler_params=_cparams(("arbitrary",)),
        name="attn_sample",
    )(lamv, q, ckt, cv, kn, vn)


def _gla_tables(c, heads, nchunk):
    nlev = int(math.log2(c))
    assert 1 << nlev == c and c % 8 == 0
    t = np.arange(c)[:, None]
    u = np.arange(c)[None, :]
    lev = np.where(t == u, 0, -1)
    for l in range(1, nlev + 1):
        n = c >> l
        same_block = (t // (2 * n)) == (u // (2 * n))
        lev = np.where(same_block & ((t % (2 * n)) >= n) & ((u % (2 * n)) < n), l, lev)
    lt = (u <= t).astype(np.float32)
    return np.concatenate([lt, lt, lt], axis=1), np.tile(lev, (nchunk, heads)).astype(np.int32), nlev


def _block_reference(b, n):
    rows, w = b.shape
    if n >= 8:
        parts = [jnp.broadcast_to(b[g + n - 1:g + n, :], (2 * n, w)) for g in range(0, rows, 2 * n)]
        return parts[0] if len(parts) == 1 else jnp.concatenate(parts, axis=0)
    b3 = b.reshape(rows // 8, 8, w)
    sub = lax.broadcasted_iota(jnp.int32, b3.shape, 1)
    pick = lambda j: jnp.broadcast_to(b3[:, j:j + 1, :], b3.shape)
    if n == 4:
        ref = pick(3)
    elif n == 2:
        ref = jnp.where(sub < 4, pick(1), pick(5))
    else:
        ref = jnp.where(sub < 2, pick(0), jnp.where(sub < 4, pick(2), jnp.where(sub < 6, pick(4), pick(6))))
    return ref.reshape(rows, w)


def _gla_block(c, nchunk, heads, nlev, lt, lev, q, k, v, la, s, fillers=()):
    fillers = list(fillers)
    fill = lambda: fillers.pop(0)() if fillers else None
    kw = q.shape[-1]
    vw = v.shape[-1]
    dk = kw // heads
    dv = vw // heads
    pad = LANES - c
    assert dv == LANES and pad >= 0
    chunk = lambda a, ci: a[ci * c:(ci + 1) * c]

    la2 = la * math.log2(math.e)
    p0 = la2.astype(BF16)
    r0 = la2 - p0.astype(F32)
    p1 = r0.astype(BF16)
    p2 = (r0 - p1.astype(F32)).astype(BF16)
    b = jnp.concatenate(
        [jnp.dot(lt, jnp.concatenate([chunk(p0, ci), chunk(p1, ci), chunk(p2, ci)], axis=0),
                 preferred_element_type=F32) for ci in range(nchunk)], axis=0)

    t = lax.broadcasted_iota(jnp.int32, b.shape, 0) % c
    klane_h = lax.broadcasted_iota(jnp.int32, (c, kw), 1) // dk
    vlane_h = lax.broadcasted_iota(jnp.int32, (c, vw), 1) // dv

    att = None
    for l in range(nlev + 1):
        if l == 0:
            qt, kt = q.astype(BF16), k.astype(BF16)
        else:
            e = jnp.exp2(-jnp.abs(b - _block_reference(b, c >> l)))
            qt = (q * e).astype(BF16)
            kt = (k * e).astype(BF16)
        parts = []
        for ci in range(nchunk):
            ktc = chunk(kt, ci)
            kstack = jnp.concatenate(
                [jnp.where(klane_h == h, ktc, jnp.zeros_like(ktc)) for h in range(heads)], axis=0)
            parts.append(lax.dot_general(chunk(qt, ci), kstack, _NT, preferred_element_type=F32))
        a = parts[0] if nchunk == 1 else jnp.concatenate(parts, axis=0)
        att = jnp.where(lev == l, a, 0.0 if att is None else att)
        fill()
    att = att.astype(BF16)
    qe = (q * jnp.exp2(b)).astype(BF16)

    o_intra, upd, decay = [], [], []
    for ci in range(nchunk):
        rows = slice(ci * c, (ci + 1) * c)
        vc = v[rows]
        vbd = jnp.concatenate(
            [jnp.where(vlane_h == h, vc, jnp.zeros_like(vc)) for h in range(heads)], axis=0)
        o_intra.append(jnp.dot(att[rows], vbd, preferred_element_type=F32))
        b_last = b[ci * c + c - 1:ci * c + c, :]
        kd = k[rows] * jnp.exp2(b_last - b[rows])
        x = jnp.concatenate([kd, jnp.zeros((pad, kw), F32), jnp.broadcast_to(b_last, (LANES, kw))], axis=0)
        xt = x.T
        for h in range(heads):
            hk = slice(h * dk, (h + 1) * dk)
            vpad = jnp.concatenate([vc[:, h * dv:(h + 1) * dv], jnp.zeros((pad, dv), BF16)], axis=0)
            upd.append(jnp.dot(xt[hk, :LANES].astype(BF16), vpad, preferred_element_type=F32))
            decay.append(jnp.exp2(xt[hk, LANES:]))
        fill()

    zero = jnp.zeros((dk, dv), BF16)
    outs = []
    for ci in range(nchunk):
        sbd = jnp.concatenate(
            [jnp.concatenate([s[h].astype(BF16) if g == h else zero for g in range(heads)], axis=1)
             for h in range(heads)], axis=0)
        outs.append(o_intra[ci] + jnp.dot(chunk(qe, ci), sbd, preferred_element_type=F32))
        s = [decay[ci * heads + h] * s[h] + upd[ci * heads + h] for h in range(heads)]
    while fillers:
        fill()
    return outs, s


def _gla_kernel(c, nchunk, heads, nlev, lt_ref, lev_ref, q_ref, k_ref, v_ref, la_ref, s0_ref,
                o_ref, sout_ref, s_ref):
    ti = pl.program_id(1)

    @pl.when(ti == 0)
    def _():
        s_ref[...] = s0_ref[0]

    outs, s = _gla_block(c, nchunk, heads, nlev, lt_ref[...], lev_ref[...], q_ref[0], k_ref[0], v_ref[0],
                         la_ref[0], [s_ref[h] for h in range(heads)])
    for ci, o in enumerate(outs):
        o_ref[0, ci * c:(ci + 1) * c, :] = o
    for h in range(heads):
        s_ref[h] = s[h]

    @pl.when(ti == pl.num_programs(1) - 1)
    def _():
        sout_ref[0] = s_ref[...]


def _gla(q, k, v, la, s0, chunk):
    bsz, t, kw = q.shape
    vw = v.shape[-1]
    heads, dk, dv = s0.shape[1:]
    nchunk = min(4, t // chunk)
    tg = nchunk * chunk
    lt3, lev, nlev = _gla_tables(chunk, heads, nchunk)
    tok = lambda w: pl.BlockSpec((1, tg, w), lambda b, i: (b, i, 0))
    st = pl.BlockSpec((1, heads, dk, dv), lambda b, i: (b, 0, 0, 0))
    return pl.pallas_call(
        functools.partial(_gla_kernel, chunk, nchunk, heads, nlev),
        grid=(bsz, t // tg),
        in_specs=[pl.BlockSpec(lt3.shape, lambda b, i: (0, 0)), pl.BlockSpec(lev.shape, lambda b, i: (0, 0)),
                  tok(kw), tok(kw), tok(vw), tok(kw), st],
        out_specs=[tok(vw), st],
        out_shape=[jax.ShapeDtypeStruct((bsz, t, vw), F32), jax.ShapeDtypeStruct(s0.shape, F32)],
        scratch_shapes=[pltpu.VMEM((heads, dk, dv), F32)],
        compiler_params=_cparams(("arbitrary", "arbitrary")),
        name="gla",
    )(jnp.asarray(lt3, BF16), jnp.asarray(lev), q, k, v, la, s0)


def _head_rmsnorm(x, g, heads):
    outs = []
    for h in range(heads):
        blk = x[:, h * LANES:(h + 1) * LANES]
        ms = jnp.mean(blk * blk, axis=-1, keepdims=True)
        outs.append((blk * lax.rsqrt(ms + EPS)) * g)
    return jnp.concatenate(outs, axis=1)


def _merge_kernel(lam_init, final, da_heads, gla_heads, x_ref, oa_ref, ob_ref, dg_ref, gg_ref, mg_ref,
                  ga_ref, gb_ref, wa_ref, wb_ref, wo_ref, gf_ref, y_ref):
    d = x_ref.shape[-1]
    dg = dg_ref[...].astype(F32)
    gg = gg_ref[...].astype(F32)
    oa = (_head_rmsnorm(oa_ref[...], ga_ref[...], da_heads) * (1.0 - lam_init)) * (dg * _sigmoid(dg))
    ob = _head_rmsnorm(ob_ref[...], gb_ref[...], gla_heads) * (gg * _sigmoid(gg))
    ya = jnp.dot(oa.astype(BF16), wa_ref[...], preferred_element_type=F32)
    yb = jnp.dot(ob.astype(BF16), wb_ref[...], preferred_element_type=F32)
    mixed = _sigmoid(mg_ref[:, :d].astype(F32)) * ya + _sigmoid(mg_ref[:, d:].astype(F32)) * yb
    y = x_ref[...] + jnp.dot(mixed.astype(BF16), wo_ref[...], preferred_element_type=F32)
    if final:
        ms = jnp.mean(y * y, axis=-1, keepdims=True)
        y = (y * lax.rsqrt(ms + EPS)) * gf_ref[...]
    y_ref[...] = y


def _merge(x2, oa, ob, dg, gg, mg, da_g, gla_g, wa, wb, wo, gf, lam_init, final, da_heads, gla_heads, tm):
    tok, d = x2.shape
    row = lambda w: pl.BlockSpec((tm, w), lambda i: (i, 0))
    full = lambda a: pl.BlockSpec(a.shape, lambda i: (0, 0))
    da_g = da_g.reshape(1, -1)
    gla_g = gla_g.reshape(1, -1)
    gf = gf.reshape(1, d)
    return pl.pallas_call(
        functools.partial(_merge_kernel, lam_init, final, da_heads, gla_heads),
        grid=(tok // tm,),
        in_specs=[row(d), row(oa.shape[1]), row(ob.shape[1]), row(dg.shape[1]), row(gg.shape[1]), row(mg.shape[1]),
                  full(da_g), full(gla_g), full(wa), full(wb), full(wo), full(gf)],
        out_specs=row(d),
        out_shape=jax.ShapeDtypeStruct((tok, d), F32),
        compiler_params=_cparams(("arbitrary",)),
        name="merge",
    )(x2, oa, ob, dg, gg, mg, da_g, gla_g, wa, wb, wo, gf)


def _layout(da_heads, da_dk, gla_heads, gla_dk, gla_dv):
    da_qk = da_heads * 2 * da_dk
    da_w = da_heads * 2 * da_dk
    kwid = gla_heads * gla_dk
    vwid = gla_heads * gla_dv
    names = ["dq", "dk", "dv", "dg", "gq", "gk", "gv", "gg"]
    sizes = [da_qk, da_qk, da_w, da_w, kwid, kwid, vwid, vwid]
    segs, off = {}, 0
    for nm, sz in zip(names, sizes):
        segs[nm] = (off, off + sz)
        off += sz
    segs["da_scale"] = da_dk ** -0.5 * math.log2(math.e)
    segs["gla_scale"] = gla_dk ** -0.5
    return segs, off


def kernel(x_prompt, x_sample, cache_k, cache_v, state_gla, norm_in_g, w_in, w_alpha_up, b_alpha,
           lambda_q1, lambda_k1, lambda_q2, lambda_k2, da_norm_g, gla_norm_g,
           w_branch_a, w_branch_b, w_out, norm_final_g):
    depth = w_in.shape[0]
    bsz, seq, d = x_prompt.shape
    dbsz, dseq, _ = x_sample.shape
    past = cache_k.shape[2]
    da_heads, da_dk = cache_k.shape[3], cache_k.shape[5]
    gla_heads, gla_dk, gla_dv = state_gla.shape[2:]
    rank = w_alpha_up.shape[1]
    kw = gla_heads * gla_dk
    assert 2 * da_dk == LANES and cache_v.shape[4] == LANES and gla_dv == LANES
    segs, ga_lo = _layout(da_heads, da_dk, gla_heads, gla_dk, gla_dv)

    xp = x_prompt.reshape(bsz * seq, d)
    xs = x_sample.reshape(dbsz * dseq, d)
    tm_p = min(512, bsz * seq)
    tm_s = min(512, dbsz * dseq)
    outs = [[] for _ in range(6)]
    for l in range(depth):
        lam_init = 0.8 - 0.6 * math.exp(-0.3 * l)
        final = l == depth - 1
        w = w_in[l]
        w_gate = jnp.pad(w[:, ga_lo:ga_lo + rank], ((0, 0), (0, LANES - rank))).astype(BF16)
        wkt = w[:, segs["dk"][0]:segs["dk"][1]].T.astype(BF16)
        wa_pad = jnp.pad(w_alpha_up[l], ((0, LANES - rank), (0, 0))).astype(BF16)
        weights = (w[:, :ga_lo].astype(BF16), w[:, ga_lo + rank:].astype(BF16), w_gate, wkt, wa_pad)
        lamv = jnp.stack([lambda_q1[l], lambda_k1[l], lambda_q2[l], lambda_k2[l]])
        wa, wb, wo = w_branch_a[l].astype(BF16), w_branch_b[l].astype(BF16), w_out[l].astype(BF16)

        p = _proj(xp, norm_in_g[l], weights, b_alpha[l], segs, tm_p, da_heads, seq=seq,
                  gla_state=jnp.zeros((bsz, gla_heads, gla_dk, gla_dv), F32))
        sh = lambda a: a.reshape(bsz, seq, a.shape[-1])
        oa = _attn_prompt(sh(p["q"]), p["kb"], sh(p["vb"]), lamv, lam_init, da_heads)
        s_p = p["sout"]
        xp = _merge(xp, oa.reshape(bsz * seq, -1), p["ob"], p["dg"], p["gg"], p["mg"],
                    da_norm_g[l], gla_norm_g[l], wa, wb, wo, norm_final_g, lam_init, final,
                    da_heads, gla_heads, tm_p)
        outs[0].append(jnp.transpose(p["kf"].reshape(bsz, da_heads, 2, da_dk, seq), (0, 4, 1, 2, 3)))
        outs[1].append(p["vf"].reshape(bsz, seq, da_heads, 2 * da_dk))
        outs[2].append(s_p)

        p = _proj(xs, norm_in_g[l], weights, b_alpha[l], segs, tm_s, da_heads)
        sh = lambda a: a.reshape(dbsz, dseq, a.shape[-1])
        ckt = jnp.transpose(cache_k[l], (0, 2, 3, 4, 1)).reshape(dbsz, da_heads * 2 * da_dk, past)
        oa = _attn_sample(sh(p["q"]), ckt, cache_v[l].reshape(dbsz, past * da_heads, 2 * da_dk),
                          sh(p["kb"]), sh(p["vb"]), lamv, lam_init, da_heads)
        ob, s_s = _gla(sh(p["gq"]), sh(p["gk"]), sh(p["gv"]), sh(p["la"]), state_gla[l], dseq)
        xs = _merge(xs, oa.reshape(dbsz * dseq, -1), ob.reshape(dbsz * dseq, -1), p["dg"], p["gg"], p["mg"],
                    da_norm_g[l], gla_norm_g[l], wa, wb, wo, norm_final_g, lam_init, final,
                    da_heads, gla_heads, tm_s)
        outs[3].append(p["kf"].reshape(dbsz, dseq, da_heads, 2, da_dk))
        outs[4].append(p["vf"].reshape(dbsz, dseq, da_heads, 2 * da_dk))
        outs[5].append(s_s)

    return (xp.reshape(bsz, seq, d), xs.reshape(dbsz, dseq, d),
            jnp.stack(outs[0]), jnp.stack(outs[1]), jnp.stack(outs[2]),
            jnp.stack(outs[3]), jnp.stack(outs[4]), jnp.stack(outs[5]))
```

```python
import functools
import math

import numpy as np
import jax
import jax.numpy as jnp
from jax import lax
from jax.experimental import pallas as pl
from jax.experimental.pallas import tpu as pltpu

F32 = jnp.float32
BF16 = jnp.bfloat16

EPS = 1e-6
CHUNK = 64
GLA_TAU = 16.0
LANES = 128
VMEM_LIMIT_BYTES = 56 * 1024 * 1024

_NT = (((1,), (1,)), ((), ()))
_TN = (((0,), (0,)), ((), ()))


def _sigmoid(x):
    return 0.5 * jnp.tanh(0.5 * x) + 0.5


def _log_sigmoid(x):
    return jnp.minimum(x, 0.0) - jnp.log1p(jnp.exp(-jnp.abs(x)))


def _cparams(sem):
    return pltpu.CompilerParams(dimension_semantics=sem, vmem_limit_bytes=VMEM_LIMIT_BYTES)


def _proj_kernel(segs, heads, seq_tiles, gla, in_names, out_names, *refs):
    n_in, n_out = len(in_names), len(out_names)
    r = dict(zip(in_names + out_names, refs[:n_in + n_out]))
    x = r["x"][...]
    tm = x.shape[0]
    ms = jnp.mean(x * x, axis=-1, keepdims=True)
    xn = ((x * lax.rsqrt(ms + EPS)) * r["g"][...]).astype(BF16)

    def seg(name):
        lo, hi = segs[name]
        return jnp.dot(xn, r["w_main"][:, lo:hi], preferred_element_type=F32)

    def put_q():
        r["q"][...] = (seg("dq") * segs["da_scale"]).astype(BF16)

    def put_k():
        if seq_tiles is not None:
            kt = lax.dot_general(r["wkt"][...], xn, _NT, preferred_element_type=F32)
            r["kf"][0] = kt
            r["kb"][0] = kt.astype(BF16)
        else:
            k = seg("dk")
            r["kf"][...] = k
            r["kb"][...] = k.astype(BF16)

    def put_v():
        v = seg("dv")
        r["vb"][...] = v.astype(BF16)
        if seq_tiles is not None:
            for h in range(heads):
                r["vf"][pl.ds(h, tm, stride=heads), :] = v[:, h * LANES:(h + 1) * LANES]
        else:
            r["vf"][...] = v

    def put_gate(name):
        r[name][...] = seg(name).astype(BF16)

    def put_merge(c0, step):
        r["mg"][:, c0:c0 + step] = jnp.dot(xn, r["w_merge"][:, c0:c0 + step],
                                           preferred_element_type=F32).astype(BF16)

    step = 512
    rest = [put_q, put_k, put_v, functools.partial(put_gate, "dg"), functools.partial(put_gate, "gg")]
    rest += [functools.partial(put_merge, c0, step) for c0 in range(0, r["w_merge"].shape[1], step)]

    ga = jnp.dot(xn, r["w_gate"][...], preferred_element_type=F32).astype(BF16)
    z = jnp.dot(ga, r["wa"][...], preferred_element_type=F32) + r["ba"][...]
    la = _log_sigmoid(z) / GLA_TAU
    gq = seg("gq") * segs["gla_scale"]
    gk = seg("gk")
    gv = seg("gv").astype(BF16)
    if gla is None:
        r["gq"][...] = gq
        r["gk"][...] = gk
        r["gv"][...] = gv
        r["la"][...] = la
        for f in rest:
            f()
    else:
        c, nchunk, gheads, nlev = gla
        s_ref = refs[n_in + n_out]
        ti = pl.program_id(0) % seq_tiles

        @pl.when(ti == 0)
        def _():
            s_ref[...] = r["s0"][0]

        outs, s = _gla_block(c, nchunk, gheads, nlev, r["lt"][...], r["lev"][...], gq, gk, gv, la,
                             [s_ref[h] for h in range(gheads)], fillers=rest)
        for ci, o in enumerate(outs):
            r["ob"][ci * c:(ci + 1) * c, :] = o
        for h in range(gheads):
            s_ref[h] = s[h]

        @pl.when(ti == seq_tiles - 1)
        def _():
            r["sout"][0] = s_ref[...]


def _proj(x2, g_in, weights, b_alpha, segs, tm, heads, seq=None, gla_state=None):
    tok, d = x2.shape
    w_main, w_merge, w_gate, wkt, wa_pad = weights
    widths = {k: v[1] - v[0] for k, v in segs.items() if isinstance(v, tuple)}
    widths["mg"] = w_merge.shape[1]
    kw = wa_pad.shape[1]
    seq_tiles = None if seq is None else seq // tm

    def row_spec(w):
        return pl.BlockSpec((tm, w), lambda i: (i, 0))

    def full_spec(shape):
        return pl.BlockSpec(shape, lambda i: (0,) * len(shape))

    def const_spec(shape):
        return pl.BlockSpec(shape, lambda i: (0,) * len(shape), pipeline_mode=pl.Buffered(1))

    ins = [("x", x2, row_spec(d)), ("g", g_in.reshape(1, d), full_spec((1, d))),
           ("w_main", w_main, const_spec(w_main.shape)), ("w_merge", w_merge, const_spec(w_merge.shape)),
           ("w_gate", w_gate, const_spec(w_gate.shape)), ("wkt", wkt, const_spec(wkt.shape)),
           ("wa", wa_pad, full_spec(wa_pad.shape)), ("ba", b_alpha.reshape(1, kw), full_spec((1, kw)))]
    out_defs = [
        ("q", widths["dq"], BF16), ("kf", widths["dk"], F32), ("kb", widths["dk"], BF16),
        ("vf", widths["dv"], F32), ("vb", widths["dv"], BF16), ("dg", widths["dg"], BF16),
        ("gg", widths["gg"], BF16), ("mg", widths["mg"], BF16),
    ]
    gla, scratch = None, []
    if gla_state is None:
        out_defs += [("gq", widths["gq"], F32), ("gk", widths["gk"], F32), ("gv", widths["gv"], BF16),
                     ("la", kw, F32)]
    else:
        out_defs += [("ob", widths["gv"], F32)]
    out_specs = {nm: row_spec(w) for nm, w, _ in out_defs}
    out_shape = {nm: jax.ShapeDtypeStruct((tok, w), dt) for nm, w, dt in out_defs}
    names = [nm for nm, _, _ in out_defs]
    if seq is not None:
        nt = seq_tiles
        for nm, dt in (("kf", F32), ("kb", BF16)):
            out_specs[nm] = pl.BlockSpec((1, widths["dk"], tm), lambda i: (i // nt, 0, i % nt))
            out_shape[nm] = jax.ShapeDtypeStruct((tok // seq, widths["dk"], seq), dt)
        out_specs["vf"] = pl.BlockSpec((tm * heads, LANES), lambda i: (i, 0))
        out_shape["vf"] = jax.ShapeDtypeStruct((tok * heads, LANES), F32)
    if gla_state is not None:
        gheads, dk, dv = gla_state.shape[1:]
        nchunk = tm // CHUNK
        lt3, lev, nlev = _gla_tables(CHUNK, gheads, nchunk)
        gla = (CHUNK, nchunk, gheads, nlev)
        st = pl.BlockSpec((1, gheads, dk, dv), lambda i: (i // nt, 0, 0, 0))
        ins += [("lt", jnp.asarray(lt3, BF16), full_spec(lt3.shape)), ("lev", jnp.asarray(lev), full_spec(lev.shape)),
                ("s0", gla_state, st)]
        names.append("sout")
        out_specs["sout"] = st
        out_shape["sout"] = jax.ShapeDtypeStruct(gla_state.shape, F32)
        scratch = [pltpu.VMEM((gheads, dk, dv), F32)]
    in_names = [nm for nm, _, _ in ins]
    outs = pl.pallas_call(
        functools.partial(_proj_kernel, segs, heads, seq_tiles, gla, in_names, names),
        grid=(tok // tm,),
        in_specs=[sp for _, _, sp in ins],
        out_specs=[out_specs[nm] for nm in names],
        out_shape=[out_shape[nm] for nm in names],
        scratch_shapes=scratch,
        compiler_params=_cparams(("arbitrary",)),
        name="proj",
    )(*[a for _, a, _ in ins])
    return dict(zip(names, outs))


def _diff_lambda(lamv, lam_init):
    a = jnp.sum(lamv[0:1, :] * lamv[1:2, :], axis=1, keepdims=True)
    b = jnp.sum(lamv[2:3, :] * lamv[3:4, :], axis=1, keepdims=True)
    return jnp.exp(a) - jnp.exp(b) + lam_init


def _stack_maps(q):
    lane = lax.broadcasted_iota(jnp.int32, q.shape, 1)
    half = q.shape[1] // 2
    zero = jnp.zeros_like(q)
    return jnp.concatenate([jnp.where(lane < half, q, zero), jnp.where(lane >= half, q, zero)], axis=0)


def _attn_prompt_kernel(lam_init, tq, ra, rb, lamv_ref, q_ref, kt_ref, v_ref, o_ref, *scratch):
    ones = jnp.ones((tq, LANES), BF16)
    nset = len(scratch) // 2
    sets = [dict(zip(("qx", "acc", "s0", "s1", "m0", "m1", "a0", "a1"), scratch[n * nset:(n + 1) * nset]))
            for n in range(2)]
    nq = q_ref.shape[1] // tq
    assert nq % 2 == 0

    def stage_a(bs, j, slot, first):
        sbuf, mbuf, abuf = (bs["s0"], bs["s1"]), (bs["m0"], bs["m1"]), (bs["a0"], bs["a1"])
        start = pl.multiple_of(j * tq, tq)
        kt = kt_ref[0, :, pl.ds(start, tq)]
        if first:
            for half in range(2):
                rows = slice(half * tq, (half + 1) * tq)
                s = jnp.dot(bs["qx"][rows, :], kt, preferred_element_type=F32)
                qc = lax.broadcasted_iota(jnp.int32, s.shape, 0) // CHUNK
                kc = lax.broadcasted_iota(jnp.int32, s.shape, 1) // CHUNK
                sbuf[slot][rows, :] = jnp.where(kc <= qc, s, -jnp.inf)
        else:
            sbuf[slot][...] = jnp.dot(bs["qx"][...], kt, preferred_element_type=F32)
        for r in range(2 * tq // ra):
            rows = slice(r * ra, (r + 1) * ra)
            m_cur = jnp.max(sbuf[slot][rows, :], axis=1, keepdims=True)
            if first:
                m_new = jnp.broadcast_to(m_cur, (ra, LANES))
                abuf[slot][rows, :] = jnp.zeros((ra, LANES), F32)
            else:
                m_prev = mbuf[1 - slot][rows, :]
                m_new = jnp.maximum(m_prev, m_cur)
                abuf[slot][rows, :] = jnp.exp2(m_prev - m_new)
            mbuf[slot][rows, :] = m_new

    def stage_b(bs, j, slot):
        sbuf, mbuf, abuf = (bs["s0"], bs["s1"]), (bs["m0"], bs["m1"]), (bs["a0"], bs["a1"])
        start = pl.multiple_of(j * tq, tq)
        vx = jnp.concatenate([v_ref[0, pl.ds(start, tq), :], ones], axis=1)
        for r in range(2 * tq // rb):
            rows = slice(r * rb, (r + 1) * rb)
            p = jnp.exp2(sbuf[slot][rows, :] - jnp.tile(mbuf[slot][rows, :], (1, tq // LANES)))
            pv = jnp.dot(p.astype(BF16), vx, preferred_element_type=F32)
            bs["acc"][rows, :] = bs["acc"][rows, :] * jnp.tile(abuf[slot][rows, :], (1, 2)) + pv

    def fill(bs, i):
        qrows = pl.ds(pl.multiple_of(i * tq, tq), tq)
        bs["qx"][...] = _stack_maps(q_ref[0, qrows, :])
        stage_a(bs, i, 0, True)

    def steady(bs, i):
        def pair(u, c):
            stage_a(bs, 2 * u, 1, False)
            stage_b(bs, jnp.where(u == 0, i, 2 * u - 1), 0)
            stage_a(bs, 2 * u + 1, 0, False)
            stage_b(bs, 2 * u, 1)
            return c

        lax.fori_loop(0, i // 2, pair, 0)

    def drain(bs, i, odd):
        last = jnp.where(i < 2, i, 2 * (i // 2) - 1)
        if odd:
            stage_a(bs, i - 1, 1, False)
            stage_b(bs, last, 0)
            stage_b(bs, i - 1, 1)
        else:
            stage_b(bs, last, 0)
        acc = bs["acc"][...]
        o1 = acc[:tq, :LANES] / acc[:tq, LANES:]
        o2 = acc[tq:, :LANES] / acc[tq:, LANES:]
        qrows = pl.ds(pl.multiple_of(i * tq, tq), tq)
        o_ref[0, qrows, :] = o1 - _diff_lambda(lamv_ref[...], lam_init) * o2

    def two_tiles(a, carry):
        even, odd = 2 * a, 2 * a + 1
        steady(sets[0], even)
        drain(sets[0], even, False)
        fill(sets[1], odd)
        steady(sets[1], odd)
        drain(sets[1], odd, True)
        fill(sets[0], jnp.minimum(even + 2, nq - 2))
        return carry

    for bs in sets:
        bs["acc"][...] = jnp.zeros(bs["acc"].shape, F32)
    fill(sets[0], 0)
    lax.fori_loop(0, nq // 2, two_tiles, 0)


def _attn_prompt(q, kt, v, lamv, lam_init, heads):
    bsz, t, w = q.shape
    tq = min(512, t)
    ra = min(128, tq)
    rb = min(512, tq)
    rowbuf = pltpu.VMEM((2 * tq, LANES), F32)
    return pl.pallas_call(
        functools.partial(_attn_prompt_kernel, lam_init, tq, ra, rb),
        grid=(bsz, heads),
        in_specs=[pl.BlockSpec(lamv.shape, lambda b, h: (0, 0)),
                  pl.BlockSpec((1, t, LANES), lambda b, h: (b, 0, h)),
                  pl.BlockSpec((1, LANES, t), lambda b, h: (b, h, 0)),
                  pl.BlockSpec((1, t, LANES), lambda b, h: (b, 0, h))],
        out_specs=pl.BlockSpec((1, t, LANES), lambda b, h: (b, 0, h)),
        out_shape=jax.ShapeDtypeStruct((bsz, t, w), F32),
        scratch_shapes=2 * [pltpu.VMEM((2 * tq, LANES), BF16), pltpu.VMEM((2 * tq, 2 * LANES), F32),
                            pltpu.VMEM((2 * tq, tq), F32), pltpu.VMEM((2 * tq, tq), F32),
                            rowbuf, rowbuf, rowbuf, rowbuf],
        compiler_params=_cparams(("arbitrary", "arbitrary")),
        name="attn_prompt",
    )(lamv, q, kt, v)


def _attn_sample_kernel(lam_init, heads, lamv_ref, q_ref, ckt_ref, cv_ref, kn_ref, vn_ref, o_ref):
    tq = q_ref.shape[1]
    past = ckt_ref.shape[2]
    lam = _diff_lambda(lamv_ref[...], lam_init)
    for h in range(heads):
        sl = slice(h * LANES, (h + 1) * LANES)
        qx = _stack_maps(q_ref[0, :, sl])
        kc = ckt_ref[0, sl, :].astype(BF16)
        vc = cv_ref[0, pl.ds(h, past, stride=heads), :].astype(BF16)
        kn = kn_ref[0, :, sl]
        vn = vn_ref[0, :, sl]
        sc = jnp.dot(qx, kc, preferred_element_type=F32)
        sn = lax.dot_general(qx, kn, _NT, preferred_element_type=F32)
        m = jnp.maximum(jnp.max(sc, axis=1, keepdims=True), jnp.max(sn, axis=1, keepdims=True))
        ec = jnp.exp2(sc - m)
        en = jnp.exp2(sn - m)
        l = jnp.sum(ec, axis=1, keepdims=True) + jnp.sum(en, axis=1, keepdims=True)
        o = (jnp.dot(ec.astype(BF16), vc, preferred_element_type=F32)
             + jnp.dot(en.astype(BF16), vn, preferred_element_type=F32)) / l
        o_ref[0, :, sl] = o[:tq] - lam * o[tq:]


def _attn_sample(q, ckt, cv, kn, vn, lamv, lam_init, heads):
    bsz, tq, w = q.shape
    new_spec = pl.BlockSpec((1, tq, w), lambda b: (b, 0, 0))
    return pl.pallas_call(
        functools.partial(_attn_sample_kernel, lam_init, heads),
        grid=(bsz,),
        in_specs=[pl.BlockSpec(lamv.shape, lambda b: (0, 0)), new_spec,
                  pl.BlockSpec((1,) + ckt.shape[1:], lambda b: (b, 0, 0)),
                  pl.BlockSpec((1,) + cv.shape[1:], lambda b: (b, 0, 0)), new_spec, new_spec],
        out_specs=new_spec,
        out_shape=jax.ShapeDtypeStruct((bsz, tq, w), F32),
        compiler_params=_cparams(("arbitrary",)),
        name="attn_sample",
    )(lamv, q, ckt, cv, kn, vn)


def _gla_tables(c, heads, nchunk):
    nlev = int(math.log2(c))
    assert 1 << nlev == c and c % 8 == 0
    t = np.arange(c)[:, None]
    u = np.arange(c)[None, :]
    lev = np.where(t == u, 0, -1)
    for l in range(1, nlev + 1):
        n = c >> l
        same_block = (t // (2 * n)) == (u // (2 * n))
        lev = np.where(same_block & ((t % (2 * n)) >= n) & ((u % (2 * n)) < n), l, lev)
    lt = (u <= t).astype(np.float32)
    return np.concatenate([lt, lt, lt], axis=1), np.tile(lev, (nchunk, heads)).astype(np.int32), nlev


def _block_reference(b, n):
    rows, w = b.shape
    if n >= 8:
        parts = [jnp.broadcast_to(b[g + n - 1:g + n, :], (2 * n, w)) for g in range(0, rows, 2 * n)]
        return parts[0] if len(parts) == 1 else jnp.concatenate(parts, axis=0)
    b3 = b.reshape(rows // 8, 8, w)
    sub = lax.broadcasted_iota(jnp.int32, b3.shape, 1)
    pick = lambda j: jnp.broadcast_to(b3[:, j:j + 1, :], b3.shape)
    if n == 4:
        ref = pick(3)
    elif n == 2:
        ref = jnp.where(sub < 4, pick(1), pick(5))
    else:
        ref = jnp.where(sub < 2, pick(0), jnp.where(sub < 4, pick(2), jnp.where(sub < 6, pick(4), pick(6))))
    return ref.reshape(rows, w)


def _gla_block(c, nchunk, heads, nlev, lt, lev, q, k, v, la, s, fillers=()):
    fillers = list(fillers)
    fill = lambda: fillers.pop(0)() if fillers else None
    kw = q.shape[-1]
    vw = v.shape[-1]
    dk = kw // heads
    dv = vw // heads
    pad = LANES - c
    assert dv == LANES and pad >= 0
    chunk = lambda a, ci: a[ci * c:(ci + 1) * c]

    la2 = la * math.log2(math.e)
    p0 = la2.astype(BF16)
    r0 = la2 - p0.astype(F32)
    p1 = r0.astype(BF16)
    p2 = (r0 - p1.astype(F32)).astype(BF16)
    b = jnp.concatenate(
        [jnp.dot(lt, jnp.concatenate([chunk(p0, ci), chunk(p1, ci), chunk(p2, ci)], axis=0),
                 preferred_element_type=F32) for ci in range(nchunk)], axis=0)

    t = lax.broadcasted_iota(jnp.int32, b.shape, 0) % c
    klane_h = lax.broadcasted_iota(jnp.int32, (c, kw), 1) // dk
    vlane_h = lax.broadcasted_iota(jnp.int32, (c, vw), 1) // dv

    att = None
    for l in range(nlev + 1):
        if l == 0:
            qt, kt = q.astype(BF16), k.astype(BF16)
        else:
            e = jnp.exp2(-jnp.abs(b - _block_reference(b, c >> l)))
            qt = (q * e).astype(BF16)
            kt = (k * e).astype(BF16)
        parts = []
        for ci in range(nchunk):
            ktc = chunk(kt, ci)
            kstack = jnp.concatenate(
                [jnp.where(klane_h == h, ktc, jnp.zeros_like(ktc)) for h in range(heads)], axis=0)
            parts.append(lax.dot_general(chunk(qt, ci), kstack, _NT, preferred_element_type=F32))
        a = parts[0] if nchunk == 1 else jnp.concatenate(parts, axis=0)
        att = jnp.where(lev == l, a, 0.0 if att is None else att)
        fill()
    att = att.astype(BF16)
    qe = (q * jnp.exp2(b)).astype(BF16)

    o_intra, upd, decay = [], [], []
    for ci in range(nchunk):
        rows = slice(ci * c, (ci + 1) * c)
        vc = v[rows]
        vbd = jnp.concatenate(
            [jnp.where(vlane_h == h, vc, jnp.zeros_like(vc)) for h in range(heads)], axis=0)
        o_intra.append(jnp.dot(att[rows], vbd, preferred_element_type=F32))
        b_last = b[ci * c + c - 1:ci * c + c, :]
        kd = k[rows] * jnp.exp2(b_last - b[rows])
        x = jnp.concatenate([kd, jnp.zeros((pad, kw), F32), jnp.broadcast_to(b_last, (LANES, kw))], axis=0)
        xt = x.T
        for h in range(heads):
            hk = slice(h * dk, (h + 1) * dk)
            vpad = jnp.concatenate([vc[:, h * dv:(h + 1) * dv], jnp.zeros((pad, dv), BF16)], axis=0)
            upd.append(jnp.dot(xt[hk, :LANES].astype(BF16), vpad, preferred_element_type=F32))
            decay.append(jnp.exp2(xt[hk, LANES:]))
        fill()

    zero = jnp.zeros((dk, dv), BF16)
    outs = []
    for ci in range(nchunk):
        sbd = jnp.concatenate(
            [jnp.concatenate([s[h].astype(BF16) if g == h else zero for g in range(heads)], axis=1)
             for h in range(heads)], axis=0)
        outs.append(o_intra[ci] + jnp.dot(chunk(qe, ci), sbd, preferred_element_type=F32))
        s = [decay[ci * heads + h] * s[h] + upd[ci * heads + h] for h in range(heads)]
    while fillers:
        fill()
    return outs, s


def _gla_kernel(c, nchunk, heads, nlev, lt_ref, lev_ref, q_ref, k_ref, v_ref, la_ref, s0_ref,
                o_ref, sout_ref, s_ref):
    ti = pl.program_id(1)

    @pl.when(ti == 0)
    def _():
        s_ref[...] = s0_ref[0]

    outs, s = _gla_block(c, nchunk, heads, nlev, lt_ref[...], lev_ref[...], q_ref[0], k_ref[0], v_ref[0],
                         la_ref[0], [s_ref[h] for h in range(heads)])
    for ci, o in enumerate(outs):
        o_ref[0, ci * c:(ci + 1) * c, :] = o
    for h in range(heads):
        s_ref[h] = s[h]

    @pl.when(ti == pl.num_programs(1) - 1)
    def _():
        sout_ref[0] = s_ref[...]


def _gla(q, k, v, la, s0, chunk):
    bsz, t, kw = q.shape
    vw = v.shape[-1]
    heads, dk, dv = s0.shape[1:]
    nchunk = min(4, t // chunk)
    tg = nchunk * chunk
    lt3, lev, nlev = _gla_tables(chunk, heads, nchunk)
    tok = lambda w: pl.BlockSpec((1, tg, w), lambda b, i: (b, i, 0))
    st = pl.BlockSpec((1, heads, dk, dv), lambda b, i: (b, 0, 0, 0))
    return pl.pallas_call(
        functools.partial(_gla_kernel, chunk, nchunk, heads, nlev),
        grid=(bsz, t // tg),
        in_specs=[pl.BlockSpec(lt3.shape, lambda b, i: (0, 0)), pl.BlockSpec(lev.shape, lambda b, i: (0, 0)),
                  tok(kw), tok(kw), tok(vw), tok(kw), st],
        out_specs=[tok(vw), st],
        out_shape=[jax.ShapeDtypeStruct((bsz, t, vw), F32), jax.ShapeDtypeStruct(s0.shape, F32)],
        scratch_shapes=[pltpu.VMEM((heads, dk, dv), F32)],
        compiler_params=_cparams(("arbitrary", "arbitrary")),
        name="gla",
    )(jnp.asarray(lt3, BF16), jnp.asarray(lev), q, k, v, la, s0)


def _head_rmsnorm(x, g, heads):
    outs = []
    for h in range(heads):
        blk = x[:, h * LANES:(h + 1) * LANES]
        ms = jnp.mean(blk * blk, axis=-1, keepdims=True)
        outs.append((blk * lax.rsqrt(ms + EPS)) * g)
    return jnp.concatenate(outs, axis=1)


def _merge_kernel(lam_init, final, da_heads, gla_heads, x_ref, oa_ref, ob_ref, dg_ref, gg_ref, mg_ref,
                  ga_ref, gb_ref, wa_ref, wb_ref, wo_ref, gf_ref, y_ref):
    d = x_ref.shape[-1]
    dg = dg_ref[...].astype(F32)
    gg = gg_ref[...].astype(F32)
    oa = (_head_rmsnorm(oa_ref[...], ga_ref[...], da_heads) * (1.0 - lam_init)) * (dg * _sigmoid(dg))
    ob = _head_rmsnorm(ob_ref[...], gb_ref[...], gla_heads) * (gg * _sigmoid(gg))
    ya = jnp.dot(oa.astype(BF16), wa_ref[...], preferred_element_type=F32)
    yb = jnp.dot(ob.astype(BF16), wb_ref[...], preferred_element_type=F32)
    mixed = _sigmoid(mg_ref[:, :d].astype(F32)) * ya + _sigmoid(mg_ref[:, d:].astype(F32)) * yb
    y = x_ref[...] + jnp.dot(mixed.astype(BF16), wo_ref[...], preferred_element_type=F32)
    if final:
        ms = jnp.mean(y * y, axis=-1, keepdims=True)
        y = (y * lax.rsqrt(ms + EPS)) * gf_ref[...]
    y_ref[...] = y


def _merge(x2, oa, ob, dg, gg, mg, da_g, gla_g, wa, wb, wo, gf, lam_init, final, da_heads, gla_heads, tm):
    tok, d = x2.shape
    row = lambda w: pl.BlockSpec((tm, w), lambda i: (i, 0))
    full = lambda a: pl.BlockSpec(a.shape, lambda i: (0, 0))
    da_g = da_g.reshape(1, -1)
    gla_g = gla_g.reshape(1, -1)
    gf = gf.reshape(1, d)
    return pl.pallas_call(
        functools.partial(_merge_kernel, lam_init, final, da_heads, gla_heads),
        grid=(tok // tm,),
        in_specs=[row(d), row(oa.shape[1]), row(ob.shape[1]), row(dg.shape[1]), row(gg.shape[1]), row(mg.shape[1]),
                  full(da_g), full(gla_g), full(wa), full(wb), full(wo), full(gf)],
        out_specs=row(d),
        out_shape=jax.ShapeDtypeStruct((tok, d), F32),
        compiler_params=_cparams(("arbitrary",)),
        name="merge",
    )(x2, oa, ob, dg, gg, mg, da_g, gla_g, wa, wb, wo, gf)


def _layout(da_heads, da_dk, gla_heads, gla_dk, gla_dv):
    da_qk = da_heads * 2 * da_dk
    da_w = da_heads * 2 * da_dk
    kwid = gla_heads * gla_dk
    vwid = gla_heads * gla_dv
    names = ["dq", "dk", "dv", "dg", "gq", "gk", "gv", "gg"]
    sizes = [da_qk, da_qk, da_w, da_w, kwid, kwid, vwid, vwid]
    segs, off = {}, 0
    for nm, sz in zip(names, sizes):
        segs[nm] = (off, off + sz)
        off += sz
    segs["da_scale"] = da_dk ** -0.5 * math.log2(math.e)
    segs["gla_scale"] = gla_dk ** -0.5
    return segs, off


def kernel(x_prompt, x_sample, cache_k, cache_v, state_gla, norm_in_g, w_in, w_alpha_up, b_alpha,
           lambda_q1, lambda_k1, lambda_q2, lambda_k2, da_norm_g, gla_norm_g,
           w_branch_a, w_branch_b, w_out, norm_final_g):
    depth = w_in.shape[0]
    bsz, seq, d = x_prompt.shape
    dbsz, dseq, _ = x_sample.shape
    past = cache_k.shape[2]
    da_heads, da_dk = cache_k.shape[3], cache_k.shape[5]
    gla_heads, gla_dk, gla_dv = state_gla.shape[2:]
    rank = w_alpha_up.shape[1]
    kw = gla_heads * gla_dk
    assert 2 * da_dk == LANES and cache_v.shape[4] == LANES and gla_dv == LANES
    segs, ga_lo = _layout(da_heads, da_dk, gla_heads, gla_dk, gla_dv)

    xp = x_prompt.reshape(bsz * seq, d)
    xs = x_sample.reshape(dbsz * dseq, d)
    tm_p = min(512, bsz * seq)
    tm_s = min(512, dbsz * dseq)
    outs = [[] for _ in range(6)]
    for l in range(depth):
        lam_init = 0.8 - 0.6 * math.exp(-0.3 * l)
        final = l == depth - 1
        w = w_in[l]
        w_gate = jnp.pad(w[:, ga_lo:ga_lo + rank], ((0, 0), (0, LANES - rank))).astype(BF16)
        wkt = w[:, segs["dk"][0]:segs["dk"][1]].T.astype(BF16)
        wa_pad = jnp.pad(w_alpha_up[l], ((0, LANES - rank), (0, 0))).astype(BF16)
        weights = (w[:, :ga_lo].astype(BF16), w[:, ga_lo + rank:].astype(BF16), w_gate, wkt, wa_pad)
        lamv = jnp.stack([lambda_q1[l], lambda_k1[l], lambda_q2[l], lambda_k2[l]])
        wa, wb, wo = w_branch_a[l].astype(BF16), w_branch_b[l].astype(BF16), w_out[l].astype(BF16)

        p = _proj(xp, norm_in_g[l], weights, b_alpha[l], segs, tm_p, da_heads, seq=seq,
                  gla_state=jnp.zeros((bsz, gla_heads, gla_dk, gla_dv), F32))
        sh = lambda a: a.reshape(bsz, seq, a.shape[-1])
        oa = _attn_prompt(sh(p["q"]), p["kb"], sh(p["vb"]), lamv, lam_init, da_heads)
        s_p = p["sout"]
        xp = _merge(xp, oa.reshape(bsz * seq, -1), p["ob"], p["dg"], p["gg"], p["mg"],
                    da_norm_g[l], gla_norm_g[l], wa, wb, wo, norm_final_g, lam_init, final,
                    da_heads, gla_heads, tm_p)
        outs[0].append(jnp.transpose(p["kf"].reshape(bsz, da_heads, 2, da_dk, seq), (0, 4, 1, 2, 3)))
        outs[1].append(p["vf"].reshape(bsz, seq, da_heads, 2 * da_dk))
        outs[2].append(s_p)

        p = _proj(xs, norm_in_g[l], weights, b_alpha[l], segs, tm_s, da_heads)
        sh = lambda a: a.reshape(dbsz, dseq, a.shape[-1])
        ckt = jnp.transpose(cache_k[l], (0, 2, 3, 4, 1)).reshape(dbsz, da_heads * 2 * da_dk, past)
        oa = _attn_sample(sh(p["q"]), ckt, cache_v[l].reshape(dbsz, past * da_heads, 2 * da_dk),
                          sh(p["kb"]), sh(p["vb"]), lamv, lam_init, da_heads)
        ob, s_s = _gla(sh(p["gq"]), sh(p["gk"]), sh(p["gv"]), sh(p["la"]), state_gla[l], dseq)
        xs = _merge(xs, oa.reshape(dbsz * dseq, -1), ob.reshape(dbsz * dseq, -1), p["dg"], p["gg"], p["mg"],
                    da_norm_g[l], gla_norm_g[l], wa, wb, wo, norm_final_g, lam_init, final,
                    da_heads, gla_heads, tm_s)
        outs[3].append(p["kf"].reshape(dbsz, dseq, da_heads, 2, da_dk))
        outs[4].append(p["vf"].reshape(dbsz, dseq, da_heads, 2 * da_dk))
        outs[5].append(s_s)

    return (xp.reshape(bsz, seq, d), xs.reshape(dbsz, dseq, d),
            jnp.stack(outs[0]), jnp.stack(outs[1]), jnp.stack(outs[2]),
            jnp.stack(outs[3]), jnp.stack(outs[4]), jnp.stack(outs[5]))
```

```python
import functools
import math

import numpy as np
import jax
import jax.numpy as jnp
from jax import lax
from jax.experimental import pallas as pl
from jax.experimental.pallas import tpu as pltpu

F32 = jnp.float32
BF16 = jnp.bfloat16

EPS = 1e-6
CHUNK = 64
GLA_TAU = 16.0
LANES = 128
VMEM_LIMIT_BYTES = 56 * 1024 * 1024

_NT = (((1,), (1,)), ((), ()))
_TN = (((0,), (0,)), ((), ()))


def _sigmoid(x):
    return 0.5 * jnp.tanh(0.5 * x) + 0.5


def _log_sigmoid(x):
    return jnp.minimum(x, 0.0) - jnp.log1p(jnp.exp(-jnp.abs(x)))


def _cparams(sem):
    return pltpu.CompilerParams(dimension_semantics=sem, vmem_limit_bytes=VMEM_LIMIT_BYTES)


def _proj_kernel(segs, heads, seq_tiles, gla, in_names, out_names, *refs):
    n_in, n_out = len(in_names), len(out_names)
    r = dict(zip(in_names + out_names, refs[:n_in + n_out]))
    x = r["x"][...]
    tm = x.shape[0]
    ms = jnp.mean(x * x, axis=-1, keepdims=True)
    xn = ((x * lax.rsqrt(ms + EPS)) * r["g"][...]).astype(BF16)

    def seg(name):
        lo, hi = segs[name]
        return jnp.dot(xn, r["w_main"][:, lo:hi], preferred_element_type=F32)

    def put_q():
        r["q"][...] = (seg("dq") * segs["da_scale"]).astype(BF16)

    def put_k():
        if seq_tiles is not None:
            kt = lax.dot_general(r["wkt"][...], xn, _NT, preferred_element_type=F32)
            r["kf"][0] = kt
            r["kb"][0] = kt.astype(BF16)
        else:
            k = seg("dk")
            r["kf"][...] = k
            r["kb"][...] = k.astype(BF16)

    def put_v():
        v = seg("dv")
        r["vb"][...] = v.astype(BF16)
        if seq_tiles is not None:
            for h in range(heads):
                r["vf"][pl.ds(h, tm, stride=heads), :] = v[:, h * LANES:(h + 1) * LANES]
        else:
            r["vf"][...] = v

    def put_gate(name):
        r[name][...] = seg(name).astype(BF16)

    def put_merge(c0, step):
        r["mg"][:, c0:c0 + step] = jnp.dot(xn, r["w_merge"][:, c0:c0 + step],
                                           preferred_element_type=F32).astype(BF16)

    step = 512
    rest = [put_q, put_k, put_v, functools.partial(put_gate, "dg"), functools.partial(put_gate, "gg")]
    rest += [functools.partial(put_merge, c0, step) for c0 in range(0, r["w_merge"].shape[1], step)]

    ga = jnp.dot(xn, r["w_gate"][...], preferred_element_type=F32).astype(BF16)
    z = jnp.dot(ga, r["wa"][...], preferred_element_type=F32) + r["ba"][...]
    la = _log_sigmoid(z) / GLA_TAU
    gq = seg("gq") * segs["gla_scale"]
    gk = seg("gk")
    gv = seg("gv").astype(BF16)
    if gla is None:
        r["gq"][...] = gq
        r["gk"][...] = gk
        r["gv"][...] = gv
        r["la"][...] = la
        for f in rest:
            f()
    else:
        c, nchunk, gheads, nlev = gla
        s_ref = refs[n_in + n_out]
        ti = pl.program_id(0) % seq_tiles

        @pl.when(ti == 0)
        def _():
            s_ref[...] = r["s0"][0]

        outs, s = _gla_block(c, nchunk, gheads, nlev, r["lt"][...], r["lev"][...], gq, gk, gv, la,
                             [s_ref[h] for h in range(gheads)], fillers=rest)
        for ci, o in enumerate(outs):
            r["ob"][ci * c:(ci + 1) * c, :] = o
        for h in range(gheads):
            s_ref[h] = s[h]

        @pl.when(ti == seq_tiles - 1)
        def _():
            r["sout"][0] = s_ref[...]


def _proj(x2, g_in, weights, b_alpha, segs, tm, heads, seq=None, gla_state=None):
    tok, d = x2.shape
    w_main, w_merge, w_gate, wkt, wa_pad = weights
    widths = {k: v[1] - v[0] for k, v in segs.items() if isinstance(v, tuple)}
    widths["mg"] = w_merge.shape[1]
    kw = wa_pad.shape[1]
    seq_tiles = None if seq is None else seq // tm

    def row_spec(w):
        return pl.BlockSpec((tm, w), lambda i: (i, 0))

    def full_spec(shape):
        return pl.BlockSpec(shape, lambda i: (0,) * len(shape))

    def const_spec(shape):
        return pl.BlockSpec(shape, lambda i: (0,) * len(shape), pipeline_mode=pl.Buffered(1))

    ins = [("x", x2, row_spec(d)), ("g", g_in.reshape(1, d), full_spec((1, d))),
           ("w_main", w_main, const_spec(w_main.shape)), ("w_merge", w_merge, const_spec(w_merge.shape)),
           ("w_gate", w_gate, const_spec(w_gate.shape)), ("wkt", wkt, const_spec(wkt.shape)),
           ("wa", wa_pad, full_spec(wa_pad.shape)), ("ba", b_alpha.reshape(1, kw), full_spec((1, kw)))]
    out_defs = [
        ("q", widths["dq"], BF16), ("kf", widths["dk"], F32), ("kb", widths["dk"], BF16),
        ("vf", widths["dv"], F32), ("vb", widths["dv"], BF16), ("dg", widths["dg"], BF16),
        ("gg", widths["gg"], BF16), ("mg", widths["mg"], BF16),
    ]
    gla, scratch = None, []
    if gla_state is None:
        out_defs += [("gq", widths["gq"], F32), ("gk", widths["gk"], F32), ("gv", widths["gv"], BF16),
                     ("la", kw, F32)]
    else:
        out_defs += [("ob", widths["gv"], F32)]
    out_specs = {nm: row_spec(w) for nm, w, _ in out_defs}
    out_shape = {nm: jax.ShapeDtypeStruct((tok, w), dt) for nm, w, dt in out_defs}
    names = [nm for nm, _, _ in out_defs]
    if seq is not None:
        nt = seq_tiles
        for nm, dt in (("kf", F32), ("kb", BF16)):
            out_specs[nm] = pl.BlockSpec((1, widths["dk"], tm), lambda i: (i // nt, 0, i % nt))
            out_shape[nm] = jax.ShapeDtypeStruct((tok // seq, widths["dk"], seq), dt)
        out_specs["vf"] = pl.BlockSpec((tm * heads, LANES), lambda i: (i, 0))
        out_shape["vf"] = jax.ShapeDtypeStruct((tok * heads, LANES), F32)
    if gla_state is not None:
        gheads, dk, dv = gla_state.shape[1:]
        nchunk = tm // CHUNK
        lt3, lev, nlev = _gla_tables(CHUNK, gheads, nchunk)
        gla = (CHUNK, nchunk, gheads, nlev)
        st = pl.BlockSpec((1, gheads, dk, dv), lambda i: (i // nt, 0, 0, 0))
        ins += [("lt", jnp.asarray(lt3, BF16), full_spec(lt3.shape)), ("lev", jnp.asarray(lev), full_spec(lev.shape)),
                ("s0", gla_state, st)]
        names.append("sout")
        out_specs["sout"] = st
        out_shape["sout"] = jax.ShapeDtypeStruct(gla_state.shape, F32)
        scratch = [pltpu.VMEM((gheads, dk, dv), F32)]
    in_names = [nm for nm, _, _ in ins]
    outs = pl.pallas_call(
        functools.partial(_proj_kernel, segs, heads, seq_tiles, gla, in_names, names),
        grid=(tok // tm,),
        in_specs=[sp for _, _, sp in ins],
        out_specs=[out_specs[nm] for nm in names],
        out_shape=[out_shape[nm] for nm in names],
        scratch_shapes=scratch,
        compiler_params=_cparams(("arbitrary",)),
        name="proj",
    )(*[a for _, a, _ in ins])
    return dict(zip(names, outs))


def _diff_lambda(lamv, lam_init):
    a = jnp.sum(lamv[0:1, :] * lamv[1:2, :], axis=1, keepdims=True)
    b = jnp.sum(lamv[2:3, :] * lamv[3:4, :], axis=1, keepdims=True)
    return jnp.exp(a) - jnp.exp(b) + lam_init


def _stack_maps(q):
    lane = lax.broadcasted_iota(jnp.int32, q.shape, 1)
    half = q.shape[1] // 2
    zero = jnp.zeros_like(q)
    return jnp.concatenate([jnp.where(lane < half, q, zero), jnp.where(lane >= half, q, zero)], axis=0)


def _attn_prompt_kernel(lam_init, tq, ra, rb, lamv_ref, q_ref, kt_ref, v_ref, o_ref, *scratch):
    ones = jnp.ones((tq, LANES), BF16)
    bs0 = dict(zip(("qx", "acc", "s0", "s1", "m0", "m1", "a0", "a1"), scratch))
    nq = q_ref.shape[1] // tq

    def stage_a(bs, j, slot, first):
        sbuf, mbuf, abuf = (bs["s0"], bs["s1"]), (bs["m0"], bs["m1"]), (bs["a0"], bs["a1"])
        start = pl.multiple_of(j * tq, tq)
        kt = kt_ref[0, :, pl.ds(start, tq)]
        if first:
            for half in range(2):
                rows = slice(half * tq, (half + 1) * tq)
                s = jnp.dot(bs["qx"][rows, :], kt, preferred_element_type=F32)
                qc = lax.broadcasted_iota(jnp.int32, s.shape, 0) // CHUNK
                kc = lax.broadcasted_iota(jnp.int32, s.shape, 1) // CHUNK
                sbuf[slot][rows, :] = jnp.where(kc <= qc, s, -jnp.inf)
        else:
            sbuf[slot][...] = jnp.dot(bs["qx"][...], kt, preferred_element_type=F32)
        for r in range(2 * tq // ra):
            rows = slice(r * ra, (r + 1) * ra)
            m_cur = jnp.max(sbuf[slot][rows, :], axis=1, keepdims=True)
            if first:
                m_new = jnp.broadcast_to(m_cur, (ra, LANES))
                abuf[slot][rows, :] = jnp.zeros((ra, LANES), F32)
            else:
                m_prev = mbuf[1 - slot][rows, :]
                m_new = jnp.maximum(m_prev, m_cur)
                abuf[slot][rows, :] = jnp.exp2(m_prev - m_new)
            mbuf[slot][rows, :] = m_new

    def stage_b(bs, j, slot):
        sbuf, mbuf, abuf = (bs["s0"], bs["s1"]), (bs["m0"], bs["m1"]), (bs["a0"], bs["a1"])
        start = pl.multiple_of(j * tq, tq)
        vx = jnp.concatenate([v_ref[0, pl.ds(start, tq), :], ones], axis=1)
        for r in range(2 * tq // rb):
            rows = slice(r * rb, (r + 1) * rb)
            p = jnp.exp2(sbuf[slot][rows, :] - jnp.tile(mbuf[slot][rows, :], (1, tq // LANES)))
            pv = jnp.dot(p.astype(BF16), vx, preferred_element_type=F32)
            bs["acc"][rows, :] = bs["acc"][rows, :] * jnp.tile(abuf[slot][rows, :], (1, 2)) + pv

    def fill(bs, i):
        qrows = pl.ds(pl.multiple_of(i * tq, tq), tq)
        bs["qx"][...] = _stack_maps(q_ref[0, qrows, :])
        stage_a(bs, i, 0, True)

    def steady(bs, i):
        def pair(u, c):
            stage_a(bs, 2 * u, 1, False)
            stage_b(bs, jnp.where(u == 0, i, 2 * u - 1), 0)
            stage_a(bs, 2 * u + 1, 0, False)
            stage_b(bs, 2 * u, 1)
            return c

        lax.fori_loop(0, i // 2, pair, 0)

    def drain(bs, i, odd):
        last = jnp.where(i < 2, i, 2 * (i // 2) - 1)
        if odd:
            stage_a(bs, i - 1, 1, False)
            stage_b(bs, last, 0)
            stage_b(bs, i - 1, 1)
        else:
            stage_b(bs, last, 0)
        acc = bs["acc"][...]
        o1 = acc[:tq, :LANES] / acc[:tq, LANES:]
        o2 = acc[tq:, :LANES] / acc[tq:, LANES:]
        qrows = pl.ds(pl.multiple_of(i * tq, tq), tq)
        o_ref[0, qrows, :] = o1 - _diff_lambda(lamv_ref[...], lam_init) * o2

    def q_tile(i, carry):
        fill(bs0, i)
        steady(bs0, i)

        @pl.when(i % 2 == 0)
        def _():
            drain(bs0, i, False)

        @pl.when(i % 2 == 1)
        def _():
            drain(bs0, i, True)

        return carry

    bs0["acc"][...] = jnp.zeros(bs0["acc"].shape, F32)
    lax.fori_loop(0, nq, q_tile, 0)


def _attn_prompt(q, kt, v, lamv, lam_init, heads):
    bsz, t, w = q.shape
    tq = min(1024, t)
    ra = min(128, tq)
    rb = min(512, tq)
    rowbuf = pltpu.VMEM((2 * tq, LANES), F32)
    return pl.pallas_call(
        functools.partial(_attn_prompt_kernel, lam_init, tq, ra, rb),
        grid=(bsz, heads),
        in_specs=[pl.BlockSpec(lamv.shape, lambda b, h: (0, 0)),
                  pl.BlockSpec((1, t, LANES), lambda b, h: (b, 0, h)),
                  pl.BlockSpec((1, LANES, t), lambda b, h: (b, h, 0)),
                  pl.BlockSpec((1, t, LANES), lambda b, h: (b, 0, h))],
        out_specs=pl.BlockSpec((1, t, LANES), lambda b, h: (b, 0, h)),
        out_shape=jax.ShapeDtypeStruct((bsz, t, w), F32),
        scratch_shapes=[pltpu.VMEM((2 * tq, LANES), BF16), pltpu.VMEM((2 * tq, 2 * LANES), F32),
                        pltpu.VMEM((2 * tq, tq), F32), pltpu.VMEM((2 * tq, tq), F32),
                        rowbuf, rowbuf, rowbuf, rowbuf],
        compiler_params=_cparams(("arbitrary", "arbitrary")),
        name="attn_prompt",
    )(lamv, q, kt, v)


def _attn_sample_kernel(lam_init, heads, lamv_ref, q_ref, ckt_ref, cv_ref, kn_ref, vn_ref, o_ref):
    tq = q_ref.shape[1]
    past = ckt_ref.shape[2]
    lam = _diff_lambda(lamv_ref[...], lam_init)
    for h in range(heads):
        sl = slice(h * LANES, (h + 1) * LANES)
        qx = _stack_maps(q_ref[0, :, sl])
        kc = ckt_ref[0, sl, :].astype(BF16)
        vc = cv_ref[0, pl.ds(h, past, stride=heads), :].astype(BF16)
        kn = kn_ref[0, :, sl]
        vn = vn_ref[0, :, sl]
        sc = jnp.dot(qx, kc, preferred_element_type=F32)
        sn = lax.dot_general(qx, kn, _NT, preferred_element_type=F32)
        m = jnp.maximum(jnp.max(sc, axis=1, keepdims=True), jnp.max(sn, axis=1, keepdims=True))
        ec = jnp.exp2(sc - m)
        en = jnp.exp2(sn - m)
        l = jnp.sum(ec, axis=1, keepdims=True) + jnp.sum(en, axis=1, keepdims=True)
        o = (jnp.dot(ec.astype(BF16), vc, preferred_element_type=F32)
             + jnp.dot(en.astype(BF16), vn, preferred_element_type=F32)) / l
        o_ref[0, :, sl] = o[:tq] - lam * o[tq:]


def _attn_sample(q, ckt, cv, kn, vn, lamv, lam_init, heads):
    bsz, tq, w = q.shape
    new_spec = pl.BlockSpec((1, tq, w), lambda b: (b, 0, 0))
    return pl.pallas_call(
        functools.partial(_attn_sample_kernel, lam_init, heads),
        grid=(bsz,),
        in_specs=[pl.BlockSpec(lamv.shape, lambda b: (0, 0)), new_spec,
                  pl.BlockSpec((1,) + ckt.shape[1:], lambda b: (b, 0, 0)),
                  pl.BlockSpec((1,) + cv.shape[1:], lambda b: (b, 0, 0)), new_spec, new_spec],
        out_specs=new_spec,
        out_shape=jax.ShapeDtypeStruct((bsz, tq, w), F32),
        compiler_params=_cparams(("arbitrary",)),
        name="attn_sample",
    )(lamv, q, ckt, cv, kn, vn)


def _gla_tables(c, heads, nchunk):
    nlev = int(math.log2(c))
    assert 1 << nlev == c and c % 8 == 0
    t = np.arange(c)[:, None]
    u = np.arange(c)[None, :]
    lev = np.where(t == u, 0, -1)
    for l in range(1, nlev + 1):
        n = c >> l
        same_block = (t // (2 * n)) == (u // (2 * n))
        lev = np.where(same_block & ((t % (2 * n)) >= n) & ((u % (2 * n)) < n), l, lev)
    lt = (u <= t).astype(np.float32)
    return np.concatenate([lt, lt, lt], axis=1), np.tile(lev, (nchunk, heads)).astype(np.int32), nlev


def _block_reference(b, n):
    rows, w = b.shape
    if n >= 8:
        parts = [jnp.broadcast_to(b[g + n - 1:g + n, :], (2 * n, w)) for g in range(0, rows, 2 * n)]
        return parts[0] if len(parts) == 1 else jnp.concatenate(parts, axis=0)
    b3 = b.reshape(rows // 8, 8, w)
    sub = lax.broadcasted_iota(jnp.int32, b3.shape, 1)
    pick = lambda j: jnp.broadcast_to(b3[:, j:j + 1, :], b3.shape)
    if n == 4:
        ref = pick(3)
    elif n == 2:
        ref = jnp.where(sub < 4, pick(1), pick(5))
    else:
        ref = jnp.where(sub < 2, pick(0), jnp.where(sub < 4, pick(2), jnp.where(sub < 6, pick(4), pick(6))))
    return ref.reshape(rows, w)


def _gla_block(c, nchunk, heads, nlev, lt, lev, q, k, v, la, s, fillers=()):
    fillers = list(fillers)
    fill = lambda: fillers.pop(0)() if fillers else None
    kw = q.shape[-1]
    vw = v.shape[-1]
    dk = kw // heads
    dv = vw // heads
    pad = LANES - c
    assert dv == LANES and pad >= 0
    chunk = lambda a, ci: a[ci * c:(ci + 1) * c]

    la2 = la * math.log2(math.e)
    p0 = la2.astype(BF16)
    r0 = la2 - p0.astype(F32)
    p1 = r0.astype(BF16)
    p2 = (r0 - p1.astype(F32)).astype(BF16)
    b = jnp.concatenate(
        [jnp.dot(lt, jnp.concatenate([chunk(p0, ci), chunk(p1, ci), chunk(p2, ci)], axis=0),
                 preferred_element_type=F32) for ci in range(nchunk)], axis=0)

    t = lax.broadcasted_iota(jnp.int32, b.shape, 0) % c
    klane_h = lax.broadcasted_iota(jnp.int32, (c, kw), 1) // dk
    vlane_h = lax.broadcasted_iota(jnp.int32, (c, vw), 1) // dv

    att = None
    for l in range(nlev + 1):
        if l == 0:
            qt, kt = q.astype(BF16), k.astype(BF16)
        else:
            e = jnp.exp2(-jnp.abs(b - _block_reference(b, c >> l)))
            qt = (q * e).astype(BF16)
            kt = (k * e).astype(BF16)
        parts = []
        for ci in range(nchunk):
            ktc = chunk(kt, ci)
            kstack = jnp.concatenate(
                [jnp.where(klane_h == h, ktc, jnp.zeros_like(ktc)) for h in range(heads)], axis=0)
            parts.append(lax.dot_general(chunk(qt, ci), kstack, _NT, preferred_element_type=F32))
        a = parts[0] if nchunk == 1 else jnp.concatenate(parts, axis=0)
        att = jnp.where(lev == l, a, 0.0 if att is None else att)
        fill()
    att = att.astype(BF16)
    qe = (q * jnp.exp2(b)).astype(BF16)

    o_intra, upd, decay = [], [], []
    for ci in range(nchunk):
        rows = slice(ci * c, (ci + 1) * c)
        vc = v[rows]
        vbd = jnp.concatenate(
            [jnp.where(vlane_h == h, vc, jnp.zeros_like(vc)) for h in range(heads)], axis=0)
        o_intra.append(jnp.dot(att[rows], vbd, preferred_element_type=F32))
        b_last = b[ci * c + c - 1:ci * c + c, :]
        kd = k[rows] * jnp.exp2(b_last - b[rows])
        x = jnp.concatenate([kd, jnp.zeros((pad, kw), F32), jnp.broadcast_to(b_last, (LANES, kw))], axis=0)
        xt = x.T
        for h in range(heads):
            hk = slice(h * dk, (h + 1) * dk)
            vpad = jnp.concatenate([vc[:, h * dv:(h + 1) * dv], jnp.zeros((pad, dv), BF16)], axis=0)
            upd.append(jnp.dot(xt[hk, :LANES].astype(BF16), vpad, preferred_element_type=F32))
            decay.append(jnp.exp2(xt[hk, LANES:]))
        fill()

    zero = jnp.zeros((dk, dv), BF16)
    outs = []
    for ci in range(nchunk):
        sbd = jnp.concatenate(
            [jnp.concatenate([s[h].astype(BF16) if g == h else zero for g in range(heads)], axis=1)
             for h in range(heads)], axis=0)
        outs.append(o_intra[ci] + jnp.dot(chunk(qe, ci), sbd, preferred_element_type=F32))
        s = [decay[ci * heads + h] * s[h] + upd[ci * heads + h] for h in range(heads)]
    while fillers:
        fill()
    return outs, s


def _gla_kernel(c, nchunk, heads, nlev, lt_ref, lev_ref, q_ref, k_ref, v_ref, la_ref, s0_ref,
                o_ref, sout_ref, s_ref):
    ti = pl.program_id(1)

    @pl.when(ti == 0)
    def _():
        s_ref[...] = s0_ref[0]

    outs, s = _gla_block(c, nchunk, heads, nlev, lt_ref[...], lev_ref[...], q_ref[0], k_ref[0], v_ref[0],
                         la_ref[0], [s_ref[h] for h in range(heads)])
    for ci, o in enumerate(outs):
        o_ref[0, ci * c:(ci + 1) * c, :] = o
    for h in range(heads):
        s_ref[h] = s[h]

    @pl.when(ti == pl.num_programs(1) - 1)
    def _():
        sout_ref[0] = s_ref[...]


def _gla(q, k, v, la, s0, chunk):
    bsz, t, kw = q.shape
    vw = v.shape[-1]
    heads, dk, dv = s0.shape[1:]
    nchunk = min(4, t // chunk)
    tg = nchunk * chunk
    lt3, lev, nlev = _gla_tables(chunk, heads, nchunk)
    tok = lambda w: pl.BlockSpec((1, tg, w), lambda b, i: (b, i, 0))
    st = pl.BlockSpec((1, heads, dk, dv), lambda b, i: (b, 0, 0, 0))
    return pl.pallas_call(
        functools.partial(_gla_kernel, chunk, nchunk, heads, nlev),
        grid=(bsz, t // tg),
        in_specs=[pl.BlockSpec(lt3.shape, lambda b, i: (0, 0)), pl.BlockSpec(lev.shape, lambda b, i: (0, 0)),
                  tok(kw), tok(kw), tok(vw), tok(kw), st],
        out_specs=[tok(vw), st],
        out_shape=[jax.ShapeDtypeStruct((bsz, t, vw), F32), jax.ShapeDtypeStruct(s0.shape, F32)],
        scratch_shapes=[pltpu.VMEM((heads, dk, dv), F32)],
        compiler_params=_cparams(("arbitrary", "arbitrary")),
        name="gla",
    )(jnp.asarray(lt3, BF16), jnp.asarray(lev), q, k, v, la, s0)


def _head_rmsnorm(x, g, heads):
    outs = []
    for h in range(heads):
        blk = x[:, h * LANES:(h + 1) * LANES]
        ms = jnp.mean(blk * blk, axis=-1, keepdims=True)
        outs.append((blk * lax.rsqrt(ms + EPS)) * g)
    return jnp.concatenate(outs, axis=1)


def _merge_kernel(lam_init, final, da_heads, gla_heads, x_ref, oa_ref, ob_ref, dg_ref, gg_ref, mg_ref,
                  ga_ref, gb_ref, wa_ref, wb_ref, wo_ref, gf_ref, y_ref):
    d = x_ref.shape[-1]
    dg = dg_ref[...].astype(F32)
    gg = gg_ref[...].astype(F32)
    oa = (_head_rmsnorm(oa_ref[...], ga_ref[...], da_heads) * (1.0 - lam_init)) * (dg * _sigmoid(dg))
    ob = _head_rmsnorm(ob_ref[...], gb_ref[...], gla_heads) * (gg * _sigmoid(gg))
    ya = jnp.dot(oa.astype(BF16), wa_ref[...], preferred_element_type=F32)
    yb = jnp.dot(ob.astype(BF16), wb_ref[...], preferred_element_type=F32)
    mixed = _sigmoid(mg_ref[:, :d].astype(F32)) * ya + _sigmoid(mg_ref[:, d:].astype(F32)) * yb
    y = x_ref[...] + jnp.dot(mixed.astype(BF16), wo_ref[...], preferred_element_type=F32)
    if final:
        ms = jnp.mean(y * y, axis=-1, keepdims=True)
        y = (y * lax.rsqrt(ms + EPS)) * gf_ref[...]
    y_ref[...] = y


def _merge(x2, oa, ob, dg, gg, mg, da_g, gla_g, wa, wb, wo, gf, lam_init, final, da_heads, gla_heads, tm):
    tok, d = x2.shape
    row = lambda w: pl.BlockSpec((tm, w), lambda i: (i, 0))
    full = lambda a: pl.BlockSpec(a.shape, lambda i: (0, 0))
    da_g = da_g.reshape(1, -1)
    gla_g = gla_g.reshape(1, -1)
    gf = gf.reshape(1, d)
    return pl.pallas_call(
        functools.partial(_merge_kernel, lam_init, final, da_heads, gla_heads),
        grid=(tok // tm,),
        in_specs=[row(d), row(oa.shape[1]), row(ob.shape[1]), row(dg.shape[1]), row(gg.shape[1]), row(mg.shape[1]),
                  full(da_g), full(gla_g), full(wa), full(wb), full(wo), full(gf)],
        out_specs=row(d),
        out_shape=jax.ShapeDtypeStruct((tok, d), F32),
        compiler_params=_cparams(("arbitrary",)),
        name="merge",
    )(x2, oa, ob, dg, gg, mg, da_g, gla_g, wa, wb, wo, gf)


def _layout(da_heads, da_dk, gla_heads, gla_dk, gla_dv):
    da_qk = da_heads * 2 * da_dk
    da_w = da_heads * 2 * da_dk
    kwid = gla_heads * gla_dk
    vwid = gla_heads * gla_dv
    names = ["dq", "dk", "dv", "dg", "gq", "gk", "gv", "gg"]
    sizes = [da_qk, da_qk, da_w, da_w, kwid, kwid, vwid, vwid]
    segs, off = {}, 0
    for nm, sz in zip(names, sizes):
        segs[nm] = (off, off + sz)
        off += sz
    segs["da_scale"] = da_dk ** -0.5 * math.log2(math.e)
    segs["gla_scale"] = gla_dk ** -0.5
    return segs, off


def kernel(x_prompt, x_sample, cache_k, cache_v, state_gla, norm_in_g, w_in, w_alpha_up, b_alpha,
           lambda_q1, lambda_k1, lambda_q2, lambda_k2, da_norm_g, gla_norm_g,
           w_branch_a, w_branch_b, w_out, norm_final_g):
    depth = w_in.shape[0]
    bsz, seq, d = x_prompt.shape
    dbsz, dseq, _ = x_sample.shape
    past = cache_k.shape[2]
    da_heads, da_dk = cache_k.shape[3], cache_k.shape[5]
    gla_heads, gla_dk, gla_dv = state_gla.shape[2:]
    rank = w_alpha_up.shape[1]
    kw = gla_heads * gla_dk
    assert 2 * da_dk == LANES and cache_v.shape[4] == LANES and gla_dv == LANES
    segs, ga_lo = _layout(da_heads, da_dk, gla_heads, gla_dk, gla_dv)

    xp = x_prompt.reshape(bsz * seq, d)
    xs = x_sample.reshape(dbsz * dseq, d)
    tm_p = min(512, bsz * seq)
    tm_s = min(512, dbsz * dseq)
    outs = [[] for _ in range(6)]
    for l in range(depth):
        lam_init = 0.8 - 0.6 * math.exp(-0.3 * l)
        final = l == depth - 1
        w = w_in[l]
        w_gate = jnp.pad(w[:, ga_lo:ga_lo + rank], ((0, 0), (0, LANES - rank))).astype(BF16)
        wkt = w[:, segs["dk"][0]:segs["dk"][1]].T.astype(BF16)
        wa_pad = jnp.pad(w_alpha_up[l], ((0, LANES - rank), (0, 0))).astype(BF16)
        weights = (w[:, :ga_lo].astype(BF16), w[:, ga_lo + rank:].astype(BF16), w_gate, wkt, wa_pad)
        lamv = jnp.stack([lambda_q1[l], lambda_k1[l], lambda_q2[l], lambda_k2[l]])
        wa, wb, wo = w_branch_a[l].astype(BF16), w_branch_b[l].astype(BF16), w_out[l].astype(BF16)

        p = _proj(xp, norm_in_g[l], weights, b_alpha[l], segs, tm_p, da_heads, seq=seq,
                  gla_state=jnp.zeros((bsz, gla_heads, gla_dk, gla_dv), F32))
        sh = lambda a: a.reshape(bsz, seq, a.shape[-1])
        oa = _attn_prompt(sh(p["q"]), p["kb"], sh(p["vb"]), lamv, lam_init, da_heads)
        s_p = p["sout"]
        xp = _merge(xp, oa.reshape(bsz * seq, -1), p["ob"], p["dg"], p["gg"], p["mg"],
                    da_norm_g[l], gla_norm_g[l], wa, wb, wo, norm_final_g, lam_init, final,
                    da_heads, gla_heads, tm_p)
        outs[0].append(jnp.transpose(p["kf"].reshape(bsz, da_heads, 2, da_dk, seq), (0, 4, 1, 2, 3)))
        outs[1].append(p["vf"].reshape(bsz, seq, da_heads, 2 * da_dk))
        outs[2].append(s_p)

        p = _proj(xs, norm_in_g[l], weights, b_alpha[l], segs, tm_s, da_heads)
        sh = lambda a: a.reshape(dbsz, dseq, a.shape[-1])
        ckt = jnp.transpose(cache_k[l], (0, 2, 3, 4, 1)).reshape(dbsz, da_heads * 2 * da_dk, past)
        oa = _attn_sample(sh(p["q"]), ckt, cache_v[l].reshape(dbsz, past * da_heads, 2 * da_dk),
                          sh(p["kb"]), sh(p["vb"]), lamv, lam_init, da_heads)
        ob, s_s = _gla(sh(p["gq"]), sh(p["gk"]), sh(p["gv"]), sh(p["la"]), state_gla[l], dseq)
        xs = _merge(xs, oa.reshape(dbsz * dseq, -1), ob.reshape(dbsz * dseq, -1), p["dg"], p["gg"], p["mg"],
                    da_norm_g[l], gla_norm_g[l], wa, wb, wo, norm_final_g, lam_init, final,
                    da_heads, gla_heads, tm_s)
        outs[3].append(p["kf"].reshape(dbsz, dseq, da_heads, 2, da_dk))
        outs[4].append(p["vf"].reshape(dbsz, dseq, da_heads, 2 * da_dk))
        outs[5].append(s_s)

    return (xp.reshape(bsz, seq, d), xs.reshape(dbsz, dseq, d),
            jnp.stack(outs[0]), jnp.stack(outs[1]), jnp.stack(outs[2]),
            jnp.stack(outs[3]), jnp.stack(outs[4]), jnp.stack(outs[5]))
```

```python
import functools
import math

import numpy as np
import jax
import jax.numpy as jnp
from jax import lax
from jax.experimental import pallas as pl
from jax.experimental.pallas import tpu as pltpu

F32 = jnp.float32
BF16 = jnp.bfloat16

EPS = 1e-6
CHUNK = 64
GLA_TAU = 16.0
LANES = 128
VMEM_LIMIT_BYTES = 56 * 1024 * 1024

_NT = (((1,), (1,)), ((), ()))
_TN = (((0,), (0,)), ((), ()))


def _log_sigmoid(x):
    return jnp.minimum(x, 0.0) - jnp.log1p(jnp.exp(-jnp.abs(x)))


def _cparams(sem):
    return pltpu.CompilerParams(dimension_semantics=sem, vmem_limit_bytes=VMEM_LIMIT_BYTES)


def _proj_kernel(segs, heads, seq_tiles, gla, in_names, out_names, *refs):
    n_in, n_out = len(in_names), len(out_names)
    r = dict(zip(in_names + out_names, refs[:n_in + n_out]))
    x = r["x"][...]
    tm = x.shape[0]
    ms = jnp.mean(x * x, axis=-1, keepdims=True)
    xn = ((x * lax.rsqrt(ms + EPS)) * r["g"][...]).astype(BF16)

    def seg(name):
        lo, hi = segs[name]
        return jnp.dot(xn, r["w_main"][:, lo:hi], preferred_element_type=F32)

    def put_q():
        r["q"][...] = (seg("dq") * segs["da_scale"]).astype(BF16)

    def put_k():
        if seq_tiles is not None:
            kt = lax.dot_general(r["wkt"][...], xn, _NT, preferred_element_type=F32)
            r["kf"][0] = kt
            r["kb"][0] = kt.astype(BF16)
        else:
            k = seg("dk")
            r["kf"][...] = k
            r["kb"][...] = k.astype(BF16)

    def put_v():
        v = seg("dv")
        r["vb"][...] = v.astype(BF16)
        if seq_tiles is not None:
            for h in range(heads):
                r["vf"][pl.ds(h, tm, stride=heads), :] = v[:, h * LANES:(h + 1) * LANES]
        else:
            r["vf"][...] = v

    def put_gate(name):
        r[name][...] = seg(name).astype(BF16)

    def put_merge(c0, step):
        r["mg"][:, c0:c0 + step] = jnp.dot(xn, r["w_merge"][:, c0:c0 + step],
                                           preferred_element_type=F32).astype(BF16)

    step = 512
    rest = [put_q, put_k, put_v, functools.partial(put_gate, "dg"), functools.partial(put_gate, "gg")]
    rest += [functools.partial(put_merge, c0, step) for c0 in range(0, r["w_merge"].shape[1], step)]

    ga = jnp.dot(xn, r["w_gate"][...], preferred_element_type=F32).astype(BF16)
    z = jnp.dot(ga, r["wa"][...], preferred_element_type=F32) + r["ba"][...]
    la = _log_sigmoid(z) / GLA_TAU
    gq = seg("gq") * segs["gla_scale"]
    gk = seg("gk")
    gv = seg("gv").astype(BF16)
    if gla is None:
        r["gq"][...] = gq
        r["gk"][...] = gk
        r["gv"][...] = gv
        r["la"][...] = la
        for f in rest:
            f()
    else:
        c, nchunk, gheads, nlev = gla
        s_ref = refs[n_in + n_out]
        ti = pl.program_id(0) % seq_tiles

        @pl.when(ti == 0)
        def _():
            s_ref[...] = r["s0"][0]

        outs, s = _gla_block(c, nchunk, gheads, nlev, r["lt"][...], r["lev"][...], gq, gk, gv, la,
                             [s_ref[h] for h in range(gheads)], fillers=rest)
        for ci, o in enumerate(outs):
            r["ob"][ci * c:(ci + 1) * c, :] = o
        for h in range(gheads):
            s_ref[h] = s[h]

        @pl.when(ti == seq_tiles - 1)
        def _():
            r["sout"][0] = s_ref[...]


def _proj(x2, g_in, weights, b_alpha, segs, tm, heads, seq=None, gla_state=None):
    tok, d = x2.shape
    w_main, w_merge, w_gate, wkt, wa_pad = weights
    widths = {k: v[1] - v[0] for k, v in segs.items() if isinstance(v, tuple)}
    widths["mg"] = w_merge.shape[1]
    kw = wa_pad.shape[1]
    seq_tiles = None if seq is None else seq // tm

    def row_spec(w):
        return pl.BlockSpec((tm, w), lambda i: (i, 0))

    def full_spec(shape):
        return pl.BlockSpec(shape, lambda i: (0,) * len(shape))

    def const_spec(shape):
        return pl.BlockSpec(shape, lambda i: (0,) * len(shape), pipeline_mode=pl.Buffered(1))

    ins = [("x", x2, row_spec(d)), ("g", g_in.reshape(1, d), full_spec((1, d))),
           ("w_main", w_main, const_spec(w_main.shape)), ("w_merge", w_merge, const_spec(w_merge.shape)),
           ("w_gate", w_gate, const_spec(w_gate.shape)), ("wkt", wkt, const_spec(wkt.shape)),
           ("wa", wa_pad, full_spec(wa_pad.shape)), ("ba", b_alpha.reshape(1, kw), full_spec((1, kw)))]
    out_defs = [
        ("q", widths["dq"], BF16), ("kf", widths["dk"], F32), ("kb", widths["dk"], BF16),
        ("vf", widths["dv"], F32), ("vb", widths["dv"], BF16), ("dg", widths["dg"], BF16),
        ("gg", widths["gg"], BF16), ("mg", widths["mg"], BF16),
    ]
    gla, scratch = None, []
    if gla_state is None:
        out_defs += [("gq", widths["gq"], F32), ("gk", widths["gk"], F32), ("gv", widths["gv"], BF16),
                     ("la", kw, F32)]
    else:
        out_defs += [("ob", widths["gv"], F32)]
    out_specs = {nm: row_spec(w) for nm, w, _ in out_defs}
    out_shape = {nm: jax.ShapeDtypeStruct((tok, w), dt) for nm, w, dt in out_defs}
    names = [nm for nm, _, _ in out_defs]
    if seq is not None:
        nt = seq_tiles
        for nm, dt in (("kf", F32), ("kb", BF16)):
            out_specs[nm] = pl.BlockSpec((1, widths["dk"], tm), lambda i: (i // nt, 0, i % nt))
            out_shape[nm] = jax.ShapeDtypeStruct((tok // seq, widths["dk"], seq), dt)
        out_specs["vf"] = pl.BlockSpec((tm * heads, LANES), lambda i: (i, 0))
        out_shape["vf"] = jax.ShapeDtypeStruct((tok * heads, LANES), F32)
    if gla_state is not None:
        gheads, dk, dv = gla_state.shape[1:]
        nchunk = tm // CHUNK
        lt3, lev, nlev = _gla_tables(CHUNK, gheads, nchunk)
        gla = (CHUNK, nchunk, gheads, nlev)
        st = pl.BlockSpec((1, gheads, dk, dv), lambda i: (i // nt, 0, 0, 0))
        ins += [("lt", jnp.asarray(lt3, BF16), full_spec(lt3.shape)), ("lev", jnp.asarray(lev), full_spec(lev.shape)),
                ("s0", gla_state, st)]
        names.append("sout")
        out_specs["sout"] = st
        out_shape["sout"] = jax.ShapeDtypeStruct(gla_state.shape, F32)
        scratch = [pltpu.VMEM((gheads, dk, dv), F32)]
    in_names = [nm for nm, _, _ in ins]
    outs = pl.pallas_call(
        functools.partial(_proj_kernel, segs, heads, seq_tiles, gla, in_names, names),
        grid=(tok // tm,),
        in_specs=[sp for _, _, sp in ins],
        out_specs=[out_specs[nm] for nm in names],
        out_shape=[out_shape[nm] for nm in names],
        scratch_shapes=scratch,
        compiler_params=_cparams(("arbitrary",)),
        name="proj",
    )(*[a for _, a, _ in ins])
    return dict(zip(names, outs))


def _diff_lambda(lamv, lam_init):
    a = jnp.sum(lamv[0:1, :] * lamv[1:2, :], axis=1, keepdims=True)
    b = jnp.sum(lamv[2:3, :] * lamv[3:4, :], axis=1, keepdims=True)
    return jnp.exp(a) - jnp.exp(b) + lam_init


def _stack_maps(q):
    lane = lax.broadcasted_iota(jnp.int32, q.shape, 1)
    half = q.shape[1] // 2
    zero = jnp.zeros_like(q)
    return jnp.concatenate([jnp.where(lane < half, q, zero), jnp.where(lane >= half, q, zero)], axis=0)


def _attn_prompt_kernel(lam_init, tq, ra, rb, lamv_ref, q_ref, kt_ref, v_ref, o_ref, *scratch):
    ones = jnp.ones((tq, LANES), BF16)
    bs0 = dict(zip(("qx", "acc", "s0", "s1", "m0", "m1", "a0", "a1"), scratch))
    nq = q_ref.shape[1] // tq

    def stage_a(bs, j, slot, first):
        sbuf, mbuf, abuf = (bs["s0"], bs["s1"]), (bs["m0"], bs["m1"]), (bs["a0"], bs["a1"])
        start = pl.multiple_of(j * tq, tq)
        kt = kt_ref[0, :, pl.ds(start, tq)]
        if first:
            hq = tq // 2
            for part in range(4):
                r0 = part * hq
                rows = slice(r0, r0 + hq)
                ncol = hq if part % 2 == 0 else tq
                s = jnp.dot(bs["qx"][rows, :], kt[:, :ncol], preferred_element_type=F32)
                qc = (r0 % tq + lax.broadcasted_iota(jnp.int32, s.shape, 0)) // CHUNK
                kc = lax.broadcasted_iota(jnp.int32, s.shape, 1) // CHUNK
                sbuf[slot][rows, :ncol] = jnp.where(kc <= qc, s, -jnp.inf)
                if ncol < tq:
                    sbuf[slot][rows, ncol:] = jnp.full((hq, tq - ncol), -jnp.inf, F32)
        else:
            sbuf[slot][...] = jnp.dot(bs["qx"][...], kt, preferred_element_type=F32)
        for r in range(2 * tq // ra):
            rows = slice(r * ra, (r + 1) * ra)
            m_cur = jnp.max(sbuf[slot][rows, :], axis=1, keepdims=True)
            if first:
                m_new = jnp.broadcast_to(m_cur, (ra, LANES))
                abuf[slot][rows, :] = jnp.zeros((ra, LANES), F32)
            else:
                m_prev = mbuf[1 - slot][rows, :]
                m_new = jnp.maximum(m_prev, m_cur)
                abuf[slot][rows, :] = jnp.exp2(m_prev - m_new)
            mbuf[slot][rows, :] = m_new

    def stage_b(bs, j, slot):
        sbuf, mbuf, abuf = (bs["s0"], bs["s1"]), (bs["m0"], bs["m1"]), (bs["a0"], bs["a1"])
        start = pl.multiple_of(j * tq, tq)
        vx = jnp.concatenate([v_ref[0, pl.ds(start, tq), :], ones], axis=1)
        for r in range(2 * tq // rb):
            rows = slice(r * rb, (r + 1) * rb)
            p = jnp.exp2(sbuf[slot][rows, :] - jnp.tile(mbuf[slot][rows, :], (1, tq // LANES)))
            pv = jnp.dot(p.astype(BF16), vx, preferred_element_type=F32)
            bs["acc"][rows, :] = bs["acc"][rows, :] * jnp.tile(abuf[slot][rows, :], (1, 2)) + pv

    def fill(bs, i):
        qrows = pl.ds(pl.multiple_of(i * tq, tq), tq)
        bs["qx"][...] = _stack_maps(q_ref[0, qrows, :])
        stage_a(bs, i, 0, True)

    def steady(bs, i):
        def pair(u, c):
            stage_a(bs, 2 * u, 1, False)
            stage_b(bs, jnp.where(u == 0, i, 2 * u - 1), 0)
            stage_a(bs, 2 * u + 1, 0, False)
            stage_b(bs, 2 * u, 1)
            return c

        lax.fori_loop(0, i // 2, pair, 0)

    def drain(bs, i, odd):
        last = jnp.where(i < 2, i, 2 * (i // 2) - 1)
        if odd:
            stage_a(bs, i - 1, 1, False)
            stage_b(bs, last, 0)
            stage_b(bs, i - 1, 1)
        else:
            stage_b(bs, last, 0)
        acc = bs["acc"][...]
        o1 = acc[:tq, :LANES] / acc[:tq, LANES:]
        o2 = acc[tq:, :LANES] / acc[tq:, LANES:]
        qrows = pl.ds(pl.multiple_of(i * tq, tq), tq)
        o_ref[0, qrows, :] = o1 - _diff_lambda(lamv_ref[...], lam_init) * o2

    def q_tile(i, carry):
        fill(bs0, i)
        steady(bs0, i)

        @pl.when(i % 2 == 0)
        def _():
            drain(bs0, i, False)

        @pl.when(i % 2 == 1)
        def _():
            drain(bs0, i, True)

        return carry

    bs0["acc"][...] = jnp.zeros(bs0["acc"].shape, F32)
    lax.fori_loop(0, nq, q_tile, 0)


def _attn_prompt(q, kt, v, lamv, lam_init, heads):
    bsz, t, w = q.shape
    tq = min(1024, t)
    ra = min(128, tq)
    rb = min(256, tq)
    rowbuf = pltpu.VMEM((2 * tq, LANES), F32)
    return pl.pallas_call(
        functools.partial(_attn_prompt_kernel, lam_init, tq, ra, rb),
        grid=(bsz, heads),
        in_specs=[pl.BlockSpec(lamv.shape, lambda b, h: (0, 0)),
                  pl.BlockSpec((1, t, LANES), lambda b, h: (b, 0, h)),
                  pl.BlockSpec((1, LANES, t), lambda b, h: (b, h, 0)),
                  pl.BlockSpec((1, t, LANES), lambda b, h: (b, 0, h))],
        out_specs=pl.BlockSpec((1, t, LANES), lambda b, h: (b, 0, h)),
        out_shape=jax.ShapeDtypeStruct((bsz, t, w), F32),
        scratch_shapes=[pltpu.VMEM((2 * tq, LANES), BF16), pltpu.VMEM((2 * tq, 2 * LANES), F32),
                        pltpu.VMEM((2 * tq, tq), F32), pltpu.VMEM((2 * tq, tq), F32),
                        rowbuf, rowbuf, rowbuf, rowbuf],
        compiler_params=_cparams(("arbitrary", "arbitrary")),
        name="attn_prompt",
    )(lamv, q, kt, v)


def _attn_sample_kernel(lam_init, heads, lamv_ref, q_ref, ckt_ref, cv_ref, kn_ref, vn_ref, o_ref):
    tq = q_ref.shape[1]
    past = ckt_ref.shape[2]
    lam = _diff_lambda(lamv_ref[...], lam_init)
    for h in range(heads):
        sl = slice(h * LANES, (h + 1) * LANES)
        qx = _stack_maps(q_ref[0, :, sl])
        kc = ckt_ref[0, sl, :].astype(BF16)
        vc = cv_ref[0, pl.ds(h, past, stride=heads), :].astype(BF16)
        kn = kn_ref[0, :, sl]
        vn = vn_ref[0, :, sl]
        sc = jnp.dot(qx, kc, preferred_element_type=F32)
        sn = lax.dot_general(qx, kn, _NT, preferred_element_type=F32)
        m = jnp.maximum(jnp.max(sc, axis=1, keepdims=True), jnp.max(sn, axis=1, keepdims=True))
        ec = jnp.exp2(sc - m)
        en = jnp.exp2(sn - m)
        l = jnp.sum(ec, axis=1, keepdims=True) + jnp.sum(en, axis=1, keepdims=True)
        o = (jnp.dot(ec.astype(BF16), vc, preferred_element_type=F32)
             + jnp.dot(en.astype(BF16), vn, preferred_element_type=F32)) / l
        o_ref[0, :, sl] = o[:tq] - lam * o[tq:]


def _attn_sample(q, ckt, cv, kn, vn, lamv, lam_init, heads):
    bsz, tq, w = q.shape
    new_spec = pl.BlockSpec((1, tq, w), lambda b: (b, 0, 0))
    return pl.pallas_call(
        functools.partial(_attn_sample_kernel, lam_init, heads),
        grid=(bsz,),
        in_specs=[pl.BlockSpec(lamv.shape, lambda b: (0, 0)), new_spec,
                  pl.BlockSpec((1,) + ckt.shape[1:], lambda b: (b, 0, 0)),
                  pl.BlockSpec((1,) + cv.shape[1:], lambda b: (b, 0, 0)), new_spec, new_spec],
        out_specs=new_spec,
        out_shape=jax.ShapeDtypeStruct((bsz, tq, w), F32),
        compiler_params=_cparams(("arbitrary",)),
        name="attn_sample",
    )(lamv, q, ckt, cv, kn, vn)


def _gla_tables(c, heads, nchunk):
    nlev = int(math.log2(c))
    assert 1 << nlev == c and c % 8 == 0
    t = np.arange(c)[:, None]
    u = np.arange(c)[None, :]
    lev = np.where(t == u, 0, -1)
    for l in range(1, nlev + 1):
        n = c >> l
        same_block = (t // (2 * n)) == (u // (2 * n))
        lev = np.where(same_block & ((t % (2 * n)) >= n) & ((u % (2 * n)) < n), l, lev)
    lt = (u <= t).astype(np.float32)
    return np.concatenate([lt, lt, lt], axis=1), np.tile(lev, (nchunk, heads)).astype(np.int32), nlev


def _block_reference(b, n):
    rows, w = b.shape
    if n >= 8:
        parts = [jnp.broadcast_to(b[g + n - 1:g + n, :], (2 * n, w)) for g in range(0, rows, 2 * n)]
        return parts[0] if len(parts) == 1 else jnp.concatenate(parts, axis=0)
    b3 = b.reshape(rows // 8, 8, w)
    sub = lax.broadcasted_iota(jnp.int32, b3.shape, 1)
    pick = lambda j: jnp.broadcast_to(b3[:, j:j + 1, :], b3.shape)
    if n == 4:
        ref = pick(3)
    elif n == 2:
        ref = jnp.where(sub < 4, pick(1), pick(5))
    else:
        ref = jnp.where(sub < 2, pick(0), jnp.where(sub < 4, pick(2), jnp.where(sub < 6, pick(4), pick(6))))
    return ref.reshape(rows, w)


def _gla_block(c, nchunk, heads, nlev, lt, lev, q, k, v, la, s, fillers=()):
    fillers = list(fillers)
    fill = lambda: fillers.pop(0)() if fillers else None
    kw = q.shape[-1]
    vw = v.shape[-1]
    dk = kw // heads
    dv = vw // heads
    pad = LANES - c
    assert dv == LANES and pad >= 0
    chunk = lambda a, ci: a[ci * c:(ci + 1) * c]

    la2 = la * math.log2(math.e)
    p0 = la2.astype(BF16)
    r0 = la2 - p0.astype(F32)
    p1 = r0.astype(BF16)
    p2 = (r0 - p1.astype(F32)).astype(BF16)
    b = jnp.concatenate(
        [jnp.dot(lt, jnp.concatenate([chunk(p0, ci), chunk(p1, ci), chunk(p2, ci)], axis=0),
                 preferred_element_type=F32) for ci in range(nchunk)], axis=0)

    t = lax.broadcasted_iota(jnp.int32, b.shape, 0) % c
    klane_h = lax.broadcasted_iota(jnp.int32, (c, kw), 1) // dk
    vlane_h = lax.broadcasted_iota(jnp.int32, (c, vw), 1) // dv

    att = None
    for l in range(nlev + 1):
        if l == 0:
            qt, kt = q.astype(BF16), k.astype(BF16)
        else:
            e = jnp.exp2(-jnp.abs(b - _block_reference(b, c >> l)))
            qt = (q * e).astype(BF16)
            kt = (k * e).astype(BF16)
        parts = []
        for ci in range(nchunk):
            ktc = chunk(kt, ci)
            kstack = jnp.concatenate(
                [jnp.where(klane_h == h, ktc, jnp.zeros_like(ktc)) for h in range(heads)], axis=0)
            parts.append(lax.dot_general(chunk(qt, ci), kstack, _NT, preferred_element_type=F32))
        a = parts[0] if nchunk == 1 else jnp.concatenate(parts, axis=0)
        att = jnp.where(lev == l, a, 0.0 if att is None else att)
        fill()
    att = att.astype(BF16)
    qe = (q * jnp.exp2(b)).astype(BF16)

    o_intra, upd, decay = [], [], []
    for ci in range(nchunk):
        rows = slice(ci * c, (ci + 1) * c)
        vc = v[rows]
        vbd = jnp.concatenate(
            [jnp.where(vlane_h == h, vc, jnp.zeros_like(vc)) for h in range(heads)], axis=0)
        o_intra.append(jnp.dot(att[rows], vbd, preferred_element_type=F32))
        b_last = b[ci * c + c - 1:ci * c + c, :]
        kd = k[rows] * jnp.exp2(b_last - b[rows])
        x = jnp.concatenate([kd, jnp.zeros((pad, kw), F32), jnp.broadcast_to(b_last, (LANES, kw))], axis=0)
        xt = x.T
        for h in range(heads):
            hk = slice(h * dk, (h + 1) * dk)
            vpad = jnp.concatenate([vc[:, h * dv:(h + 1) * dv], jnp.zeros((pad, dv), BF16)], axis=0)
            upd.append(jnp.dot(xt[hk, :LANES].astype(BF16), vpad, preferred_element_type=F32))
            decay.append(jnp.exp2(xt[hk, LANES:]))
        fill()

    zero = jnp.zeros((dk, dv), BF16)
    outs = []
    for ci in range(nchunk):
        sbd = jnp.concatenate(
            [jnp.concatenate([s[h].astype(BF16) if g == h else zero for g in range(heads)], axis=1)
             for h in range(heads)], axis=0)
        outs.append(o_intra[ci] + jnp.dot(chunk(qe, ci), sbd, preferred_element_type=F32))
        s = [decay[ci * heads + h] * s[h] + upd[ci * heads + h] for h in range(heads)]
    while fillers:
        fill()
    return outs, s


def _gla_kernel(c, nchunk, heads, nlev, lt_ref, lev_ref, q_ref, k_ref, v_ref, la_ref, s0_ref,
                o_ref, sout_ref, s_ref):
    ti = pl.program_id(1)

    @pl.when(ti == 0)
    def _():
        s_ref[...] = s0_ref[0]

    outs, s = _gla_block(c, nchunk, heads, nlev, lt_ref[...], lev_ref[...], q_ref[0], k_ref[0], v_ref[0],
                         la_ref[0], [s_ref[h] for h in range(heads)])
    for ci, o in enumerate(outs):
        o_ref[0, ci * c:(ci + 1) * c, :] = o
    for h in range(heads):
        s_ref[h] = s[h]

    @pl.when(ti == pl.num_programs(1) - 1)
    def _():
        sout_ref[0] = s_ref[...]


def _gla(q, k, v, la, s0, chunk):
    bsz, t, kw = q.shape
    vw = v.shape[-1]
    heads, dk, dv = s0.shape[1:]
    nchunk = min(4, t // chunk)
    tg = nchunk * chunk
    lt3, lev, nlev = _gla_tables(chunk, heads, nchunk)
    tok = lambda w: pl.BlockSpec((1, tg, w), lambda b, i: (b, i, 0))
    st = pl.BlockSpec((1, heads, dk, dv), lambda b, i: (b, 0, 0, 0))
    return pl.pallas_call(
        functools.partial(_gla_kernel, chunk, nchunk, heads, nlev),
        grid=(bsz, t // tg),
        in_specs=[pl.BlockSpec(lt3.shape, lambda b, i: (0, 0)), pl.BlockSpec(lev.shape, lambda b, i: (0, 0)),
                  tok(kw), tok(kw), tok(vw), tok(kw), st],
        out_specs=[tok(vw), st],
        out_shape=[jax.ShapeDtypeStruct((bsz, t, vw), F32), jax.ShapeDtypeStruct(s0.shape, F32)],
        scratch_shapes=[pltpu.VMEM((heads, dk, dv), F32)],
        compiler_params=_cparams(("arbitrary", "arbitrary")),
        name="gla",
    )(jnp.asarray(lt3, BF16), jnp.asarray(lev), q, k, v, la, s0)


def _head_rmsnorm(x, g, heads):
    outs = []
    for h in range(heads):
        blk = x[:, h * LANES:(h + 1) * LANES]
        ms = jnp.mean(blk * blk, axis=-1, keepdims=True)
        outs.append((blk * lax.rsqrt(ms + EPS)) * g)
    return jnp.concatenate(outs, axis=1)


def _merge_kernel(lam_init, final, da_heads, gla_heads, x_ref, oa_ref, ob_ref, dg_ref, gg_ref, mg_ref,
                  ga_ref, gb_ref, wa_ref, wb_ref, wo_ref, gf_ref, y_ref):
    d = x_ref.shape[-1]

    def silu(x):
        h = 0.5 * x
        return h * jnp.tanh(h) + h

    oa = (_head_rmsnorm(oa_ref[...], ga_ref[...], da_heads) * (1.0 - lam_init)) * silu(dg_ref[...].astype(F32))
    ob = _head_rmsnorm(ob_ref[...], gb_ref[...], gla_heads) * silu(gg_ref[...].astype(F32))
    ya = jnp.dot(oa.astype(BF16), wa_ref[...], preferred_element_type=F32)
    yb = jnp.dot(ob.astype(BF16), wb_ref[...], preferred_element_type=F32)
    ta = jnp.tanh(0.5 * mg_ref[:, :d].astype(F32))
    tb = jnp.tanh(0.5 * mg_ref[:, d:].astype(F32))
    mixed = 0.5 * ((ta * ya + ya) + (tb * yb + yb))
    y = x_ref[...] + jnp.dot(mixed.astype(BF16), wo_ref[...], preferred_element_type=F32)
    if final:
        ms = jnp.mean(y * y, axis=-1, keepdims=True)
        y = (y * lax.rsqrt(ms + EPS)) * gf_ref[...]
    y_ref[...] = y


def _merge(x2, oa, ob, dg, gg, mg, da_g, gla_g, wa, wb, wo, gf, lam_init, final, da_heads, gla_heads, tm):
    tok, d = x2.shape
    row = lambda w: pl.BlockSpec((tm, w), lambda i: (i, 0))
    full = lambda a: pl.BlockSpec(a.shape, lambda i: (0, 0))
    da_g = da_g.reshape(1, -1)
    gla_g = gla_g.reshape(1, -1)
    gf = gf.reshape(1, d)
    return pl.pallas_call(
        functools.partial(_merge_kernel, lam_init, final, da_heads, gla_heads),
        grid=(tok // tm,),
        in_specs=[row(d), row(oa.shape[1]), row(ob.shape[1]), row(dg.shape[1]), row(gg.shape[1]), row(mg.shape[1]),
                  full(da_g), full(gla_g), full(wa), full(wb), full(wo), full(gf)],
        out_specs=row(d),
        out_shape=jax.ShapeDtypeStruct((tok, d), F32),
        compiler_params=_cparams(("arbitrary",)),
        name="merge",
    )(x2, oa, ob, dg, gg, mg, da_g, gla_g, wa, wb, wo, gf)


def _layout(da_heads, da_dk, gla_heads, gla_dk, gla_dv):
    da_qk = da_heads * 2 * da_dk
    da_w = da_heads * 2 * da_dk
    kwid = gla_heads * gla_dk
    vwid = gla_heads * gla_dv
    names = ["dq", "dk", "dv", "dg", "gq", "gk", "gv", "gg"]
    sizes = [da_qk, da_qk, da_w, da_w, kwid, kwid, vwid, vwid]
    segs, off = {}, 0
    for nm, sz in zip(names, sizes):
        segs[nm] = (off, off + sz)
        off += sz
    segs["da_scale"] = da_dk ** -0.5 * math.log2(math.e)
    segs["gla_scale"] = gla_dk ** -0.5
    return segs, off


def kernel(x_prompt, x_sample, cache_k, cache_v, state_gla, norm_in_g, w_in, w_alpha_up, b_alpha,
           lambda_q1, lambda_k1, lambda_q2, lambda_k2, da_norm_g, gla_norm_g,
           w_branch_a, w_branch_b, w_out, norm_final_g):
    depth = w_in.shape[0]
    bsz, seq, d = x_prompt.shape
    dbsz, dseq, _ = x_sample.shape
    past = cache_k.shape[2]
    da_heads, da_dk = cache_k.shape[3], cache_k.shape[5]
    gla_heads, gla_dk, gla_dv = state_gla.shape[2:]
    rank = w_alpha_up.shape[1]
    kw = gla_heads * gla_dk
    assert 2 * da_dk == LANES and cache_v.shape[4] == LANES and gla_dv == LANES
    segs, ga_lo = _layout(da_heads, da_dk, gla_heads, gla_dk, gla_dv)

    xp = x_prompt.reshape(bsz * seq, d)
    xs = x_sample.reshape(dbsz * dseq, d)
    tm_p = min(512, bsz * seq)
    tm_s = min(512, dbsz * dseq)
    outs = [[] for _ in range(6)]
    for l in range(depth):
        lam_init = 0.8 - 0.6 * math.exp(-0.3 * l)
        final = l == depth - 1
        w = w_in[l]
        w_gate = jnp.pad(w[:, ga_lo:ga_lo + rank], ((0, 0), (0, LANES - rank))).astype(BF16)
        wkt = w[:, segs["dk"][0]:segs["dk"][1]].T.astype(BF16)
        wa_pad = jnp.pad(w_alpha_up[l], ((0, LANES - rank), (0, 0))).astype(BF16)
        weights = (w[:, :ga_lo].astype(BF16), w[:, ga_lo + rank:].astype(BF16), w_gate, wkt, wa_pad)
        lamv = jnp.stack([lambda_q1[l], lambda_k1[l], lambda_q2[l], lambda_k2[l]])
        wa, wb, wo = w_branch_a[l].astype(BF16), w_branch_b[l].astype(BF16), w_out[l].astype(BF16)

        p = _proj(xp, norm_in_g[l], weights, b_alpha[l], segs, tm_p, da_heads, seq=seq,
                  gla_state=jnp.zeros((bsz, gla_heads, gla_dk, gla_dv), F32))
        sh = lambda a: a.reshape(bsz, seq, a.shape[-1])
        oa = _attn_prompt(sh(p["q"]), p["kb"], sh(p["vb"]), lamv, lam_init, da_heads)
        s_p = p["sout"]
        xp = _merge(xp, oa.reshape(bsz * seq, -1), p["ob"], p["dg"], p["gg"], p["mg"],
                    da_norm_g[l], gla_norm_g[l], wa, wb, wo, norm_final_g, lam_init, final,
                    da_heads, gla_heads, tm_p)
        outs[0].append(jnp.transpose(p["kf"].reshape(bsz, da_heads, 2, da_dk, seq), (0, 4, 1, 2, 3)))
        outs[1].append(p["vf"].reshape(bsz, seq, da_heads, 2 * da_dk))
        outs[2].append(s_p)

        p = _proj(xs, norm_in_g[l], weights, b_alpha[l], segs, tm_s, da_heads)
        sh = lambda a: a.reshape(dbsz, dseq, a.shape[-1])
        ckt = jnp.transpose(cache_k[l], (0, 2, 3, 4, 1)).reshape(dbsz, da_heads * 2 * da_dk, past)
        oa = _attn_sample(sh(p["q"]), ckt, cache_v[l].reshape(dbsz, past * da_heads, 2 * da_dk),
                          sh(p["kb"]), sh(p["vb"]), lamv, lam_init, da_heads)
        ob, s_s = _gla(sh(p["gq"]), sh(p["gk"]), sh(p["gv"]), sh(p["la"]), state_gla[l], dseq)
        xs = _merge(xs, oa.reshape(dbsz * dseq, -1), ob.reshape(dbsz * dseq, -1), p["dg"], p["gg"], p["mg"],
                    da_norm_g[l], gla_norm_g[l], wa, wb, wo, norm_final_g, lam_init, final,
                    da_heads, gla_heads, tm_s)
        outs[3].append(p["kf"].reshape(dbsz, dseq, da_heads, 2, da_dk))
        outs[4].append(p["vf"].reshape(dbsz, dseq, da_heads, 2 * da_dk))
        outs[5].append(s_s)

    return (xp.reshape(bsz, seq, d), xs.reshape(dbsz, dseq, d),
            jnp.stack(outs[0]), jnp.stack(outs[1]), jnp.stack(outs[2]),
            jnp.stack(outs[3]), jnp.stack(outs[4]), jnp.stack(outs[5]))
```

```python
import functools
import math

import numpy as np
import jax
import jax.numpy as jnp
from jax import lax
from jax.experimental import pallas as pl
from jax.experimental.pallas import tpu as pltpu

F32 = jnp.float32
BF16 = jnp.bfloat16

EPS = 1e-6
CHUNK = 64
GLA_TAU = 16.0
LANES = 128
VMEM_LIMIT_BYTES = 56 * 1024 * 1024

_NT = (((1,), (1,)), ((), ()))
_TN = (((0,), (0,)), ((), ()))


def _log_sigmoid(x):
    return jnp.minimum(x, 0.0) - jnp.log1p(jnp.exp(-jnp.abs(x)))


def _cparams(sem):
    return pltpu.CompilerParams(dimension_semantics=sem, vmem_limit_bytes=VMEM_LIMIT_BYTES)


def _proj_kernel(segs, heads, seq_tiles, gla, in_names, out_names, *refs):
    n_in, n_out = len(in_names), len(out_names)
    r = dict(zip(in_names + out_names, refs[:n_in + n_out]))
    x = r["x"][...]
    tm = x.shape[0]
    ms = jnp.mean(x * x, axis=-1, keepdims=True)
    xn = ((x * lax.rsqrt(ms + EPS)) * r["g"][...]).astype(BF16)

    def seg(name):
        lo, hi = segs[name]
        return jnp.dot(xn, r["w_main"][:, lo:hi], preferred_element_type=F32)

    def put_q():
        r["q"][...] = (seg("dq") * segs["da_scale"]).astype(BF16)

    def put_k():
        if seq_tiles is not None:
            kt = lax.dot_general(r["wkt"][...], xn, _NT, preferred_element_type=F32)
            r["kf"][0] = kt
            r["kb"][0] = kt.astype(BF16)
        else:
            k = seg("dk")
            r["kf"][...] = k
            r["kb"][...] = k.astype(BF16)

    def put_v():
        v = seg("dv")
        r["vb"][...] = v.astype(BF16)
        if seq_tiles is not None:
            for h in range(heads):
                r["vf"][pl.ds(h, tm, stride=heads), :] = v[:, h * LANES:(h + 1) * LANES]
        else:
            r["vf"][...] = v

    def put_gate(name):
        r[name][...] = seg(name).astype(BF16)

    def put_merge(c0, step):
        r["mg"][:, c0:c0 + step] = jnp.dot(xn, r["w_merge"][:, c0:c0 + step],
                                           preferred_element_type=F32).astype(BF16)

    step = 512
    rest = [put_q, put_k, put_v, functools.partial(put_gate, "dg"), functools.partial(put_gate, "gg")]
    rest += [functools.partial(put_merge, c0, step) for c0 in range(0, r["w_merge"].shape[1], step)]

    ga = jnp.dot(xn, r["w_gate"][...], preferred_element_type=F32).astype(BF16)
    z = jnp.dot(ga, r["wa"][...], preferred_element_type=F32) + r["ba"][...]
    la = _log_sigmoid(z) / GLA_TAU
    gq = seg("gq") * segs["gla_scale"]
    gk = seg("gk")
    gv = seg("gv").astype(BF16)
    if gla is None:
        r["gq"][...] = gq
        r["gk"][...] = gk
        r["gv"][...] = gv
        r["la"][...] = la
        for f in rest:
            f()
    else:
        c, nchunk, gheads, nlev = gla
        s_ref = refs[n_in + n_out]
        ti = pl.program_id(0) % seq_tiles

        @pl.when(ti == 0)
        def _():
            s_ref[...] = r["s0"][0]

        outs, s = _gla_block(c, nchunk, gheads, nlev, r["lt"][...], r["lev"][...], gq, gk, gv, la,
                             [s_ref[h] for h in range(gheads)], fillers=rest)
        for ci, o in enumerate(outs):
            r["ob"][ci * c:(ci + 1) * c, :] = o
        for h in range(gheads):
            s_ref[h] = s[h]

        @pl.when(ti == seq_tiles - 1)
        def _():
            r["sout"][0] = s_ref[...]


def _proj(x2, g_in, weights, b_alpha, segs, tm, heads, seq=None, gla_state=None):
    tok, d = x2.shape
    w_main, w_merge, w_gate, wkt, wa_pad = weights
    widths = {k: v[1] - v[0] for k, v in segs.items() if isinstance(v, tuple)}
    widths["mg"] = w_merge.shape[1]
    kw = wa_pad.shape[1]
    seq_tiles = None if seq is None else seq // tm

    def row_spec(w):
        return pl.BlockSpec((tm, w), lambda i: (i, 0))

    def full_spec(shape):
        return pl.BlockSpec(shape, lambda i: (0,) * len(shape))

    def const_spec(shape):
        return pl.BlockSpec(shape, lambda i: (0,) * len(shape), pipeline_mode=pl.Buffered(1))

    ins = [("x", x2, row_spec(d)), ("g", g_in.reshape(1, d), full_spec((1, d))),
           ("w_main", w_main, const_spec(w_main.shape)), ("w_merge", w_merge, const_spec(w_merge.shape)),
           ("w_gate", w_gate, const_spec(w_gate.shape)), ("wkt", wkt, const_spec(wkt.shape)),
           ("wa", wa_pad, full_spec(wa_pad.shape)), ("ba", b_alpha.reshape(1, kw), full_spec((1, kw)))]
    out_defs = [
        ("q", widths["dq"], BF16), ("kf", widths["dk"], F32), ("kb", widths["dk"], BF16),
        ("vf", widths["dv"], F32), ("vb", widths["dv"], BF16), ("dg", widths["dg"], BF16),
        ("gg", widths["gg"], BF16), ("mg", widths["mg"], BF16),
    ]
    gla, scratch = None, []
    if gla_state is None:
        out_defs += [("gq", widths["gq"], F32), ("gk", widths["gk"], F32), ("gv", widths["gv"], BF16),
                     ("la", kw, F32)]
    else:
        out_defs += [("ob", widths["gv"], F32)]
    out_specs = {nm: row_spec(w) for nm, w, _ in out_defs}
    out_shape = {nm: jax.ShapeDtypeStruct((tok, w), dt) for nm, w, dt in out_defs}
    names = [nm for nm, _, _ in out_defs]
    if seq is not None:
        nt = seq_tiles
        for nm, dt in (("kf", F32), ("kb", BF16)):
            out_specs[nm] = pl.BlockSpec((1, widths["dk"], tm), lambda i: (i // nt, 0, i % nt))
            out_shape[nm] = jax.ShapeDtypeStruct((tok // seq, widths["dk"], seq), dt)
        out_specs["vf"] = pl.BlockSpec((tm * heads, LANES), lambda i: (i, 0))
        out_shape["vf"] = jax.ShapeDtypeStruct((tok * heads, LANES), F32)
    if gla_state is not None:
        gheads, dk, dv = gla_state.shape[1:]
        nchunk = tm // CHUNK
        lt3, lev, nlev = _gla_tables(CHUNK, gheads, nchunk)
        gla = (CHUNK, nchunk, gheads, nlev)
        st = pl.BlockSpec((1, gheads, dk, dv), lambda i: (i // nt, 0, 0, 0))
        ins += [("lt", jnp.asarray(lt3, BF16), full_spec(lt3.shape)), ("lev", jnp.asarray(lev), full_spec(lev.shape)),
                ("s0", gla_state, st)]
        names.append("sout")
        out_specs["sout"] = st
        out_shape["sout"] = jax.ShapeDtypeStruct(gla_state.shape, F32)
        scratch = [pltpu.VMEM((gheads, dk, dv), F32)]
    in_names = [nm for nm, _, _ in ins]
    outs = pl.pallas_call(
        functools.partial(_proj_kernel, segs, heads, seq_tiles, gla, in_names, names),
        grid=(tok // tm,),
        in_specs=[sp for _, _, sp in ins],
        out_specs=[out_specs[nm] for nm in names],
        out_shape=[out_shape[nm] for nm in names],
        scratch_shapes=scratch,
        compiler_params=_cparams(("arbitrary",)),
        name="proj",
    )(*[a for _, a, _ in ins])
    return dict(zip(names, outs))


def _diff_lambda(lamv, lam_init):
    a = jnp.sum(lamv[0:1, :] * lamv[1:2, :], axis=1, keepdims=True)
    b = jnp.sum(lamv[2:3, :] * lamv[3:4, :], axis=1, keepdims=True)
    return jnp.exp(a) - jnp.exp(b) + lam_init


def _stack_maps(q):
    lane = lax.broadcasted_iota(jnp.int32, q.shape, 1)
    half = q.shape[1] // 2
    zero = jnp.zeros_like(q)
    return jnp.concatenate([jnp.where(lane < half, q, zero), jnp.where(lane >= half, q, zero)], axis=0)


def _attn_prompt_kernel(lam_init, tq, ra, rb, lamv_ref, q_ref, kt_ref, v_ref, o_ref,
                        qx_ref, acc_ref, s0_ref, s1_ref, m0_ref, m1_ref, a0_ref, a1_ref):
    ones = jnp.ones((tq, LANES), BF16)
    sbuf, mbuf, abuf = (s0_ref, s1_ref), (m0_ref, m1_ref), (a0_ref, a1_ref)
    nq = q_ref.shape[1] // tq
    hq = tq // 2
    assert hq % max(ra, rb) == 0

    def visible(row0):
        return hq if (row0 % tq) < hq else tq

    def values(j):
        start = pl.multiple_of(j * tq, tq)
        return jnp.concatenate([v_ref[0, pl.ds(start, tq), :], ones], axis=1)

    def own_a(i):
        kt = kt_ref[0, :, pl.ds(pl.multiple_of(i * tq, tq), tq)]
        for part in range(4):
            r0 = part * hq
            ncol = visible(r0)
            s = jnp.dot(qx_ref[r0:r0 + hq, :], kt[:, :ncol], preferred_element_type=F32)
            qc = (r0 % tq + lax.broadcasted_iota(jnp.int32, s.shape, 0)) // CHUNK
            kc = lax.broadcasted_iota(jnp.int32, s.shape, 1) // CHUNK
            s0_ref[r0:r0 + hq, :ncol] = jnp.where(kc <= qc, s, -jnp.inf)
        for r in range(2 * tq // ra):
            rows = slice(r * ra, (r + 1) * ra)
            m_cur = jnp.max(s0_ref[rows, :visible(r * ra)], axis=1, keepdims=True)
            m0_ref[rows, :] = jnp.broadcast_to(m_cur, (ra, LANES))

    def own_b(i):
        vx = values(i)
        for r in range(2 * tq // rb):
            rows = slice(r * rb, (r + 1) * rb)
            ncol = visible(r * rb)
            p = jnp.exp2(s0_ref[rows, :ncol] - jnp.tile(m0_ref[rows, :], (1, ncol // LANES)))
            acc_ref[rows, :] = jnp.dot(p.astype(BF16), vx[:ncol], preferred_element_type=F32)

    def stage_a(j, slot):
        kt = kt_ref[0, :, pl.ds(pl.multiple_of(j * tq, tq), tq)]
        sbuf[slot][...] = jnp.dot(qx_ref[...], kt, preferred_element_type=F32)
        for r in range(2 * tq // ra):
            rows = slice(r * ra, (r + 1) * ra)
            m_prev = mbuf[1 - slot][rows, :]
            m_new = jnp.maximum(m_prev, jnp.max(sbuf[slot][rows, :], axis=1, keepdims=True))
            abuf[slot][rows, :] = jnp.exp2(m_prev - m_new)
            mbuf[slot][rows, :] = m_new

    def stage_b(j, slot):
        vx = values(j)
        for r in range(2 * tq // rb):
            rows = slice(r * rb, (r + 1) * rb)
            p = jnp.exp2(sbuf[slot][rows, :] - jnp.tile(mbuf[slot][rows, :], (1, tq // LANES)))
            pv = jnp.dot(p.astype(BF16), vx, preferred_element_type=F32)
            acc_ref[rows, :] = acc_ref[rows, :] * jnp.tile(abuf[slot][rows, :], (1, 2)) + pv

    def q_tile(i, carry):
        qrows = pl.ds(pl.multiple_of(i * tq, tq), tq)
        qx_ref[...] = _stack_maps(q_ref[0, qrows, :])
        own_a(i)

        @pl.when(i == 0)
        def _():
            own_b(i)

        @pl.when(i > 0)
        def _():
            stage_a(0, 1)
            own_b(i)

            def pair(u, c):
                stage_a(2 * u + 1, 0)
                stage_b(2 * u, 1)
                stage_a(2 * u + 2, 1)
                stage_b(2 * u + 1, 0)
                return c

            lax.fori_loop(0, (i - 1) // 2, pair, 0)

            @pl.when(i % 2 == 1)
            def _():
                stage_b(i - 1, 1)

            @pl.when(i % 2 == 0)
            def _():
                stage_a(i - 1, 0)
                stage_b(i - 2, 1)
                stage_b(i - 1, 0)

        acc = acc_ref[...]
        o1 = acc[:tq, :LANES] / acc[:tq, LANES:]
        o2 = acc[tq:, :LANES] / acc[tq:, LANES:]
        o_ref[0, qrows, :] = o1 - _diff_lambda(lamv_ref[...], lam_init) * o2
        return carry

    lax.fori_loop(0, nq, q_tile, 0)


def _attn_prompt(q, kt, v, lamv, lam_init, heads):
    bsz, t, w = q.shape
    tq = min(1024, t)
    ra = min(128, tq)
    rb = min(256, tq)
    rowbuf = pltpu.VMEM((2 * tq, LANES), F32)
    return pl.pallas_call(
        functools.partial(_attn_prompt_kernel, lam_init, tq, ra, rb),
        grid=(bsz, heads),
        in_specs=[pl.BlockSpec(lamv.shape, lambda b, h: (0, 0)),
                  pl.BlockSpec((1, t, LANES), lambda b, h: (b, 0, h)),
                  pl.BlockSpec((1, LANES, t), lambda b, h: (b, h, 0)),
                  pl.BlockSpec((1, t, LANES), lambda b, h: (b, 0, h))],
        out_specs=pl.BlockSpec((1, t, LANES), lambda b, h: (b, 0, h)),
        out_shape=jax.ShapeDtypeStruct((bsz, t, w), F32),
        scratch_shapes=[pltpu.VMEM((2 * tq, LANES), BF16), pltpu.VMEM((2 * tq, 2 * LANES), F32),
                        pltpu.VMEM((2 * tq, tq), F32), pltpu.VMEM((2 * tq, tq), F32),
                        rowbuf, rowbuf, rowbuf, rowbuf],
        compiler_params=_cparams(("arbitrary", "arbitrary")),
        name="attn_prompt",
    )(lamv, q, kt, v)


def _attn_sample_kernel(lam_init, heads, lamv_ref, q_ref, ckt_ref, cv_ref, kn_ref, vn_ref, o_ref):
    tq = q_ref.shape[1]
    past = ckt_ref.shape[2]
    lam = _diff_lambda(lamv_ref[...], lam_init)
    for h in range(heads):
        sl = slice(h * LANES, (h + 1) * LANES)
        qx = _stack_maps(q_ref[0, :, sl])
        kc = ckt_ref[0, sl, :].astype(BF16)
        vc = cv_ref[0, pl.ds(h, past, stride=heads), :].astype(BF16)
        kn = kn_ref[0, :, sl]
        vn = vn_ref[0, :, sl]
        sc = jnp.dot(qx, kc, preferred_element_type=F32)
        sn = lax.dot_general(qx, kn, _NT, preferred_element_type=F32)
        m = jnp.maximum(jnp.max(sc, axis=1, keepdims=True), jnp.max(sn, axis=1, keepdims=True))
        ec = jnp.exp2(sc - m)
        en = jnp.exp2(sn - m)
        l = jnp.sum(ec, axis=1, keepdims=True) + jnp.sum(en, axis=1, keepdims=True)
        o = (jnp.dot(ec.astype(BF16), vc, preferred_element_type=F32)
             + jnp.dot(en.astype(BF16), vn, preferred_element_type=F32)) / l
        o_ref[0, :, sl] = o[:tq] - lam * o[tq:]


def _attn_sample(q, ckt, cv, kn, vn, lamv, lam_init, heads):
    bsz, tq, w = q.shape
    new_spec = pl.BlockSpec((1, tq, w), lambda b: (b, 0, 0))
    return pl.pallas_call(
        functools.partial(_attn_sample_kernel, lam_init, heads),
        grid=(bsz,),
        in_specs=[pl.BlockSpec(lamv.shape, lambda b: (0, 0)), new_spec,
                  pl.BlockSpec((1,) + ckt.shape[1:], lambda b: (b, 0, 0)),
                  pl.BlockSpec((1,) + cv.shape[1:], lambda b: (b, 0, 0)), new_spec, new_spec],
        out_specs=new_spec,
        out_shape=jax.ShapeDtypeStruct((bsz, tq, w), F32),
        compiler_params=_cparams(("arbitrary",)),
        name="attn_sample",
    )(lamv, q, ckt, cv, kn, vn)


def _gla_tables(c, heads, nchunk):
    nlev = int(math.log2(c))
    assert 1 << nlev == c and c % 8 == 0
    t = np.arange(c)[:, None]
    u = np.arange(c)[None, :]
    lev = np.where(t == u, 0, -1)
    for l in range(1, nlev + 1):
        n = c >> l
        same_block = (t // (2 * n)) == (u // (2 * n))
        lev = np.where(same_block & ((t % (2 * n)) >= n) & ((u % (2 * n)) < n), l, lev)
    lt = (u <= t).astype(np.float32)
    return np.concatenate([lt, lt, lt], axis=1), np.tile(lev, (nchunk, heads)).astype(np.int32), nlev


def _block_reference(b, n):
    rows, w = b.shape
    if n >= 8:
        parts = [jnp.broadcast_to(b[g + n - 1:g + n, :], (2 * n, w)) for g in range(0, rows, 2 * n)]
        return parts[0] if len(parts) == 1 else jnp.concatenate(parts, axis=0)
    b3 = b.reshape(rows // 8, 8, w)
    sub = lax.broadcasted_iota(jnp.int32, b3.shape, 1)
    pick = lambda j: jnp.broadcast_to(b3[:, j:j + 1, :], b3.shape)
    if n == 4:
        ref = pick(3)
    elif n == 2:
        ref = jnp.where(sub < 4, pick(1), pick(5))
    else:
        ref = jnp.where(sub < 2, pick(0), jnp.where(sub < 4, pick(2), jnp.where(sub < 6, pick(4), pick(6))))
    return ref.reshape(rows, w)


def _gla_block(c, nchunk, heads, nlev, lt, lev, q, k, v, la, s, fillers=()):
    fillers = list(fillers)
    fill = lambda: fillers.pop(0)() if fillers else None
    kw = q.shape[-1]
    vw = v.shape[-1]
    dk = kw // heads
    dv = vw // heads
    pad = LANES - c
    assert dv == LANES and pad >= 0
    chunk = lambda a, ci: a[ci * c:(ci + 1) * c]

    la2 = la * math.log2(math.e)
    p0 = la2.astype(BF16)
    r0 = la2 - p0.astype(F32)
    p1 = r0.astype(BF16)
    p2 = (r0 - p1.astype(F32)).astype(BF16)
    b = jnp.concatenate(
        [jnp.dot(lt, jnp.concatenate([chunk(p0, ci), chunk(p1, ci), chunk(p2, ci)], axis=0),
                 preferred_element_type=F32) for ci in range(nchunk)], axis=0)

    t = lax.broadcasted_iota(jnp.int32, b.shape, 0) % c
    klane_h = lax.broadcasted_iota(jnp.int32, (c, kw), 1) // dk
    vlane_h = lax.broadcasted_iota(jnp.int32, (c, vw), 1) // dv

    att = None
    for l in range(nlev + 1):
        if l == 0:
            qt, kt = q.astype(BF16), k.astype(BF16)
        else:
            e = jnp.exp2(-jnp.abs(b - _block_reference(b, c >> l)))
            qt = (q * e).astype(BF16)
            kt = (k * e).astype(BF16)
        parts = []
        for ci in range(nchunk):
            ktc = chunk(kt, ci)
            kstack = jnp.concatenate(
                [jnp.where(klane_h == h, ktc, jnp.zeros_like(ktc)) for h in range(heads)], axis=0)
            parts.append(lax.dot_general(chunk(qt, ci), kstack, _NT, preferred_element_type=F32))
        a = parts[0] if nchunk == 1 else jnp.concatenate(parts, axis=0)
        att = jnp.where(lev == l, a, 0.0 if att is None else att)
        fill()
    att = att.astype(BF16)
    qe = (q * jnp.exp2(b)).astype(BF16)

    o_intra, upd, decay = [], [], []
    for ci in range(nchunk):
        rows = slice(ci * c, (ci + 1) * c)
        vc = v[rows]
        vbd = jnp.concatenate(
            [jnp.where(vlane_h == h, vc, jnp.zeros_like(vc)) for h in range(heads)], axis=0)
        o_intra.append(jnp.dot(att[rows], vbd, preferred_element_type=F32))
        b_last = b[ci * c + c - 1:ci * c + c, :]
        kd = k[rows] * jnp.exp2(b_last - b[rows])
        x = jnp.concatenate([kd, jnp.zeros((pad, kw), F32), jnp.broadcast_to(b_last, (LANES, kw))], axis=0)
        xt = x.T
        for h in range(heads):
            hk = slice(h * dk, (h + 1) * dk)
            vpad = jnp.concatenate([vc[:, h * dv:(h + 1) * dv], jnp.zeros((pad, dv), BF16)], axis=0)
            upd.append(jnp.dot(xt[hk, :LANES].astype(BF16), vpad, preferred_element_type=F32))
            decay.append(jnp.exp2(xt[hk, LANES:]))
        fill()

    zero = jnp.zeros((dk, dv), BF16)
    outs = []
    for ci in range(nchunk):
        sbd = jnp.concatenate(
            [jnp.concatenate([s[h].astype(BF16) if g == h else zero for g in range(heads)], axis=1)
             for h in range(heads)], axis=0)
        outs.append(o_intra[ci] + jnp.dot(chunk(qe, ci), sbd, preferred_element_type=F32))
        s = [decay[ci * heads + h] * s[h] + upd[ci * heads + h] for h in range(heads)]
    while fillers:
        fill()
    return outs, s


def _gla_kernel(c, nchunk, heads, nlev, lt_ref, lev_ref, q_ref, k_ref, v_ref, la_ref, s0_ref,
                o_ref, sout_ref, s_ref):
    ti = pl.program_id(1)

    @pl.when(ti == 0)
    def _():
        s_ref[...] = s0_ref[0]

    outs, s = _gla_block(c, nchunk, heads, nlev, lt_ref[...], lev_ref[...], q_ref[0], k_ref[0], v_ref[0],
                         la_ref[0], [s_ref[h] for h in range(heads)])
    for ci, o in enumerate(outs):
        o_ref[0, ci * c:(ci + 1) * c, :] = o
    for h in range(heads):
        s_ref[h] = s[h]

    @pl.when(ti == pl.num_programs(1) - 1)
    def _():
        sout_ref[0] = s_ref[...]


def _gla(q, k, v, la, s0, chunk):
    bsz, t, kw = q.shape
    vw = v.shape[-1]
    heads, dk, dv = s0.shape[1:]
    nchunk = min(4, t // chunk)
    tg = nchunk * chunk
    lt3, lev, nlev = _gla_tables(chunk, heads, nchunk)
    tok = lambda w: pl.BlockSpec((1, tg, w), lambda b, i: (b, i, 0))
    st = pl.BlockSpec((1, heads, dk, dv), lambda b, i: (b, 0, 0, 0))
    return pl.pallas_call(
        functools.partial(_gla_kernel, chunk, nchunk, heads, nlev),
        grid=(bsz, t // tg),
        in_specs=[pl.BlockSpec(lt3.shape, lambda b, i: (0, 0)), pl.BlockSpec(lev.shape, lambda b, i: (0, 0)),
                  tok(kw), tok(kw), tok(vw), tok(kw), st],
        out_specs=[tok(vw), st],
        out_shape=[jax.ShapeDtypeStruct((bsz, t, vw), F32), jax.ShapeDtypeStruct(s0.shape, F32)],
        scratch_shapes=[pltpu.VMEM((heads, dk, dv), F32)],
        compiler_params=_cparams(("arbitrary", "arbitrary")),
        name="gla",
    )(jnp.asarray(lt3, BF16), jnp.asarray(lev), q, k, v, la, s0)


def _head_rmsnorm(x, g, heads):
    outs = []
    for h in range(heads):
        blk = x[:, h * LANES:(h + 1) * LANES]
        ms = jnp.mean(blk * blk, axis=-1, keepdims=True)
        outs.append((blk * lax.rsqrt(ms + EPS)) * g)
    return jnp.concatenate(outs, axis=1)


def _merge_kernel(lam_init, final, da_heads, gla_heads, x_ref, oa_ref, ob_ref, dg_ref, gg_ref, mg_ref,
                  ga_ref, gb_ref, wa_ref, wb_ref, wo_ref, gf_ref, y_ref):
    d = x_ref.shape[-1]

    def silu(x):
        h = 0.5 * x
        return h * jnp.tanh(h) + h

    oa = (_head_rmsnorm(oa_ref[...], ga_ref[...], da_heads) * (1.0 - lam_init)) * silu(dg_ref[...].astype(F32))
    ob = _head_rmsnorm(ob_ref[...], gb_ref[...], gla_heads) * silu(gg_ref[...].astype(F32))
    ya = jnp.dot(oa.astype(BF16), wa_ref[...], preferred_element_type=F32)
    yb = jnp.dot(ob.astype(BF16), wb_ref[...], preferred_element_type=F32)
    ta = jnp.tanh(0.5 * mg_ref[:, :d].astype(F32))
    tb = jnp.tanh(0.5 * mg_ref[:, d:].astype(F32))
    mixed = 0.5 * ((ta * ya + ya) + (tb * yb + yb))
    y = x_ref[...] + jnp.dot(mixed.astype(BF16), wo_ref[...], preferred_element_type=F32)
    if final:
        ms = jnp.mean(y * y, axis=-1, keepdims=True)
        y = (y * lax.rsqrt(ms + EPS)) * gf_ref[...]
    y_ref[...] = y


def _merge(x2, oa, ob, dg, gg, mg, da_g, gla_g, wa, wb, wo, gf, lam_init, final, da_heads, gla_heads, tm):
    tok, d = x2.shape
    row = lambda w: pl.BlockSpec((tm, w), lambda i: (i, 0))
    full = lambda a: pl.BlockSpec(a.shape, lambda i: (0, 0))
    da_g = da_g.reshape(1, -1)
    gla_g = gla_g.reshape(1, -1)
    gf = gf.reshape(1, d)
    return pl.pallas_call(
        functools.partial(_merge_kernel, lam_init, final, da_heads, gla_heads),
        grid=(tok // tm,),
        in_specs=[row(d), row(oa.shape[1]), row(ob.shape[1]), row(dg.shape[1]), row(gg.shape[1]), row(mg.shape[1]),
                  full(da_g), full(gla_g), full(wa), full(wb), full(wo), full(gf)],
        out_specs=row(d),
        out_shape=jax.ShapeDtypeStruct((tok, d), F32),
        compiler_params=_cparams(("arbitrary",)),
        name="merge",
    )(x2, oa, ob, dg, gg, mg, da_g, gla_g, wa, wb, wo, gf)


def _layout(da_heads, da_dk, gla_heads, gla_dk, gla_dv):
    da_qk = da_heads * 2 * da_dk
    da_w = da_heads * 2 * da_dk
    kwid = gla_heads * gla_dk
    vwid = gla_heads * gla_dv
    names = ["dq", "dk", "dv", "dg", "gq", "gk", "gv", "gg"]
    sizes = [da_qk, da_qk, da_w, da_w, kwid, kwid, vwid, vwid]
    segs, off = {}, 0
    for nm, sz in zip(names, sizes):
        segs[nm] = (off, off + sz)
        off += sz
    segs["da_scale"] = da_dk ** -0.5 * math.log2(math.e)
    segs["gla_scale"] = gla_dk ** -0.5
    return segs, off


def kernel(x_prompt, x_sample, cache_k, cache_v, state_gla, norm_in_g, w_in, w_alpha_up, b_alpha,
           lambda_q1, lambda_k1, lambda_q2, lambda_k2, da_norm_g, gla_norm_g,
           w_branch_a, w_branch_b, w_out, norm_final_g):
    depth = w_in.shape[0]
    bsz, seq, d = x_prompt.shape
    dbsz, dseq, _ = x_sample.shape
    past = cache_k.shape[2]
    da_heads, da_dk = cache_k.shape[3], cache_k.shape[5]
    gla_heads, gla_dk, gla_dv = state_gla.shape[2:]
    rank = w_alpha_up.shape[1]
    kw = gla_heads * gla_dk
    assert 2 * da_dk == LANES and cache_v.shape[4] == LANES and gla_dv == LANES
    segs, ga_lo = _layout(da_heads, da_dk, gla_heads, gla_dk, gla_dv)

    xp = x_prompt.reshape(bsz * seq, d)
    xs = x_sample.reshape(dbsz * dseq, d)
    tm_p = min(512, bsz * seq)
    tm_s = min(512, dbsz * dseq)
    outs = [[] for _ in range(6)]
    for l in range(depth):
        lam_init = 0.8 - 0.6 * math.exp(-0.3 * l)
        final = l == depth - 1
        w = w_in[l]
        w_gate = jnp.pad(w[:, ga_lo:ga_lo + rank], ((0, 0), (0, LANES - rank))).astype(BF16)
        wkt = w[:, segs["dk"][0]:segs["dk"][1]].T.astype(BF16)
        wa_pad = jnp.pad(w_alpha_up[l], ((0, LANES - rank), (0, 0))).astype(BF16)
        weights = (w[:, :ga_lo].astype(BF16), w[:, ga_lo + rank:].astype(BF16), w_gate, wkt, wa_pad)
        lamv = jnp.stack([lambda_q1[l], lambda_k1[l], lambda_q2[l], lambda_k2[l]])
        wa, wb, wo = w_branch_a[l].astype(BF16), w_branch_b[l].astype(BF16), w_out[l].astype(BF16)

        p = _proj(xp, norm_in_g[l], weights, b_alpha[l], segs, tm_p, da_heads, seq=seq,
                  gla_state=jnp.zeros((bsz, gla_heads, gla_dk, gla_dv), F32))
        sh = lambda a: a.reshape(bsz, seq, a.shape[-1])
        oa = _attn_prompt(sh(p["q"]), p["kb"], sh(p["vb"]), lamv, lam_init, da_heads)
        s_p = p["sout"]
        xp = _merge(xp, oa.reshape(bsz * seq, -1), p["ob"], p["dg"], p["gg"], p["mg"],
                    da_norm_g[l], gla_norm_g[l], wa, wb, wo, norm_final_g, lam_init, final,
                    da_heads, gla_heads, tm_p)
        outs[0].append(jnp.transpose(p["kf"].reshape(bsz, da_heads, 2, da_dk, seq), (0, 4, 1, 2, 3)))
        outs[1].append(p["vf"].reshape(bsz, seq, da_heads, 2 * da_dk))
        outs[2].append(s_p)

        p = _proj(xs, norm_in_g[l], weights, b_alpha[l], segs, tm_s, da_heads)
        sh = lambda a: a.reshape(dbsz, dseq, a.shape[-1])
        ckt = jnp.transpose(cache_k[l], (0, 2, 3, 4, 1)).reshape(dbsz, da_heads * 2 * da_dk, past)
        oa = _attn_sample(sh(p["q"]), ckt, cache_v[l].reshape(dbsz, past * da_heads, 2 * da_dk),
                          sh(p["kb"]), sh(p["vb"]), lamv, lam_init, da_heads)
        ob, s_s = _gla(sh(p["gq"]), sh(p["gk"]), sh(p["gv"]), sh(p["la"]), state_gla[l], dseq)
        xs = _merge(xs, oa.reshape(dbsz * dseq, -1), ob.reshape(dbsz * dseq, -1), p["dg"], p["gg"], p["mg"],
                    da_norm_g[l], gla_norm_g[l], wa, wb, wo, norm_final_g, lam_init, final,
                    da_heads, gla_heads, tm_s)
        outs[3].append(p["kf"].reshape(dbsz, dseq, da_heads, 2, da_dk))
        outs[4].append(p["vf"].reshape(dbsz, dseq, da_heads, 2 * da_dk))
        outs[5].append(s_s)

    return (xp.reshape(bsz, seq, d), xs.reshape(dbsz, dseq, d),
            jnp.stack(outs[0]), jnp.stack(outs[1]), jnp.stack(outs[2]),
            jnp.stack(outs[3]), jnp.stack(outs[4]), jnp.stack(outs[5]))
```

```python
import functools
import math

import numpy as np
import jax
import jax.numpy as jnp
from jax import lax
from jax.experimental import pallas as pl
from jax.experimental.pallas import tpu as pltpu

F32 = jnp.float32
BF16 = jnp.bfloat16

EPS = 1e-6
CHUNK = 64
GLA_TAU = 16.0
LANES = 128
VMEM_LIMIT_BYTES = 56 * 1024 * 1024

_NT = (((1,), (1,)), ((), ()))
_TN = (((0,), (0,)), ((), ()))


def _log_sigmoid(x):
    return jnp.minimum(x, 0.0) - jnp.log1p(jnp.exp(-jnp.abs(x)))


def _cparams(sem):
    return pltpu.CompilerParams(dimension_semantics=sem, vmem_limit_bytes=VMEM_LIMIT_BYTES)


def _proj_kernel(segs, heads, seq_tiles, gla, in_names, out_names, *refs):
    n_in, n_out = len(in_names), len(out_names)
    r = dict(zip(in_names + out_names, refs[:n_in + n_out]))
    x = r["x"][...]
    tm = x.shape[0]
    ms = jnp.mean(x * x, axis=-1, keepdims=True)
    xn = ((x * lax.rsqrt(ms + EPS)) * r["g"][...]).astype(BF16)

    def seg(name):
        lo, hi = segs[name]
        return jnp.dot(xn, r["w_main"][:, lo:hi], preferred_element_type=F32)

    def put_q():
        r["q"][...] = (seg("dq") * segs["da_scale"]).astype(BF16)

    def put_k():
        if seq_tiles is not None:
            kt = lax.dot_general(r["wkt"][...], xn, _NT, preferred_element_type=F32)
            r["kf"][0] = kt
            r["kb"][0] = kt.astype(BF16)
        else:
            k = seg("dk")
            r["kf"][...] = k
            r["kb"][...] = k.astype(BF16)

    def put_v():
        v = seg("dv")
        r["vb"][...] = v.astype(BF16)
        if seq_tiles is not None:
            for h in range(heads):
                r["vf"][pl.ds(h, tm, stride=heads), :] = v[:, h * LANES:(h + 1) * LANES]
        else:
            r["vf"][...] = v

    def put_gate(name):
        r[name][...] = seg(name).astype(BF16)

    def put_merge(c0, step):
        r["mg"][:, c0:c0 + step] = jnp.dot(xn, r["w_merge"][:, c0:c0 + step],
                                           preferred_element_type=F32).astype(BF16)

    step = 512
    rest = [put_q, put_k, put_v, functools.partial(put_gate, "dg"), functools.partial(put_gate, "gg")]
    rest += [functools.partial(put_merge, c0, step) for c0 in range(0, r["w_merge"].shape[1], step)]

    ga = jnp.dot(xn, r["w_gate"][...], preferred_element_type=F32).astype(BF16)
    z = jnp.dot(ga, r["wa"][...], preferred_element_type=F32) + r["ba"][...]
    la = _log_sigmoid(z) / GLA_TAU
    gq = seg("gq") * segs["gla_scale"]
    gk = seg("gk")
    gv = seg("gv").astype(BF16)
    if gla is None:
        r["gq"][...] = gq
        r["gk"][...] = gk
        r["gv"][...] = gv
        r["la"][...] = la
        for f in rest:
            f()
    else:
        c, nchunk, gheads, nlev = gla
        s_ref = refs[n_in + n_out]
        ti = pl.program_id(0) % seq_tiles

        @pl.when(ti == 0)
        def _():
            s_ref[...] = r["s0"][0]

        outs, s = _gla_block(c, nchunk, gheads, nlev, r["lt"][...], r["lev"][...], gq, gk, gv, la,
                             [s_ref[h] for h in range(gheads)], fillers=rest)
        for ci, o in enumerate(outs):
            r["ob"][ci * c:(ci + 1) * c, :] = o
        for h in range(gheads):
            s_ref[h] = s[h]

        @pl.when(ti == seq_tiles - 1)
        def _():
            r["sout"][0] = s_ref[...]


def _proj(x2, g_in, weights, b_alpha, segs, tm, heads, seq=None, gla_state=None):
    tok, d = x2.shape
    w_main, w_merge, w_gate, wkt, wa_pad = weights
    widths = {k: v[1] - v[0] for k, v in segs.items() if isinstance(v, tuple)}
    widths["mg"] = w_merge.shape[1]
    kw = wa_pad.shape[1]
    seq_tiles = None if seq is None else seq // tm

    def row_spec(w):
        return pl.BlockSpec((tm, w), lambda i: (i, 0))

    def full_spec(shape):
        return pl.BlockSpec(shape, lambda i: (0,) * len(shape))

    def const_spec(shape):
        return pl.BlockSpec(shape, lambda i: (0,) * len(shape), pipeline_mode=pl.Buffered(1))

    ins = [("x", x2, row_spec(d)), ("g", g_in.reshape(1, d), full_spec((1, d))),
           ("w_main", w_main, const_spec(w_main.shape)), ("w_merge", w_merge, const_spec(w_merge.shape)),
           ("w_gate", w_gate, const_spec(w_gate.shape)), ("wkt", wkt, const_spec(wkt.shape)),
           ("wa", wa_pad, full_spec(wa_pad.shape)), ("ba", b_alpha.reshape(1, kw), full_spec((1, kw)))]
    out_defs = [
        ("q", widths["dq"], BF16), ("kf", widths["dk"], F32), ("kb", widths["dk"], BF16),
        ("vf", widths["dv"], F32), ("vb", widths["dv"], BF16), ("dg", widths["dg"], BF16),
        ("gg", widths["gg"], BF16), ("mg", widths["mg"], BF16),
    ]
    gla, scratch = None, []
    if gla_state is None:
        out_defs += [("gq", widths["gq"], F32), ("gk", widths["gk"], F32), ("gv", widths["gv"], BF16),
                     ("la", kw, F32)]
    else:
        out_defs += [("ob", widths["gv"], F32)]
    out_specs = {nm: row_spec(w) for nm, w, _ in out_defs}
    out_shape = {nm: jax.ShapeDtypeStruct((tok, w), dt) for nm, w, dt in out_defs}
    names = [nm for nm, _, _ in out_defs]
    if seq is not None:
        nt = seq_tiles
        for nm, dt in (("kf", F32), ("kb", BF16)):
            out_specs[nm] = pl.BlockSpec((1, widths["dk"], tm), lambda i: (i // nt, 0, i % nt))
            out_shape[nm] = jax.ShapeDtypeStruct((tok // seq, widths["dk"], seq), dt)
        out_specs["vf"] = pl.BlockSpec((tm * heads, LANES), lambda i: (i, 0))
        out_shape["vf"] = jax.ShapeDtypeStruct((tok * heads, LANES), F32)
    if gla_state is not None:
        gheads, dk, dv = gla_state.shape[1:]
        nchunk = tm // CHUNK
        lt3, lev, nlev = _gla_tables(CHUNK, gheads, nchunk)
        gla = (CHUNK, nchunk, gheads, nlev)
        st = pl.BlockSpec((1, gheads, dk, dv), lambda i: (i // nt, 0, 0, 0))
        ins += [("lt", jnp.asarray(lt3, BF16), full_spec(lt3.shape)), ("lev", jnp.asarray(lev), full_spec(lev.shape)),
                ("s0", gla_state, st)]
        names.append("sout")
        out_specs["sout"] = st
        out_shape["sout"] = jax.ShapeDtypeStruct(gla_state.shape, F32)
        scratch = [pltpu.VMEM((gheads, dk, dv), F32)]
    in_names = [nm for nm, _, _ in ins]
    outs = pl.pallas_call(
        functools.partial(_proj_kernel, segs, heads, seq_tiles, gla, in_names, names),
        grid=(tok // tm,),
        in_specs=[sp for _, _, sp in ins],
        out_specs=[out_specs[nm] for nm in names],
        out_shape=[out_shape[nm] for nm in names],
        scratch_shapes=scratch,
        compiler_params=_cparams(("arbitrary",)),
        name="proj",
    )(*[a for _, a, _ in ins])
    return dict(zip(names, outs))


def _diff_lambda(lamv, lam_init):
    a = jnp.sum(lamv[0:1, :] * lamv[1:2, :], axis=1, keepdims=True)
    b = jnp.sum(lamv[2:3, :] * lamv[3:4, :], axis=1, keepdims=True)
    return jnp.exp(a) - jnp.exp(b) + lam_init


def _stack_maps(q):
    lane = lax.broadcasted_iota(jnp.int32, q.shape, 1)
    half = q.shape[1] // 2
    zero = jnp.zeros_like(q)
    return jnp.concatenate([jnp.where(lane < half, q, zero), jnp.where(lane >= half, q, zero)], axis=0)


def _attn_prompt_kernel(lam_init, tq, ra, rb, lamv_ref, q_ref, kt_ref, v_ref, o_ref,
                        qx_ref, acc_ref, s0_ref, s1_ref, m0_ref, m1_ref, a0_ref, a1_ref):
    ones = jnp.ones((tq, LANES), BF16)
    sbuf, mbuf, abuf = (s0_ref, s1_ref), (m0_ref, m1_ref), (a0_ref, a1_ref)
    nq = q_ref.shape[1] // tq
    hq = tq // 2
    assert hq % max(ra, rb) == 0

    def visible(row0):
        return hq if (row0 % tq) < hq else tq

    def values(j):
        start = pl.multiple_of(j * tq, tq)
        return jnp.concatenate([v_ref[0, pl.ds(start, tq), :], ones], axis=1)

    def own_a(i):
        kt = kt_ref[0, :, pl.ds(pl.multiple_of(i * tq, tq), tq)]
        for part in range(4):
            r0 = part * hq
            ncol = visible(r0)
            s = jnp.dot(qx_ref[r0:r0 + hq, :], kt[:, :ncol], preferred_element_type=F32)
            qc = (r0 % tq + lax.broadcasted_iota(jnp.int32, s.shape, 0)) // CHUNK
            kc = lax.broadcasted_iota(jnp.int32, s.shape, 1) // CHUNK
            s0_ref[r0:r0 + hq, :ncol] = jnp.where(kc <= qc, s, -jnp.inf)
        for r in range(2 * tq // ra):
            rows = slice(r * ra, (r + 1) * ra)
            m_cur = jnp.max(s0_ref[rows, :visible(r * ra)], axis=1, keepdims=True)
            m0_ref[rows, :] = jnp.broadcast_to(m_cur, (ra, LANES))

    def own_b(i):
        vx = values(i)
        for r in range(2 * tq // rb):
            rows = slice(r * rb, (r + 1) * rb)
            ncol = visible(r * rb)
            p = jnp.exp2(s0_ref[rows, :ncol] - jnp.tile(m0_ref[rows, :], (1, ncol // LANES)))
            acc_ref[rows, :] = jnp.dot(p.astype(BF16), vx[:ncol], preferred_element_type=F32)

    def stage_a(j, slot):
        kt = kt_ref[0, :, pl.ds(pl.multiple_of(j * tq, tq), tq)]
        sbuf[slot][...] = jnp.dot(qx_ref[...], kt, preferred_element_type=F32)
        for r in range(2 * tq // ra):
            rows = slice(r * ra, (r + 1) * ra)
            m_prev = mbuf[1 - slot][rows, :]
            m_new = jnp.maximum(m_prev, jnp.max(sbuf[slot][rows, :], axis=1, keepdims=True))
            abuf[slot][rows, :] = jnp.exp2(m_prev - m_new)
            mbuf[slot][rows, :] = m_new

    def stage_b(j, slot):
        vx = values(j)
        for r in range(2 * tq // rb):
            rows = slice(r * rb, (r + 1) * rb)
            p = jnp.exp2(sbuf[slot][rows, :] - jnp.tile(mbuf[slot][rows, :], (1, tq // LANES)))
            pv = jnp.dot(p.astype(BF16), vx, preferred_element_type=F32)
            acc_ref[rows, :] = acc_ref[rows, :] * jnp.tile(abuf[slot][rows, :], (1, 2)) + pv

    def q_tile(i, carry):
        qrows = pl.ds(pl.multiple_of(i * tq, tq), tq)
        qx_ref[...] = _stack_maps(q_ref[0, qrows, :])
        own_a(i)

        @pl.when(i == 0)
        def _():
            own_b(i)

        @pl.when(i > 0)
        def _():
            stage_a(0, 1)
            own_b(i)

            def pair(u, c):
                stage_a(2 * u + 1, 0)
                stage_b(2 * u, 1)
                stage_a(2 * u + 2, 1)
                stage_b(2 * u + 1, 0)
                return c

            lax.fori_loop(0, (i - 1) // 2, pair, 0)

            @pl.when(i % 2 == 1)
            def _():
                stage_b(i - 1, 1)

            @pl.when(i % 2 == 0)
            def _():
                stage_a(i - 1, 0)
                stage_b(i - 2, 1)
                stage_b(i - 1, 0)

        acc = acc_ref[...]
        o1 = acc[:tq, :LANES] / acc[:tq, LANES:]
        o2 = acc[tq:, :LANES] / acc[tq:, LANES:]
        o_ref[0, qrows, :] = o1 - _diff_lambda(lamv_ref[...], lam_init) * o2
        return carry

    lax.fori_loop(0, nq, q_tile, 0)


def _attn_prompt(q, kt, v, lamv, lam_init, heads):
    bsz, t, w = q.shape
    tq = min(1024, t)
    ra = min(128, tq)
    rb = min(256, tq)
    rowbuf = pltpu.VMEM((2 * tq, LANES), F32)
    return pl.pallas_call(
        functools.partial(_attn_prompt_kernel, lam_init, tq, ra, rb),
        grid=(bsz, heads),
        in_specs=[pl.BlockSpec(lamv.shape, lambda b, h: (0, 0)),
                  pl.BlockSpec((1, t, LANES), lambda b, h: (b, 0, h)),
                  pl.BlockSpec((1, LANES, t), lambda b, h: (b, h, 0)),
                  pl.BlockSpec((1, t, LANES), lambda b, h: (b, 0, h))],
        out_specs=pl.BlockSpec((1, t, LANES), lambda b, h: (b, 0, h)),
        out_shape=jax.ShapeDtypeStruct((bsz, t, w), F32),
        scratch_shapes=[pltpu.VMEM((2 * tq, LANES), BF16), pltpu.VMEM((2 * tq, 2 * LANES), F32),
                        pltpu.VMEM((2 * tq, tq), F32), pltpu.VMEM((2 * tq, tq), F32),
                        rowbuf, rowbuf, rowbuf, rowbuf],
        compiler_params=_cparams(("arbitrary", "arbitrary")),
        name="attn_prompt",
    )(lamv, q, kt, v)


def _attn_sample_kernel(lam_init, heads, lamv_ref, q_ref, ckt_ref, cv_ref, kn_ref, vn_ref, o_ref):
    tq = q_ref.shape[1]
    past = ckt_ref.shape[2]
    lam = _diff_lambda(lamv_ref[...], lam_init)
    for h in range(heads):
        sl = slice(h * LANES, (h + 1) * LANES)
        qx = _stack_maps(q_ref[0, :, sl])
        kc = ckt_ref[0, sl, :].astype(BF16)
        vc = cv_ref[0, pl.ds(h, past, stride=heads), :].astype(BF16)
        kn = kn_ref[0, :, sl]
        vn = vn_ref[0, :, sl]
        sc = jnp.dot(qx, kc, preferred_element_type=F32)
        sn = lax.dot_general(qx, kn, _NT, preferred_element_type=F32)
        m = jnp.maximum(jnp.max(sc, axis=1, keepdims=True), jnp.max(sn, axis=1, keepdims=True))
        ec = jnp.exp2(sc - m)
        en = jnp.exp2(sn - m)
        l = jnp.sum(ec, axis=1, keepdims=True) + jnp.sum(en, axis=1, keepdims=True)
        o = (jnp.dot(ec.astype(BF16), vc, preferred_element_type=F32)
             + jnp.dot(en.astype(BF16), vn, preferred_element_type=F32)) / l
        o_ref[0, :, sl] = o[:tq] - lam * o[tq:]


def _attn_sample(q, ckt, cv, kn, vn, lamv, lam_init, heads):
    bsz, tq, w = q.shape
    new_spec = pl.BlockSpec((1, tq, w), lambda b: (b, 0, 0))
    return pl.pallas_call(
        functools.partial(_attn_sample_kernel, lam_init, heads),
        grid=(bsz,),
        in_specs=[pl.BlockSpec(lamv.shape, lambda b: (0, 0)), new_spec,
                  pl.BlockSpec((1,) + ckt.shape[1:], lambda b: (b, 0, 0)),
                  pl.BlockSpec((1,) + cv.shape[1:], lambda b: (b, 0, 0)), new_spec, new_spec],
        out_specs=new_spec,
        out_shape=jax.ShapeDtypeStruct((bsz, tq, w), F32),
        compiler_params=_cparams(("arbitrary",)),
        name="attn_sample",
    )(lamv, q, ckt, cv, kn, vn)


def _gla_tables(c, heads, nchunk):
    nlev = int(math.log2(c))
    assert 1 << nlev == c and c % 8 == 0
    t = np.arange(c)[:, None]
    u = np.arange(c)[None, :]
    lev = np.where(t == u, 0, -1)
    for l in range(1, nlev + 1):
        n = c >> l
        same_block = (t // (2 * n)) == (u // (2 * n))
        lev = np.where(same_block & ((t % (2 * n)) >= n) & ((u % (2 * n)) < n), l, lev)
    lt = (u <= t).astype(np.float32)
    return np.concatenate([lt, lt, lt], axis=1), np.tile(lev, (nchunk, heads)).astype(np.int32), nlev


def _block_reference(b, n):
    rows, w = b.shape
    if n >= 8:
        parts = [jnp.broadcast_to(b[g + n - 1:g + n, :], (2 * n, w)) for g in range(0, rows, 2 * n)]
        return parts[0] if len(parts) == 1 else jnp.concatenate(parts, axis=0)
    b3 = b.reshape(rows // 8, 8, w)
    sub = lax.broadcasted_iota(jnp.int32, b3.shape, 1)
    pick = lambda j: jnp.broadcast_to(b3[:, j:j + 1, :], b3.shape)
    if n == 4:
        ref = pick(3)
    elif n == 2:
        ref = jnp.where(sub < 4, pick(1), pick(5))
    else:
        ref = jnp.where(sub < 2, pick(0), jnp.where(sub < 4, pick(2), jnp.where(sub < 6, pick(4), pick(6))))
    return ref.reshape(rows, w)


def _gla_block(c, nchunk, heads, nlev, lt, lev, q, k, v, la, s, fillers=(), independent=False):
    s_in = s
    fillers = list(fillers)
    fill = lambda: fillers.pop(0)() if fillers else None
    kw = q.shape[-1]
    vw = v.shape[-1]
    dk = kw // heads
    dv = vw // heads
    pad = LANES - c
    assert dv == LANES and pad >= 0
    chunk = lambda a, ci: a[ci * c:(ci + 1) * c]

    la2 = la * math.log2(math.e)
    p0 = la2.astype(BF16)
    r0 = la2 - p0.astype(F32)
    p1 = r0.astype(BF16)
    p2 = (r0 - p1.astype(F32)).astype(BF16)
    b = jnp.concatenate(
        [jnp.dot(lt, jnp.concatenate([chunk(p0, ci), chunk(p1, ci), chunk(p2, ci)], axis=0),
                 preferred_element_type=F32) for ci in range(nchunk)], axis=0)

    t = lax.broadcasted_iota(jnp.int32, b.shape, 0) % c
    klane_h = lax.broadcasted_iota(jnp.int32, (c, kw), 1) // dk
    vlane_h = lax.broadcasted_iota(jnp.int32, (c, vw), 1) // dv

    att = None
    for l in range(nlev + 1):
        if l == 0:
            qt, kt = q.astype(BF16), k.astype(BF16)
        else:
            e = jnp.exp2(-jnp.abs(b - _block_reference(b, c >> l)))
            qt = (q * e).astype(BF16)
            kt = (k * e).astype(BF16)
        parts = []
        for ci in range(nchunk):
            ktc = chunk(kt, ci)
            kstack = jnp.concatenate(
                [jnp.where(klane_h == h, ktc, jnp.zeros_like(ktc)) for h in range(heads)], axis=0)
            parts.append(lax.dot_general(chunk(qt, ci), kstack, _NT, preferred_element_type=F32))
        a = parts[0] if nchunk == 1 else jnp.concatenate(parts, axis=0)
        att = jnp.where(lev == l, a, 0.0 if att is None else att)
        fill()
    att = att.astype(BF16)
    qe = (q * jnp.exp2(b)).astype(BF16)

    o_intra, upd, decay = [], [], []
    for ci in range(nchunk):
        rows = slice(ci * c, (ci + 1) * c)
        vc = v[rows]
        vbd = jnp.concatenate(
            [jnp.where(vlane_h == h, vc, jnp.zeros_like(vc)) for h in range(heads)], axis=0)
        o_intra.append(jnp.dot(att[rows], vbd, preferred_element_type=F32))
        b_last = b[ci * c + c - 1:ci * c + c, :]
        kd = k[rows] * jnp.exp2(b_last - b[rows])
        x = jnp.concatenate([kd, jnp.zeros((pad, kw), F32), jnp.broadcast_to(b_last, (LANES, kw))], axis=0)
        xt = x.T
        for h in range(heads):
            hk = slice(h * dk, (h + 1) * dk)
            vpad = jnp.concatenate([vc[:, h * dv:(h + 1) * dv], jnp.zeros((pad, dv), BF16)], axis=0)
            upd.append(jnp.dot(xt[hk, :LANES].astype(BF16), vpad, preferred_element_type=F32))
            decay.append(jnp.exp2(xt[hk, LANES:]))
        fill()

    zero = jnp.zeros((dk, dv), BF16)
    outs, s_out = [], []
    for ci in range(nchunk):
        if independent:
            s = s_in[ci]
        sbd = jnp.concatenate(
            [jnp.concatenate([s[h].astype(BF16) if g == h else zero for g in range(heads)], axis=1)
             for h in range(heads)], axis=0)
        outs.append(o_intra[ci] + jnp.dot(chunk(qe, ci), sbd, preferred_element_type=F32))
        s = [decay[ci * heads + h] * s[h] + upd[ci * heads + h] for h in range(heads)]
        s_out.append(s)
    while fillers:
        fill()
    return outs, (s_out if independent else s)


def _gla_rows_kernel(c, nrow, heads, nlev, lt_ref, lev_ref, q_ref, k_ref, v_ref, la_ref, s0_ref,
                     o_ref, sout_ref):
    flat = lambda ref: ref[...].reshape(nrow * c, ref.shape[-1])
    states = [[s0_ref[n, h] for h in range(heads)] for n in range(nrow)]
    outs, s = _gla_block(c, nrow, heads, nlev, lt_ref[...], lev_ref[...], flat(q_ref), flat(k_ref), flat(v_ref),
                         flat(la_ref), states, independent=True)
    for n in range(nrow):
        o_ref[n] = outs[n]
        for h in range(heads):
            sout_ref[n, h] = s[n][h]


def _gla_rows(q, k, v, la, s0):
    bsz, c, kw = q.shape
    vw = v.shape[-1]
    heads, dk, dv = s0.shape[1:]
    nrow = math.gcd(bsz, 4)
    lt3, lev, nlev = _gla_tables(c, heads, nrow)
    tok = lambda w: pl.BlockSpec((nrow, c, w), lambda b: (b, 0, 0))
    st = pl.BlockSpec((nrow, heads, dk, dv), lambda b: (b, 0, 0, 0))
    return pl.pallas_call(
        functools.partial(_gla_rows_kernel, c, nrow, heads, nlev),
        grid=(bsz // nrow,),
        in_specs=[pl.BlockSpec(lt3.shape, lambda b: (0, 0)), pl.BlockSpec(lev.shape, lambda b: (0, 0)),
                  tok(kw), tok(kw), tok(vw), tok(kw), st],
        out_specs=[tok(vw), st],
        out_shape=[jax.ShapeDtypeStruct((bsz, c, vw), F32), jax.ShapeDtypeStruct(s0.shape, F32)],
        compiler_params=_cparams(("arbitrary",)),
        name="gla",
    )(jnp.asarray(lt3, BF16), jnp.asarray(lev), q, k, v, la, s0)


def _head_rmsnorm(x, g, heads):
    outs = []
    for h in range(heads):
        blk = x[:, h * LANES:(h + 1) * LANES]
        ms = jnp.mean(blk * blk, axis=-1, keepdims=True)
        outs.append((blk * lax.rsqrt(ms + EPS)) * g)
    return jnp.concatenate(outs, axis=1)


def _merge_kernel(lam_init, final, da_heads, gla_heads, x_ref, oa_ref, ob_ref, dg_ref, gg_ref, mg_ref,
                  ga_ref, gb_ref, wa_ref, wb_ref, wo_ref, gf_ref, y_ref):
    d = x_ref.shape[-1]

    def silu(x):
        h = 0.5 * x
        return h * jnp.tanh(h) + h

    oa = (_head_rmsnorm(oa_ref[...], ga_ref[...], da_heads) * (1.0 - lam_init)) * silu(dg_ref[...].astype(F32))
    ob = _head_rmsnorm(ob_ref[...], gb_ref[...], gla_heads) * silu(gg_ref[...].astype(F32))
    ya = jnp.dot(oa.astype(BF16), wa_ref[...], preferred_element_type=F32)
    yb = jnp.dot(ob.astype(BF16), wb_ref[...], preferred_element_type=F32)
    ta = jnp.tanh(0.5 * mg_ref[:, :d].astype(F32))
    tb = jnp.tanh(0.5 * mg_ref[:, d:].astype(F32))
    mixed = 0.5 * ((ta * ya + ya) + (tb * yb + yb))
    y = x_ref[...] + jnp.dot(mixed.astype(BF16), wo_ref[...], preferred_element_type=F32)
    if final:
        ms = jnp.mean(y * y, axis=-1, keepdims=True)
        y = (y * lax.rsqrt(ms + EPS)) * gf_ref[...]
    y_ref[...] = y


def _merge(x2, oa, ob, dg, gg, mg, da_g, gla_g, wa, wb, wo, gf, lam_init, final, da_heads, gla_heads, tm):
    tok, d = x2.shape
    row = lambda w: pl.BlockSpec((tm, w), lambda i: (i, 0))
    full = lambda a: pl.BlockSpec(a.shape, lambda i: (0, 0))
    da_g = da_g.reshape(1, -1)
    gla_g = gla_g.reshape(1, -1)
    gf = gf.reshape(1, d)
    return pl.pallas_call(
        functools.partial(_merge_kernel, lam_init, final, da_heads, gla_heads),
        grid=(tok // tm,),
        in_specs=[row(d), row(oa.shape[1]), row(ob.shape[1]), row(dg.shape[1]), row(gg.shape[1]), row(mg.shape[1]),
                  full(da_g), full(gla_g), full(wa), full(wb), full(wo), full(gf)],
        out_specs=row(d),
        out_shape=jax.ShapeDtypeStruct((tok, d), F32),
        compiler_params=_cparams(("arbitrary",)),
        name="merge",
    )(x2, oa, ob, dg, gg, mg, da_g, gla_g, wa, wb, wo, gf)


def _layout(da_heads, da_dk, gla_heads, gla_dk, gla_dv):
    da_qk = da_heads * 2 * da_dk
    da_w = da_heads * 2 * da_dk
    kwid = gla_heads * gla_dk
    vwid = gla_heads * gla_dv
    names = ["dq", "dk", "dv", "dg", "gq", "gk", "gv", "gg"]
    sizes = [da_qk, da_qk, da_w, da_w, kwid, kwid, vwid, vwid]
    segs, off = {}, 0
    for nm, sz in zip(names, sizes):
        segs[nm] = (off, off + sz)
        off += sz
    segs["da_scale"] = da_dk ** -0.5 * math.log2(math.e)
    segs["gla_scale"] = gla_dk ** -0.5
    return segs, off


def kernel(x_prompt, x_sample, cache_k, cache_v, state_gla, norm_in_g, w_in, w_alpha_up, b_alpha,
           lambda_q1, lambda_k1, lambda_q2, lambda_k2, da_norm_g, gla_norm_g,
           w_branch_a, w_branch_b, w_out, norm_final_g):
    depth = w_in.shape[0]
    bsz, seq, d = x_prompt.shape
    dbsz, dseq, _ = x_sample.shape
    past = cache_k.shape[2]
    da_heads, da_dk = cache_k.shape[3], cache_k.shape[5]
    gla_heads, gla_dk, gla_dv = state_gla.shape[2:]
    rank = w_alpha_up.shape[1]
    kw = gla_heads * gla_dk
    assert 2 * da_dk == LANES and cache_v.shape[4] == LANES and gla_dv == LANES
    segs, ga_lo = _layout(da_heads, da_dk, gla_heads, gla_dk, gla_dv)

    xp = x_prompt.reshape(bsz * seq, d)
    xs = x_sample.reshape(dbsz * dseq, d)
    tm_p = min(512, bsz * seq)
    tm_s = min(512, dbsz * dseq)
    outs = [[] for _ in range(6)]
    for l in range(depth):
        lam_init = 0.8 - 0.6 * math.exp(-0.3 * l)
        final = l == depth - 1
        w = w_in[l]
        w_gate = jnp.pad(w[:, ga_lo:ga_lo + rank], ((0, 0), (0, LANES - rank))).astype(BF16)
        wkt = w[:, segs["dk"][0]:segs["dk"][1]].T.astype(BF16)
        wa_pad = jnp.pad(w_alpha_up[l], ((0, LANES - rank), (0, 0))).astype(BF16)
        weights = (w[:, :ga_lo].astype(BF16), w[:, ga_lo + rank:].astype(BF16), w_gate, wkt, wa_pad)
        lamv = jnp.stack([lambda_q1[l], lambda_k1[l], lambda_q2[l], lambda_k2[l]])
        wa, wb, wo = w_branch_a[l].astype(BF16), w_branch_b[l].astype(BF16), w_out[l].astype(BF16)

        p = _proj(xp, norm_in_g[l], weights, b_alpha[l], segs, tm_p, da_heads, seq=seq,
                  gla_state=jnp.zeros((bsz, gla_heads, gla_dk, gla_dv), F32))
        sh = lambda a: a.reshape(bsz, seq, a.shape[-1])
        oa = _attn_prompt(sh(p["q"]), p["kb"], sh(p["vb"]), lamv, lam_init, da_heads)
        s_p = p["sout"]
        xp = _merge(xp, oa.reshape(bsz * seq, -1), p["ob"], p["dg"], p["gg"], p["mg"],
                    da_norm_g[l], gla_norm_g[l], wa, wb, wo, norm_final_g, lam_init, final,
                    da_heads, gla_heads, tm_p)
        outs[0].append(jnp.transpose(p["kf"].reshape(bsz, da_heads, 2, da_dk, seq), (0, 4, 1, 2, 3)))
        outs[1].append(p["vf"].reshape(bsz, seq, da_heads, 2 * da_dk))
        outs[2].append(s_p)

        p = _proj(xs, norm_in_g[l], weights, b_alpha[l], segs, tm_s, da_heads)
        sh = lambda a: a.reshape(dbsz, dseq, a.shape[-1])
        ckt = jnp.transpose(cache_k[l], (0, 2, 3, 4, 1)).reshape(dbsz, da_heads * 2 * da_dk, past)
        oa = _attn_sample(sh(p["q"]), ckt, cache_v[l].reshape(dbsz, past * da_heads, 2 * da_dk),
                          sh(p["kb"]), sh(p["vb"]), lamv, lam_init, da_heads)
        ob, s_s = _gla_rows(sh(p["gq"]), sh(p["gk"]), sh(p["gv"]), sh(p["la"]), state_gla[l])
        xs = _merge(xs, oa.reshape(dbsz * dseq, -1), ob.reshape(dbsz * dseq, -1), p["dg"], p["gg"], p["mg"],
                    da_norm_g[l], gla_norm_g[l], wa, wb, wo, norm_final_g, lam_init, final,
                    da_heads, gla_heads, tm_s)
        outs[3].append(p["kf"].reshape(dbsz, dseq, da_heads, 2, da_dk))
        outs[4].append(p["vf"].reshape(dbsz, dseq, da_heads, 2 * da_dk))
        outs[5].append(s_s)

    return (xp.reshape(bsz, seq, d), xs.reshape(dbsz, dseq, d),
            jnp.stack(outs[0]), jnp.stack(outs[1]), jnp.stack(outs[2]),
            jnp.stack(outs[3]), jnp.stack(outs[4]), jnp.stack(outs[5]))
```

```python
import functools
import math

import numpy as np
import jax
import jax.numpy as jnp
from jax import lax
from jax.experimental import pallas as pl
from jax.experimental.pallas import tpu as pltpu

F32 = jnp.float32
BF16 = jnp.bfloat16

EPS = 1e-6
CHUNK = 64
GLA_TAU = 16.0
LANES = 128
VMEM_LIMIT_BYTES = 56 * 1024 * 1024

_NT = (((1,), (1,)), ((), ()))
_TN = (((0,), (0,)), ((), ()))


def _log_sigmoid(x):
    return jnp.minimum(x, 0.0) - jnp.log1p(jnp.exp(-jnp.abs(x)))


def _cparams(sem):
    return pltpu.CompilerParams(dimension_semantics=sem, vmem_limit_bytes=VMEM_LIMIT_BYTES)


def _proj_kernel(segs, heads, seq_tiles, gla, in_names, out_names, *refs):
    n_in, n_out = len(in_names), len(out_names)
    r = dict(zip(in_names + out_names, refs[:n_in + n_out]))
    x = r["x"][...]
    tm = x.shape[0]
    ms = jnp.mean(x * x, axis=-1, keepdims=True)
    xn = ((x * lax.rsqrt(ms + EPS)) * r["g"][...]).astype(BF16)

    def seg(name):
        lo, hi = segs[name]
        return jnp.dot(xn, r["w_main"][:, lo:hi], preferred_element_type=F32)

    def put_q():
        r["q"][...] = (seg("dq") * segs["da_scale"]).astype(BF16)

    def put_k():
        if seq_tiles is not None:
            kt = lax.dot_general(r["wkt"][...], xn, _NT, preferred_element_type=F32)
            r["kf"][0] = kt
            r["kb"][0] = kt.astype(BF16)
        else:
            k = seg("dk")
            r["kf"][...] = k
            r["kb"][...] = k.astype(BF16)

    def put_v():
        v = seg("dv")
        r["vb"][...] = v.astype(BF16)
        if seq_tiles is not None:
            for h in range(heads):
                r["vf"][pl.ds(h, tm, stride=heads), :] = v[:, h * LANES:(h + 1) * LANES]
        else:
            r["vf"][...] = v

    def put_gate(name):
        r[name][...] = (0.5 * seg(name)).astype(BF16)

    def put_merge(c0, step):
        r["mg"][:, c0:c0 + step] = (0.5 * jnp.dot(xn, r["w_merge"][:, c0:c0 + step],
                                                  preferred_element_type=F32)).astype(BF16)

    step = 512
    rest = [put_q, put_k, put_v, functools.partial(put_gate, "dg"), functools.partial(put_gate, "gg")]
    rest += [functools.partial(put_merge, c0, step) for c0 in range(0, r["w_merge"].shape[1], step)]

    ga = jnp.dot(xn, r["w_gate"][...], preferred_element_type=F32).astype(BF16)
    z = jnp.dot(ga, r["wa"][...], preferred_element_type=F32) + r["ba"][...]
    la = _log_sigmoid(z) / GLA_TAU
    gq = seg("gq") * segs["gla_scale"]
    gk = seg("gk")
    gv = seg("gv").astype(BF16)
    if gla is None:
        r["gq"][...] = gq
        r["gk"][...] = gk
        r["gv"][...] = gv
        r["la"][...] = la
        for f in rest:
            f()
    else:
        c, nchunk, gheads, nlev = gla
        s_ref = refs[n_in + n_out]
        ti = pl.program_id(0) % seq_tiles

        @pl.when(ti == 0)
        def _():
            s_ref[...] = r["s0"][0]

        outs, s = _gla_block(c, nchunk, gheads, nlev, r["lt"][...], r["lev"][...], gq, gk, gv, la,
                             [s_ref[h] for h in range(gheads)], fillers=rest)
        for ci, o in enumerate(outs):
            r["ob"][ci * c:(ci + 1) * c, :] = o
        for h in range(gheads):
            s_ref[h] = s[h]

        @pl.when(ti == seq_tiles - 1)
        def _():
            r["sout"][0] = s_ref[...]


def _proj(x2, g_in, weights, b_alpha, segs, tm, heads, seq=None, gla_state=None):
    tok, d = x2.shape
    w_main, w_merge, w_gate, wkt, wa_pad = weights
    widths = {k: v[1] - v[0] for k, v in segs.items() if isinstance(v, tuple)}
    widths["mg"] = w_merge.shape[1]
    kw = wa_pad.shape[1]
    seq_tiles = None if seq is None else seq // tm

    def row_spec(w):
        return pl.BlockSpec((tm, w), lambda i: (i, 0))

    def full_spec(shape):
        return pl.BlockSpec(shape, lambda i: (0,) * len(shape))

    def const_spec(shape):
        return pl.BlockSpec(shape, lambda i: (0,) * len(shape), pipeline_mode=pl.Buffered(1))

    ins = [("x", x2, row_spec(d)), ("g", g_in.reshape(1, d), full_spec((1, d))),
           ("w_main", w_main, const_spec(w_main.shape)), ("w_merge", w_merge, const_spec(w_merge.shape)),
           ("w_gate", w_gate, const_spec(w_gate.shape)), ("wkt", wkt, const_spec(wkt.shape)),
           ("wa", wa_pad, full_spec(wa_pad.shape)), ("ba", b_alpha.reshape(1, kw), full_spec((1, kw)))]
    out_defs = [
        ("q", widths["dq"], BF16), ("kf", widths["dk"], F32), ("kb", widths["dk"], BF16),
        ("vf", widths["dv"], F32), ("vb", widths["dv"], BF16), ("dg", widths["dg"], BF16),
        ("gg", widths["gg"], BF16), ("mg", widths["mg"], BF16),
    ]
    gla, scratch = None, []
    if gla_state is None:
        out_defs += [("gq", widths["gq"], F32), ("gk", widths["gk"], F32), ("gv", widths["gv"], BF16),
                     ("la", kw, F32)]
    else:
        out_defs += [("ob", widths["gv"], F32)]
    out_specs = {nm: row_spec(w) for nm, w, _ in out_defs}
    out_shape = {nm: jax.ShapeDtypeStruct((tok, w), dt) for nm, w, dt in out_defs}
    names = [nm for nm, _, _ in out_defs]
    if seq is not None:
        nt = seq_tiles
        for nm, dt in (("kf", F32), ("kb", BF16)):
            out_specs[nm] = pl.BlockSpec((1, widths["dk"], tm), lambda i: (i // nt, 0, i % nt))
            out_shape[nm] = jax.ShapeDtypeStruct((tok // seq, widths["dk"], seq), dt)
        out_specs["vf"] = pl.BlockSpec((tm * heads, LANES), lambda i: (i, 0))
        out_shape["vf"] = jax.ShapeDtypeStruct((tok * heads, LANES), F32)
    if gla_state is not None:
        gheads, dk, dv = gla_state.shape[1:]
        nchunk = tm // CHUNK
        lt3, lev, nlev = _gla_tables(CHUNK, gheads, nchunk)
        gla = (CHUNK, nchunk, gheads, nlev)
        st = pl.BlockSpec((1, gheads, dk, dv), lambda i: (i // nt, 0, 0, 0))
        ins += [("lt", jnp.asarray(lt3, BF16), full_spec(lt3.shape)), ("lev", jnp.asarray(lev), full_spec(lev.shape)),
                ("s0", gla_state, st)]
        names.append("sout")
        out_specs["sout"] = st
        out_shape["sout"] = jax.ShapeDtypeStruct(gla_state.shape, F32)
        scratch = [pltpu.VMEM((gheads, dk, dv), F32)]
    in_names = [nm for nm, _, _ in ins]
    outs = pl.pallas_call(
        functools.partial(_proj_kernel, segs, heads, seq_tiles, gla, in_names, names),
        grid=(tok // tm,),
        in_specs=[sp for _, _, sp in ins],
        out_specs=[out_specs[nm] for nm in names],
        out_shape=[out_shape[nm] for nm in names],
        scratch_shapes=scratch,
        compiler_params=_cparams(("arbitrary",)),
        name="proj",
    )(*[a for _, a, _ in ins])
    return dict(zip(names, outs))


def _diff_lambda(lamv, lam_init):
    a = jnp.sum(lamv[0:1, :] * lamv[1:2, :], axis=1, keepdims=True)
    b = jnp.sum(lamv[2:3, :] * lamv[3:4, :], axis=1, keepdims=True)
    return jnp.exp(a) - jnp.exp(b) + lam_init


def _stack_maps(q):
    lane = lax.broadcasted_iota(jnp.int32, q.shape, 1)
    half = q.shape[1] // 2
    zero = jnp.zeros_like(q)
    return jnp.concatenate([jnp.where(lane < half, q, zero), jnp.where(lane >= half, q, zero)], axis=0)


def _attn_prompt_kernel(lam_init, tq, ra, rb, lamv_ref, q_ref, kt_ref, v_ref, o_ref,
                        qx_ref, acc_ref, s0_ref, s1_ref, m0_ref, m1_ref, a0_ref, a1_ref):
    ones = jnp.ones((tq, LANES), BF16)
    sbuf, mbuf, abuf = (s0_ref, s1_ref), (m0_ref, m1_ref), (a0_ref, a1_ref)
    nq = q_ref.shape[1] // tq
    hq = tq // 2
    assert hq % max(ra, rb) == 0

    def visible(row0):
        return hq if (row0 % tq) < hq else tq

    def values(j):
        start = pl.multiple_of(j * tq, tq)
        return jnp.concatenate([v_ref[0, pl.ds(start, tq), :], ones], axis=1)

    def own_a(i):
        kt = kt_ref[0, :, pl.ds(pl.multiple_of(i * tq, tq), tq)]
        for part in range(4):
            r0 = part * hq
            ncol = visible(r0)
            s = jnp.dot(qx_ref[r0:r0 + hq, :], kt[:, :ncol], preferred_element_type=F32)
            qc = (r0 % tq + lax.broadcasted_iota(jnp.int32, s.shape, 0)) // CHUNK
            kc = lax.broadcasted_iota(jnp.int32, s.shape, 1) // CHUNK
            s0_ref[r0:r0 + hq, :ncol] = jnp.where(kc <= qc, s, -jnp.inf)
        for r in range(2 * tq // ra):
            rows = slice(r * ra, (r + 1) * ra)
            m_cur = jnp.max(s0_ref[rows, :visible(r * ra)], axis=1, keepdims=True)
            m0_ref[rows, :] = jnp.broadcast_to(m_cur, (ra, LANES))

    def own_b(i):
        vx = values(i)
        for r in range(2 * tq // rb):
            rows = slice(r * rb, (r + 1) * rb)
            ncol = visible(r * rb)
            p = jnp.exp2(s0_ref[rows, :ncol] - jnp.tile(m0_ref[rows, :], (1, ncol // LANES)))
            acc_ref[rows, :] = jnp.dot(p.astype(BF16), vx[:ncol], preferred_element_type=F32)

    def stage_a(j, slot):
        kt = kt_ref[0, :, pl.ds(pl.multiple_of(j * tq, tq), tq)]
        sbuf[slot][...] = jnp.dot(qx_ref[...], kt, preferred_element_type=F32)
        for r in range(2 * tq // ra):
            rows = slice(r * ra, (r + 1) * ra)
            m_prev = mbuf[1 - slot][rows, :]
            m_new = jnp.maximum(m_prev, jnp.max(sbuf[slot][rows, :], axis=1, keepdims=True))
            abuf[slot][rows, :] = jnp.exp2(m_prev - m_new)
            mbuf[slot][rows, :] = m_new

    def stage_b(j, slot):
        vx = values(j)
        for r in range(2 * tq // rb):
            rows = slice(r * rb, (r + 1) * rb)
            p = jnp.exp2(sbuf[slot][rows, :] - jnp.tile(mbuf[slot][rows, :], (1, tq // LANES)))
            pv = jnp.dot(p.astype(BF16), vx, preferred_element_type=F32)
            acc_ref[rows, :] = acc_ref[rows, :] * jnp.tile(abuf[slot][rows, :], (1, 2)) + pv

    def q_tile(i, carry):
        qrows = pl.ds(pl.multiple_of(i * tq, tq), tq)
        qx_ref[...] = _stack_maps(q_ref[0, qrows, :])
        own_a(i)

        @pl.when(i == 0)
        def _():
            own_b(i)

        @pl.when(i > 0)
        def _():
            stage_a(0, 1)
            own_b(i)

            def pair(u, c):
                stage_a(2 * u + 1, 0)
                stage_b(2 * u, 1)
                stage_a(2 * u + 2, 1)
                stage_b(2 * u + 1, 0)
                return c

            lax.fori_loop(0, (i - 1) // 2, pair, 0)

            @pl.when(i % 2 == 1)
            def _():
                stage_b(i - 1, 1)

            @pl.when(i % 2 == 0)
            def _():
                stage_a(i - 1, 0)
                stage_b(i - 2, 1)
                stage_b(i - 1, 0)

        acc = acc_ref[...]
        o1 = acc[:tq, :LANES] / acc[:tq, LANES:]
        o2 = acc[tq:, :LANES] / acc[tq:, LANES:]
        o_ref[0, qrows, :] = o1 - _diff_lambda(lamv_ref[...], lam_init) * o2
        return carry

    lax.fori_loop(0, nq, q_tile, 0)


def _attn_prompt(q, kt, v, lamv, lam_init, heads):
    bsz, t, w = q.shape
    tq = min(1024, t)
    ra = min(128, tq)
    rb = min(256, tq)
    rowbuf = pltpu.VMEM((2 * tq, LANES), F32)
    return pl.pallas_call(
        functools.partial(_attn_prompt_kernel, lam_init, tq, ra, rb),
        grid=(bsz, heads),
        in_specs=[pl.BlockSpec(lamv.shape, lambda b, h: (0, 0)),
                  pl.BlockSpec((1, t, LANES), lambda b, h: (b, 0, h)),
                  pl.BlockSpec((1, LANES, t), lambda b, h: (b, h, 0)),
                  pl.BlockSpec((1, t, LANES), lambda b, h: (b, 0, h))],
        out_specs=pl.BlockSpec((1, t, LANES), lambda b, h: (b, 0, h)),
        out_shape=jax.ShapeDtypeStruct((bsz, t, w), F32),
        scratch_shapes=[pltpu.VMEM((2 * tq, LANES), BF16), pltpu.VMEM((2 * tq, 2 * LANES), F32),
                        pltpu.VMEM((2 * tq, tq), F32), pltpu.VMEM((2 * tq, tq), F32),
                        rowbuf, rowbuf, rowbuf, rowbuf],
        compiler_params=_cparams(("arbitrary", "arbitrary")),
        name="attn_prompt",
    )(lamv, q, kt, v)


def _attn_sample_kernel(lam_init, heads, lamv_ref, q_ref, ckt_ref, cv_ref, kn_ref, vn_ref, o_ref):
    nrow, tq = q_ref.shape[0], q_ref.shape[1]
    past = ckt_ref.shape[2]
    lam = _diff_lambda(lamv_ref[...], lam_init)
    for n in range(nrow):
        for h in range(heads):
            sl = slice(h * LANES, (h + 1) * LANES)
            qx = _stack_maps(q_ref[n, :, sl])
            kc = ckt_ref[n, sl, :].astype(BF16)
            vc = cv_ref[n, pl.ds(h, past, stride=heads), :].astype(BF16)
            kn = kn_ref[n, :, sl]
            vn = vn_ref[n, :, sl]
            sc = jnp.dot(qx, kc, preferred_element_type=F32)
            sn = lax.dot_general(qx, kn, _NT, preferred_element_type=F32)
            m = jnp.maximum(jnp.max(sc, axis=1, keepdims=True), jnp.max(sn, axis=1, keepdims=True))
            ec = jnp.exp2(sc - m)
            en = jnp.exp2(sn - m)
            l = jnp.sum(ec, axis=1, keepdims=True) + jnp.sum(en, axis=1, keepdims=True)
            o = (jnp.dot(ec.astype(BF16), vc, preferred_element_type=F32)
                 + jnp.dot(en.astype(BF16), vn, preferred_element_type=F32)) / l
            o_ref[n, :, sl] = o[:tq] - lam * o[tq:]


def _attn_sample(q, ckt, cv, kn, vn, lamv, lam_init, heads):
    bsz, tq, w = q.shape
    nrow = math.gcd(bsz, 2)
    rows = lambda a: pl.BlockSpec((nrow,) + a.shape[1:], lambda b: (b, 0, 0))
    return pl.pallas_call(
        functools.partial(_attn_sample_kernel, lam_init, heads),
        grid=(bsz // nrow,),
        in_specs=[pl.BlockSpec(lamv.shape, lambda b: (0, 0)), rows(q), rows(ckt), rows(cv), rows(kn), rows(vn)],
        out_specs=rows(q),
        out_shape=jax.ShapeDtypeStruct((bsz, tq, w), F32),
        compiler_params=_cparams(("arbitrary",)),
        name="attn_sample",
    )(lamv, q, ckt, cv, kn, vn)


def _gla_tables(c, heads, nchunk):
    nlev = int(math.log2(c))
    assert 1 << nlev == c and c % 8 == 0
    t = np.arange(c)[:, None]
    u = np.arange(c)[None, :]
    lev = np.where(t == u, 0, -1)
    for l in range(1, nlev + 1):
        n = c >> l
        same_block = (t // (2 * n)) == (u // (2 * n))
        lev = np.where(same_block & ((t % (2 * n)) >= n) & ((u % (2 * n)) < n), l, lev)
    lt = (u <= t).astype(np.float32)
    return np.concatenate([lt, lt, lt], axis=1), np.tile(lev, (nchunk, heads)).astype(np.int32), nlev


def _block_reference(b, n):
    rows, w = b.shape
    if n >= 8:
        parts = [jnp.broadcast_to(b[g + n - 1:g + n, :], (2 * n, w)) for g in range(0, rows, 2 * n)]
        return parts[0] if len(parts) == 1 else jnp.concatenate(parts, axis=0)
    b3 = b.reshape(rows // 8, 8, w)
    sub = lax.broadcasted_iota(jnp.int32, b3.shape, 1)
    pick = lambda j: jnp.broadcast_to(b3[:, j:j + 1, :], b3.shape)
    if n == 4:
        ref = pick(3)
    elif n == 2:
        ref = jnp.where(sub < 4, pick(1), pick(5))
    else:
        ref = jnp.where(sub < 2, pick(0), jnp.where(sub < 4, pick(2), jnp.where(sub < 6, pick(4), pick(6))))
    return ref.reshape(rows, w)


def _gla_block(c, nchunk, heads, nlev, lt, lev, q, k, v, la, s, fillers=(), independent=False):
    s_in = s
    fillers = list(fillers)
    fill = lambda: fillers.pop(0)() if fillers else None
    kw = q.shape[-1]
    vw = v.shape[-1]
    dk = kw // heads
    dv = vw // heads
    pad = LANES - c
    assert dv == LANES and pad >= 0
    chunk = lambda a, ci: a[ci * c:(ci + 1) * c]

    la2 = la * math.log2(math.e)
    p0 = la2.astype(BF16)
    r0 = la2 - p0.astype(F32)
    p1 = r0.astype(BF16)
    p2 = (r0 - p1.astype(F32)).astype(BF16)
    b = jnp.concatenate(
        [jnp.dot(lt, jnp.concatenate([chunk(p0, ci), chunk(p1, ci), chunk(p2, ci)], axis=0),
                 preferred_element_type=F32) for ci in range(nchunk)], axis=0)

    t = lax.broadcasted_iota(jnp.int32, b.shape, 0) % c
    klane_h = lax.broadcasted_iota(jnp.int32, (c, kw), 1) // dk
    vlane_h = lax.broadcasted_iota(jnp.int32, (c, vw), 1) // dv

    att = None
    for l in range(nlev + 1):
        if l == 0:
            qt, kt = q.astype(BF16), k.astype(BF16)
        else:
            e = jnp.exp2(-jnp.abs(b - _block_reference(b, c >> l)))
            qt = (q * e).astype(BF16)
            kt = (k * e).astype(BF16)
        parts = []
        for ci in range(nchunk):
            ktc = chunk(kt, ci)
            kstack = jnp.concatenate(
                [jnp.where(klane_h == h, ktc, jnp.zeros_like(ktc)) for h in range(heads)], axis=0)
            parts.append(lax.dot_general(chunk(qt, ci), kstack, _NT, preferred_element_type=F32))
        a = parts[0] if nchunk == 1 else jnp.concatenate(parts, axis=0)
        att = jnp.where(lev == l, a, 0.0 if att is None else att)
        fill()
    att = att.astype(BF16)
    qe = (q * jnp.exp2(b)).astype(BF16)

    o_intra, upd, decay = [], [], []
    for ci in range(nchunk):
        rows = slice(ci * c, (ci + 1) * c)
        vc = v[rows]
        vbd = jnp.concatenate(
            [jnp.where(vlane_h == h, vc, jnp.zeros_like(vc)) for h in range(heads)], axis=0)
        o_intra.append(jnp.dot(att[rows], vbd, preferred_element_type=F32))
        b_last = b[ci * c + c - 1:ci * c + c, :]
        kd = k[rows] * jnp.exp2(b_last - b[rows])
        x = jnp.concatenate([kd, jnp.zeros((pad, kw), F32), jnp.broadcast_to(b_last, (LANES, kw))], axis=0)
        xt = x.T
        for h in range(heads):
            hk = slice(h * dk, (h + 1) * dk)
            vpad = jnp.concatenate([vc[:, h * dv:(h + 1) * dv], jnp.zeros((pad, dv), BF16)], axis=0)
            upd.append(jnp.dot(xt[hk, :LANES].astype(BF16), vpad, preferred_element_type=F32))
            decay.append(jnp.exp2(xt[hk, LANES:]))
        fill()

    zero = jnp.zeros((dk, dv), BF16)
    outs, s_out = [], []
    for ci in range(nchunk):
        if independent:
            s = s_in[ci]
        sbd = jnp.concatenate(
            [jnp.concatenate([s[h].astype(BF16) if g == h else zero for g in range(heads)], axis=1)
             for h in range(heads)], axis=0)
        outs.append(o_intra[ci] + jnp.dot(chunk(qe, ci), sbd, preferred_element_type=F32))
        s = [decay[ci * heads + h] * s[h] + upd[ci * heads + h] for h in range(heads)]
        s_out.append(s)
    while fillers:
        fill()
    return outs, (s_out if independent else s)


def _gla_rows_kernel(c, nrow, heads, nlev, lt_ref, lev_ref, q_ref, k_ref, v_ref, la_ref, s0_ref,
                     o_ref, sout_ref):
    flat = lambda ref: ref[...].reshape(nrow * c, ref.shape[-1])
    states = [[s0_ref[n, h] for h in range(heads)] for n in range(nrow)]
    outs, s = _gla_block(c, nrow, heads, nlev, lt_ref[...], lev_ref[...], flat(q_ref), flat(k_ref), flat(v_ref),
                         flat(la_ref), states, independent=True)
    for n in range(nrow):
        o_ref[n] = outs[n]
        for h in range(heads):
            sout_ref[n, h] = s[n][h]


def _gla_rows(q, k, v, la, s0):
    bsz, c, kw = q.shape
    vw = v.shape[-1]
    heads, dk, dv = s0.shape[1:]
    nrow = math.gcd(bsz, 4)
    lt3, lev, nlev = _gla_tables(c, heads, nrow)
    tok = lambda w: pl.BlockSpec((nrow, c, w), lambda b: (b, 0, 0))
    st = pl.BlockSpec((nrow, heads, dk, dv), lambda b: (b, 0, 0, 0))
    return pl.pallas_call(
        functools.partial(_gla_rows_kernel, c, nrow, heads, nlev),
        grid=(bsz // nrow,),
        in_specs=[pl.BlockSpec(lt3.shape, lambda b: (0, 0)), pl.BlockSpec(lev.shape, lambda b: (0, 0)),
                  tok(kw), tok(kw), tok(vw), tok(kw), st],
        out_specs=[tok(vw), st],
        out_shape=[jax.ShapeDtypeStruct((bsz, c, vw), F32), jax.ShapeDtypeStruct(s0.shape, F32)],
        compiler_params=_cparams(("arbitrary",)),
        name="gla",
    )(jnp.asarray(lt3, BF16), jnp.asarray(lev), q, k, v, la, s0)


def _head_rmsnorm(x, g, heads):
    outs = []
    for h in range(heads):
        blk = x[:, h * LANES:(h + 1) * LANES]
        ms = jnp.mean(blk * blk, axis=-1, keepdims=True)
        outs.append((blk * lax.rsqrt(ms + EPS)) * g)
    return jnp.concatenate(outs, axis=1)


def _merge_kernel(lam_init, final, da_heads, gla_heads, x_ref, oa_ref, ob_ref, dg_ref, gg_ref, mg_ref,
                  ga_ref, gb_ref, wa_ref, wb_ref, wo_ref, gf_ref, y_ref):
    d = x_ref.shape[-1]

    def silu_of_half(h):
        return h * jnp.tanh(h) + h

    oa = _head_rmsnorm(oa_ref[...], ga_ref[...] * (1.0 - lam_init), da_heads) * silu_of_half(dg_ref[...].astype(F32))
    ob = _head_rmsnorm(ob_ref[...], gb_ref[...], gla_heads) * silu_of_half(gg_ref[...].astype(F32))
    ya = jnp.dot(oa.astype(BF16), wa_ref[...], preferred_element_type=F32)
    yb = jnp.dot(ob.astype(BF16), wb_ref[...], preferred_element_type=F32)
    ta = jnp.tanh(mg_ref[:, :d].astype(F32))
    tb = jnp.tanh(mg_ref[:, d:].astype(F32))
    mixed = 0.5 * ((ta * ya + ya) + (tb * yb + yb))
    y = x_ref[...] + jnp.dot(mixed.astype(BF16), wo_ref[...], preferred_element_type=F32)
    if final:
        ms = jnp.mean(y * y, axis=-1, keepdims=True)
        y = (y * lax.rsqrt(ms + EPS)) * gf_ref[...]
    y_ref[...] = y


def _merge(x2, oa, ob, dg, gg, mg, da_g, gla_g, wa, wb, wo, gf, lam_init, final, da_heads, gla_heads, tm):
    tok, d = x2.shape
    row = lambda w: pl.BlockSpec((tm, w), lambda i: (i, 0))
    full = lambda a: pl.BlockSpec(a.shape, lambda i: (0, 0))
    da_g = da_g.reshape(1, -1)
    gla_g = gla_g.reshape(1, -1)
    gf = gf.reshape(1, d)
    return pl.pallas_call(
        functools.partial(_merge_kernel, lam_init, final, da_heads, gla_heads),
        grid=(tok // tm,),
        in_specs=[row(d), row(oa.shape[1]), row(ob.shape[1]), row(dg.shape[1]), row(gg.shape[1]), row(mg.shape[1]),
                  full(da_g), full(gla_g), full(wa), full(wb), full(wo), full(gf)],
        out_specs=row(d),
        out_shape=jax.ShapeDtypeStruct((tok, d), F32),
        compiler_params=_cparams(("arbitrary",)),
        name="merge",
    )(x2, oa, ob, dg, gg, mg, da_g, gla_g, wa, wb, wo, gf)


def _layout(da_heads, da_dk, gla_heads, gla_dk, gla_dv):
    da_qk = da_heads * 2 * da_dk
    da_w = da_heads * 2 * da_dk
    kwid = gla_heads * gla_dk
    vwid = gla_heads * gla_dv
    names = ["dq", "dk", "dv", "dg", "gq", "gk", "gv", "gg"]
    sizes = [da_qk, da_qk, da_w, da_w, kwid, kwid, vwid, vwid]
    segs, off = {}, 0
    for nm, sz in zip(names, sizes):
        segs[nm] = (off, off + sz)
        off += sz
    segs["da_scale"] = da_dk ** -0.5 * math.log2(math.e)
    segs["gla_scale"] = gla_dk ** -0.5
    return segs, off


def kernel(x_prompt, x_sample, cache_k, cache_v, state_gla, norm_in_g, w_in, w_alpha_up, b_alpha,
           lambda_q1, lambda_k1, lambda_q2, lambda_k2, da_norm_g, gla_norm_g,
           w_branch_a, w_branch_b, w_out, norm_final_g):
    depth = w_in.shape[0]
    bsz, seq, d = x_prompt.shape
    dbsz, dseq, _ = x_sample.shape
    past = cache_k.shape[2]
    da_heads, da_dk = cache_k.shape[3], cache_k.shape[5]
    gla_heads, gla_dk, gla_dv = state_gla.shape[2:]
    rank = w_alpha_up.shape[1]
    kw = gla_heads * gla_dk
    assert 2 * da_dk == LANES and cache_v.shape[4] == LANES and gla_dv == LANES
    segs, ga_lo = _layout(da_heads, da_dk, gla_heads, gla_dk, gla_dv)

    xp = x_prompt.reshape(bsz * seq, d)
    xs = x_sample.reshape(dbsz * dseq, d)
    tm_p = min(512, bsz * seq)
    tm_s = min(512, dbsz * dseq)
    outs = [[] for _ in range(6)]
    for l in range(depth):
        lam_init = 0.8 - 0.6 * math.exp(-0.3 * l)
        final = l == depth - 1
        w = w_in[l]
        w_gate = jnp.pad(w[:, ga_lo:ga_lo + rank], ((0, 0), (0, LANES - rank))).astype(BF16)
        wkt = w[:, segs["dk"][0]:segs["dk"][1]].T.astype(BF16)
        wa_pad = jnp.pad(w_alpha_up[l], ((0, LANES - rank), (0, 0))).astype(BF16)
        weights = (w[:, :ga_lo].astype(BF16), w[:, ga_lo + rank:].astype(BF16), w_gate, wkt, wa_pad)
        lamv = jnp.stack([lambda_q1[l], lambda_k1[l], lambda_q2[l], lambda_k2[l]])
        wa, wb, wo = w_branch_a[l].astype(BF16), w_branch_b[l].astype(BF16), w_out[l].astype(BF16)

        p = _proj(xp, norm_in_g[l], weights, b_alpha[l], segs, tm_p, da_heads, seq=seq,
                  gla_state=jnp.zeros((bsz, gla_heads, gla_dk, gla_dv), F32))
        sh = lambda a: a.reshape(bsz, seq, a.shape[-1])
        oa = _attn_prompt(sh(p["q"]), p["kb"], sh(p["vb"]), lamv, lam_init, da_heads)
        s_p = p["sout"]
        xp = _merge(xp, oa.reshape(bsz * seq, -1), p["ob"], p["dg"], p["gg"], p["mg"],
                    da_norm_g[l], gla_norm_g[l], wa, wb, wo, norm_final_g, lam_init, final,
                    da_heads, gla_heads, tm_p)
        outs[0].append(jnp.transpose(p["kf"].reshape(bsz, da_heads, 2, da_dk, seq), (0, 4, 1, 2, 3)))
        outs[1].append(p["vf"].reshape(bsz, seq, da_heads, 2 * da_dk))
        outs[2].append(s_p)

        p = _proj(xs, norm_in_g[l], weights, b_alpha[l], segs, tm_s, da_heads)
        sh = lambda a: a.reshape(dbsz, dseq, a.shape[-1])
        ckt = jnp.transpose(cache_k[l], (0, 2, 3, 4, 1)).reshape(dbsz, da_heads * 2 * da_dk, past)
        oa = _attn_sample(sh(p["q"]), ckt, cache_v[l].reshape(dbsz, past * da_heads, 2 * da_dk),
                          sh(p["kb"]), sh(p["vb"]), lamv, lam_init, da_heads)
        ob, s_s = _gla_rows(sh(p["gq"]), sh(p["gk"]), sh(p["gv"]), sh(p["la"]), state_gla[l])
        xs = _merge(xs, oa.reshape(dbsz * dseq, -1), ob.reshape(dbsz * dseq, -1), p["dg"], p["gg"], p["mg"],
                    da_norm_g[l], gla_norm_g[l], wa, wb, wo, norm_final_g, lam_init, final,
                    da_heads, gla_heads, tm_s)
        outs[3].append(p["kf"].reshape(dbsz, dseq, da_heads, 2, da_dk))
        outs[4].append(p["vf"].reshape(dbsz, dseq, da_heads, 2 * da_dk))
        outs[5].append(s_s)

    return (xp.reshape(bsz, seq, d), xs.reshape(dbsz, dseq, d),
            jnp.stack(outs[0]), jnp.stack(outs[1]), jnp.stack(outs[2]),
            jnp.stack(outs[3]), jnp.stack(outs[4]), jnp.stack(outs[5]))
```

```python
import functools
import math

import numpy as np
import jax
import jax.numpy as jnp
from jax import lax
from jax.experimental import pallas as pl
from jax.experimental.pallas import tpu as pltpu

F32 = jnp.float32
BF16 = jnp.bfloat16

EPS = 1e-6
CHUNK = 64
GLA_TAU = 16.0
LANES = 128
VMEM_LIMIT_BYTES = 56 * 1024 * 1024

ROW_TILE = 512
PROJ_COL_STEP = 512
ATTN_TILE = 1024
ATTN_MAX_ROWS = 128
ATTN_PV_ROWS = 256
GLA_ROWS_PER_STEP = 4
ATTN_ROWS_PER_STEP = 2

_NT = (((1,), (1,)), ((), ()))


def _log_sigmoid(x):
    return jnp.minimum(x, 0.0) - jnp.log1p(jnp.exp(-jnp.abs(x)))


def _cparams(sem):
    return pltpu.CompilerParams(dimension_semantics=sem, vmem_limit_bytes=VMEM_LIMIT_BYTES)


def _proj_kernel(segs, heads, seq_tiles, gla, in_names, out_names, *refs):
    n_in, n_out = len(in_names), len(out_names)
    r = dict(zip(in_names + out_names, refs[:n_in + n_out]))
    x = r["x"][...]
    tm = x.shape[0]
    ms = jnp.mean(x * x, axis=-1, keepdims=True)
    xn = ((x * lax.rsqrt(ms + EPS)) * r["g"][...]).astype(BF16)

    def seg(name):
        lo, hi = segs[name]
        return jnp.dot(xn, r["w_main"][:, lo:hi], preferred_element_type=F32)

    def put_q():
        r["q"][...] = (seg("dq") * segs["da_scale"]).astype(BF16)

    def put_k():
        if seq_tiles is not None:
            kt = lax.dot_general(r["wkt"][...], xn, _NT, preferred_element_type=F32)
            r["kf"][0] = kt
            r["kb"][0] = kt.astype(BF16)
        else:
            k = seg("dk")
            r["kf"][...] = k
            r["kb"][...] = k.astype(BF16)

    def put_v():
        v = seg("dv")
        r["vb"][...] = v.astype(BF16)
        if seq_tiles is not None:
            for h in range(heads):
                r["vf"][pl.ds(h, tm, stride=heads), :] = v[:, h * LANES:(h + 1) * LANES]
        else:
            r["vf"][...] = v

    def put_gate(name):
        r[name][...] = (0.5 * seg(name)).astype(BF16)

    def put_merge(c0, step):
        r["mg"][:, c0:c0 + step] = (0.5 * jnp.dot(xn, r["w_merge"][:, c0:c0 + step],
                                                  preferred_element_type=F32)).astype(BF16)

    step = PROJ_COL_STEP
    rest = [put_q, put_k, put_v, functools.partial(put_gate, "dg"), functools.partial(put_gate, "gg")]
    rest += [functools.partial(put_merge, c0, step) for c0 in range(0, r["w_merge"].shape[1], step)]

    ga = jnp.dot(xn, r["w_gate"][...], preferred_element_type=F32).astype(BF16)
    z = jnp.dot(ga, r["wa"][...], preferred_element_type=F32) + r["ba"][...]
    la = _log_sigmoid(z) / GLA_TAU
    gq = seg("gq") * segs["gla_scale"]
    gk = seg("gk")
    gv = seg("gv").astype(BF16)
    if gla is None:
        r["gq"][...] = gq
        r["gk"][...] = gk
        r["gv"][...] = gv
        r["la"][...] = la
        for f in rest:
            f()
    else:
        c, nchunk, gheads, nlev = gla
        s_ref = refs[n_in + n_out]
        ti = pl.program_id(0) % seq_tiles

        @pl.when(ti == 0)
        def _():
            s_ref[...] = r["s0"][0]

        outs, s = _gla_block(c, nchunk, gheads, nlev, r["lt"][...], r["lev"][...], gq, gk, gv, la,
                             [s_ref[h] for h in range(gheads)], fillers=rest)
        for ci, o in enumerate(outs):
            r["ob"][ci * c:(ci + 1) * c, :] = o
        for h in range(gheads):
            s_ref[h] = s[h]

        @pl.when(ti == seq_tiles - 1)
        def _():
            r["sout"][0] = s_ref[...]


def _proj(x2, g_in, weights, b_alpha, segs, tm, heads, seq=None, gla_state=None):
    tok, d = x2.shape
    w_main, w_merge, w_gate, wkt, wa_pad = weights
    widths = {k: v[1] - v[0] for k, v in segs.items() if isinstance(v, tuple)}
    widths["mg"] = w_merge.shape[1]
    kw = wa_pad.shape[1]
    seq_tiles = None if seq is None else seq // tm

    def row_spec(w):
        return pl.BlockSpec((tm, w), lambda i: (i, 0))

    def full_spec(shape):
        return pl.BlockSpec(shape, lambda i: (0,) * len(shape))

    def const_spec(shape):
        return pl.BlockSpec(shape, lambda i: (0,) * len(shape), pipeline_mode=pl.Buffered(1))

    ins = [("x", x2, row_spec(d)), ("g", g_in.reshape(1, d), full_spec((1, d))),
           ("w_main", w_main, const_spec(w_main.shape)), ("w_merge", w_merge, const_spec(w_merge.shape)),
           ("w_gate", w_gate, const_spec(w_gate.shape)), ("wkt", wkt, const_spec(wkt.shape)),
           ("wa", wa_pad, full_spec(wa_pad.shape)), ("ba", b_alpha.reshape(1, kw), full_spec((1, kw)))]
    out_defs = [
        ("q", widths["dq"], BF16), ("kf", widths["dk"], F32), ("kb", widths["dk"], BF16),
        ("vf", widths["dv"], F32), ("vb", widths["dv"], BF16), ("dg", widths["dg"], BF16),
        ("gg", widths["gg"], BF16), ("mg", widths["mg"], BF16),
    ]
    gla, scratch = None, []
    if gla_state is None:
        out_defs += [("gq", widths["gq"], F32), ("gk", widths["gk"], F32), ("gv", widths["gv"], BF16),
                     ("la", kw, F32)]
    else:
        out_defs += [("ob", widths["gv"], F32)]
    out_specs = {nm: row_spec(w) for nm, w, _ in out_defs}
    out_shape = {nm: jax.ShapeDtypeStruct((tok, w), dt) for nm, w, dt in out_defs}
    names = [nm for nm, _, _ in out_defs]
    if seq is not None:
        nt = seq_tiles
        for nm, dt in (("kf", F32), ("kb", BF16)):
            out_specs[nm] = pl.BlockSpec((1, widths["dk"], tm), lambda i: (i // nt, 0, i % nt))
            out_shape[nm] = jax.ShapeDtypeStruct((tok // seq, widths["dk"], seq), dt)
        out_specs["vf"] = pl.BlockSpec((tm * heads, LANES), lambda i: (i, 0))
        out_shape["vf"] = jax.ShapeDtypeStruct((tok * heads, LANES), F32)
    if gla_state is not None:
        gheads, dk, dv = gla_state.shape[1:]
        nchunk = tm // CHUNK
        lt3, lev, nlev = _gla_tables(CHUNK, gheads, nchunk)
        gla = (CHUNK, nchunk, gheads, nlev)
        st = pl.BlockSpec((1, gheads, dk, dv), lambda i: (i // nt, 0, 0, 0))
        ins += [("lt", jnp.asarray(lt3, BF16), full_spec(lt3.shape)), ("lev", jnp.asarray(lev), full_spec(lev.shape)),
                ("s0", gla_state, st)]
        names.append("sout")
        out_specs["sout"] = st
        out_shape["sout"] = jax.ShapeDtypeStruct(gla_state.shape, F32)
        scratch = [pltpu.VMEM((gheads, dk, dv), F32)]
    in_names = [nm for nm, _, _ in ins]
    outs = pl.pallas_call(
        functools.partial(_proj_kernel, segs, heads, seq_tiles, gla, in_names, names),
        grid=(tok // tm,),
        in_specs=[sp for _, _, sp in ins],
        out_specs=[out_specs[nm] for nm in names],
        out_shape=[out_shape[nm] for nm in names],
        scratch_shapes=scratch,
        compiler_params=_cparams(("arbitrary",)),
        name="proj",
    )(*[a for _, a, _ in ins])
    return dict(zip(names, outs))


def _diff_lambda(lamv, lam_init):
    a = jnp.sum(lamv[0:1, :] * lamv[1:2, :], axis=1, keepdims=True)
    b = jnp.sum(lamv[2:3, :] * lamv[3:4, :], axis=1, keepdims=True)
    return jnp.exp(a) - jnp.exp(b) + lam_init


def _stack_maps(q):
    lane = lax.broadcasted_iota(jnp.int32, q.shape, 1)
    half = q.shape[1] // 2
    zero = jnp.zeros_like(q)
    return jnp.concatenate([jnp.where(lane < half, q, zero), jnp.where(lane >= half, q, zero)], axis=0)


def _attn_prompt_kernel(lam_init, tq, ra, rb, lamv_ref, q_ref, kt_ref, v_ref, o_ref,
                        qx_ref, acc_ref, s0_ref, s1_ref, m0_ref, m1_ref, a0_ref, a1_ref):
    ones = jnp.ones((tq, LANES), BF16)
    sbuf, mbuf, abuf = (s0_ref, s1_ref), (m0_ref, m1_ref), (a0_ref, a1_ref)
    nq = q_ref.shape[1] // tq
    hq = tq // 2
    assert hq % max(ra, rb) == 0

    def visible(row0):
        return hq if (row0 % tq) < hq else tq

    def values(j):
        start = pl.multiple_of(j * tq, tq)
        return jnp.concatenate([v_ref[0, pl.ds(start, tq), :], ones], axis=1)

    def own_a(i):
        kt = kt_ref[0, :, pl.ds(pl.multiple_of(i * tq, tq), tq)]
        for part in range(4):
            r0 = part * hq
            ncol = visible(r0)
            s = jnp.dot(qx_ref[r0:r0 + hq, :], kt[:, :ncol], preferred_element_type=F32)
            qc = (r0 % tq + lax.broadcasted_iota(jnp.int32, s.shape, 0)) // CHUNK
            kc = lax.broadcasted_iota(jnp.int32, s.shape, 1) // CHUNK
            s0_ref[r0:r0 + hq, :ncol] = jnp.where(kc <= qc, s, -jnp.inf)
        for r in range(2 * tq // ra):
            rows = slice(r * ra, (r + 1) * ra)
            m_cur = jnp.max(s0_ref[rows, :visible(r * ra)], axis=1, keepdims=True)
            m0_ref[rows, :] = jnp.broadcast_to(m_cur, (ra, LANES))

    def own_b(i):
        vx = values(i)
        for r in range(2 * tq // rb):
            rows = slice(r * rb, (r + 1) * rb)
            ncol = visible(r * rb)
            p = jnp.exp2(s0_ref[rows, :ncol] - jnp.tile(m0_ref[rows, :], (1, ncol // LANES)))
            acc_ref[rows, :] = jnp.dot(p.astype(BF16), vx[:ncol], preferred_element_type=F32)

    def stage_a(j, slot):
        kt = kt_ref[0, :, pl.ds(pl.multiple_of(j * tq, tq), tq)]
        sbuf[slot][...] = jnp.dot(qx_ref[...], kt, preferred_element_type=F32)
        for r in range(2 * tq // ra):
            rows = slice(r * ra, (r + 1) * ra)
            m_prev = mbuf[1 - slot][rows, :]
            m_new = jnp.maximum(m_prev, jnp.max(sbuf[slot][rows, :], axis=1, keepdims=True))
            abuf[slot][rows, :] = jnp.exp2(m_prev - m_new)
            mbuf[slot][rows, :] = m_new

    def stage_b(j, slot):
        vx = values(j)
        for r in range(2 * tq // rb):
            rows = slice(r * rb, (r + 1) * rb)
            p = jnp.exp2(sbuf[slot][rows, :] - jnp.tile(mbuf[slot][rows, :], (1, tq // LANES)))
            pv = jnp.dot(p.astype(BF16), vx, preferred_element_type=F32)
            acc_ref[rows, :] = acc_ref[rows, :] * jnp.tile(abuf[slot][rows, :], (1, 2)) + pv

    def q_tile(i, carry):
        qrows = pl.ds(pl.multiple_of(i * tq, tq), tq)
        qx_ref[...] = _stack_maps(q_ref[0, qrows, :])
        own_a(i)

        @pl.when(i == 0)
        def _():
            own_b(i)

        @pl.when(i > 0)
        def _():
            stage_a(0, 1)
            own_b(i)

            def pair(u, c):
                stage_a(2 * u + 1, 0)
                stage_b(2 * u, 1)
                stage_a(2 * u + 2, 1)
                stage_b(2 * u + 1, 0)
                return c

            lax.fori_loop(0, (i - 1) // 2, pair, 0)

            @pl.when(i % 2 == 1)
            def _():
                stage_b(i - 1, 1)

            @pl.when(i % 2 == 0)
            def _():
                stage_a(i - 1, 0)
                stage_b(i - 2, 1)
                stage_b(i - 1, 0)

        acc = acc_ref[...]
        o1 = acc[:tq, :LANES] / acc[:tq, LANES:]
        o2 = acc[tq:, :LANES] / acc[tq:, LANES:]
        o_ref[0, qrows, :] = o1 - _diff_lambda(lamv_ref[...], lam_init) * o2
        return carry

    lax.fori_loop(0, nq, q_tile, 0)


def _attn_prompt(q, kt, v, lamv, lam_init, heads):
    bsz, t, w = q.shape
    tq = min(ATTN_TILE, t)
    ra = min(ATTN_MAX_ROWS, tq)
    rb = min(ATTN_PV_ROWS, tq)
    rowbuf = pltpu.VMEM((2 * tq, LANES), F32)
    return pl.pallas_call(
        functools.partial(_attn_prompt_kernel, lam_init, tq, ra, rb),
        grid=(bsz, heads),
        in_specs=[pl.BlockSpec(lamv.shape, lambda b, h: (0, 0)),
                  pl.BlockSpec((1, t, LANES), lambda b, h: (b, 0, h)),
                  pl.BlockSpec((1, LANES, t), lambda b, h: (b, h, 0)),
                  pl.BlockSpec((1, t, LANES), lambda b, h: (b, 0, h))],
        out_specs=pl.BlockSpec((1, t, LANES), lambda b, h: (b, 0, h)),
        out_shape=jax.ShapeDtypeStruct((bsz, t, w), F32),
        scratch_shapes=[pltpu.VMEM((2 * tq, LANES), BF16), pltpu.VMEM((2 * tq, 2 * LANES), F32),
                        pltpu.VMEM((2 * tq, tq), F32), pltpu.VMEM((2 * tq, tq), F32),
                        rowbuf, rowbuf, rowbuf, rowbuf],
        compiler_params=_cparams(("arbitrary", "arbitrary")),
        name="attn_prompt",
    )(lamv, q, kt, v)


def _attn_sample_kernel(lam_init, heads, lamv_ref, q_ref, ckt_ref, cv_ref, kn_ref, vn_ref, o_ref):
    nrow, tq = q_ref.shape[0], q_ref.shape[1]
    past = ckt_ref.shape[2]
    lam = _diff_lambda(lamv_ref[...], lam_init)
    for n in range(nrow):
        for h in range(heads):
            sl = slice(h * LANES, (h + 1) * LANES)
            qx = _stack_maps(q_ref[n, :, sl])
            kc = ckt_ref[n, sl, :].astype(BF16)
            vc = cv_ref[n, pl.ds(h, past, stride=heads), :].astype(BF16)
            kn = kn_ref[n, :, sl]
            vn = vn_ref[n, :, sl]
            sc = jnp.dot(qx, kc, preferred_element_type=F32)
            sn = lax.dot_general(qx, kn, _NT, preferred_element_type=F32)
            m = jnp.maximum(jnp.max(sc, axis=1, keepdims=True), jnp.max(sn, axis=1, keepdims=True))
            ec = jnp.exp2(sc - m)
            en = jnp.exp2(sn - m)
            l = jnp.sum(ec, axis=1, keepdims=True) + jnp.sum(en, axis=1, keepdims=True)
            o = (jnp.dot(ec.astype(BF16), vc, preferred_element_type=F32)
                 + jnp.dot(en.astype(BF16), vn, preferred_element_type=F32)) / l
            o_ref[n, :, sl] = o[:tq] - lam * o[tq:]


def _attn_sample(q, ckt, cv, kn, vn, lamv, lam_init, heads):
    bsz, tq, w = q.shape
    nrow = math.gcd(bsz, ATTN_ROWS_PER_STEP)
    rows = lambda a: pl.BlockSpec((nrow,) + a.shape[1:], lambda b: (b, 0, 0))
    return pl.pallas_call(
        functools.partial(_attn_sample_kernel, lam_init, heads),
        grid=(bsz // nrow,),
        in_specs=[pl.BlockSpec(lamv.shape, lambda b: (0, 0)), rows(q), rows(ckt), rows(cv), rows(kn), rows(vn)],
        out_specs=rows(q),
        out_shape=jax.ShapeDtypeStruct((bsz, tq, w), F32),
        compiler_params=_cparams(("arbitrary",)),
        name="attn_sample",
    )(lamv, q, ckt, cv, kn, vn)


def _gla_tables(c, heads, nchunk):
    nlev = int(math.log2(c))
    assert 1 << nlev == c and c % 8 == 0
    t = np.arange(c)[:, None]
    u = np.arange(c)[None, :]
    lev = np.where(t == u, 0, -1)
    for l in range(1, nlev + 1):
        n = c >> l
        same_block = (t // (2 * n)) == (u // (2 * n))
        lev = np.where(same_block & ((t % (2 * n)) >= n) & ((u % (2 * n)) < n), l, lev)
    lt = (u <= t).astype(np.float32)
    return np.concatenate([lt, lt, lt], axis=1), np.tile(lev, (nchunk, heads)).astype(np.int32), nlev


def _block_reference(b, n):
    rows, w = b.shape
    if n >= 8:
        parts = [jnp.broadcast_to(b[g + n - 1:g + n, :], (2 * n, w)) for g in range(0, rows, 2 * n)]
        return parts[0] if len(parts) == 1 else jnp.concatenate(parts, axis=0)
    b3 = b.reshape(rows // 8, 8, w)
    sub = lax.broadcasted_iota(jnp.int32, b3.shape, 1)
    pick = lambda j: jnp.broadcast_to(b3[:, j:j + 1, :], b3.shape)
    if n == 4:
        ref = pick(3)
    elif n == 2:
        ref = jnp.where(sub < 4, pick(1), pick(5))
    else:
        ref = jnp.where(sub < 2, pick(0), jnp.where(sub < 4, pick(2), jnp.where(sub < 6, pick(4), pick(6))))
    return ref.reshape(rows, w)


def _gla_block(c, nchunk, heads, nlev, lt, lev, q, k, v, la, s, fillers=(), independent=False):
    s_in = s
    fillers = list(fillers)
    fill = lambda: fillers.pop(0)() if fillers else None
    kw = q.shape[-1]
    vw = v.shape[-1]
    dk = kw // heads
    dv = vw // heads
    pad = LANES - c
    assert dv == LANES and pad >= 0
    chunk = lambda a, ci: a[ci * c:(ci + 1) * c]

    la2 = la * math.log2(math.e)
    p0 = la2.astype(BF16)
    r0 = la2 - p0.astype(F32)
    p1 = r0.astype(BF16)
    p2 = (r0 - p1.astype(F32)).astype(BF16)
    b = jnp.concatenate(
        [jnp.dot(lt, jnp.concatenate([chunk(p0, ci), chunk(p1, ci), chunk(p2, ci)], axis=0),
                 preferred_element_type=F32) for ci in range(nchunk)], axis=0)

    t = lax.broadcasted_iota(jnp.int32, b.shape, 0) % c
    klane_h = lax.broadcasted_iota(jnp.int32, (c, kw), 1) // dk
    vlane_h = lax.broadcasted_iota(jnp.int32, (c, vw), 1) // dv

    att = None
    for l in range(nlev + 1):
        if l == 0:
            qt, kt = q.astype(BF16), k.astype(BF16)
        else:
            e = jnp.exp2(-jnp.abs(b - _block_reference(b, c >> l)))
            qt = (q * e).astype(BF16)
            kt = (k * e).astype(BF16)
        parts = []
        for ci in range(nchunk):
            ktc = chunk(kt, ci)
            kstack = jnp.concatenate(
                [jnp.where(klane_h == h, ktc, jnp.zeros_like(ktc)) for h in range(heads)], axis=0)
            parts.append(lax.dot_general(chunk(qt, ci), kstack, _NT, preferred_element_type=F32))
        a = parts[0] if nchunk == 1 else jnp.concatenate(parts, axis=0)
        att = jnp.where(lev == l, a, 0.0 if att is None else att)
        fill()
    att = att.astype(BF16)
    qe = (q * jnp.exp2(b)).astype(BF16)

    o_intra, upd, decay = [], [], []
    for ci in range(nchunk):
        rows = slice(ci * c, (ci + 1) * c)
        vc = v[rows]
        vbd = jnp.concatenate(
            [jnp.where(vlane_h == h, vc, jnp.zeros_like(vc)) for h in range(heads)], axis=0)
        o_intra.append(jnp.dot(att[rows], vbd, preferred_element_type=F32))
        b_last = b[ci * c + c - 1:ci * c + c, :]
        kd = k[rows] * jnp.exp2(b_last - b[rows])
        x = jnp.concatenate([kd, jnp.zeros((pad, kw), F32), jnp.broadcast_to(b_last, (LANES, kw))], axis=0)
        xt = x.T
        for h in range(heads):
            hk = slice(h * dk, (h + 1) * dk)
            vpad = jnp.concatenate([vc[:, h * dv:(h + 1) * dv], jnp.zeros((pad, dv), BF16)], axis=0)
            upd.append(jnp.dot(xt[hk, :LANES].astype(BF16), vpad, preferred_element_type=F32))
            decay.append(jnp.exp2(xt[hk, LANES:]))
        fill()

    zero = jnp.zeros((dk, dv), BF16)
    outs, s_out = [], []
    for ci in range(nchunk):
        if independent:
            s = s_in[ci]
        sbd = jnp.concatenate(
            [jnp.concatenate([s[h].astype(BF16) if g == h else zero for g in range(heads)], axis=1)
             for h in range(heads)], axis=0)
        outs.append(o_intra[ci] + jnp.dot(chunk(qe, ci), sbd, preferred_element_type=F32))
        s = [decay[ci * heads + h] * s[h] + upd[ci * heads + h] for h in range(heads)]
        s_out.append(s)
    while fillers:
        fill()
    return outs, (s_out if independent else s)


def _gla_rows_kernel(c, nrow, heads, nlev, lt_ref, lev_ref, q_ref, k_ref, v_ref, la_ref, s0_ref,
                     o_ref, sout_ref):
    flat = lambda ref: ref[...].reshape(nrow * c, ref.shape[-1])
    states = [[s0_ref[n, h] for h in range(heads)] for n in range(nrow)]
    outs, s = _gla_block(c, nrow, heads, nlev, lt_ref[...], lev_ref[...], flat(q_ref), flat(k_ref), flat(v_ref),
                         flat(la_ref), states, independent=True)
    for n in range(nrow):
        o_ref[n] = outs[n]
        for h in range(heads):
            sout_ref[n, h] = s[n][h]


def _gla_rows(q, k, v, la, s0):
    bsz, c, kw = q.shape
    vw = v.shape[-1]
    heads, dk, dv = s0.shape[1:]
    nrow = math.gcd(bsz, GLA_ROWS_PER_STEP)
    lt3, lev, nlev = _gla_tables(c, heads, nrow)
    tok = lambda w: pl.BlockSpec((nrow, c, w), lambda b: (b, 0, 0))
    st = pl.BlockSpec((nrow, heads, dk, dv), lambda b: (b, 0, 0, 0))
    return pl.pallas_call(
        functools.partial(_gla_rows_kernel, c, nrow, heads, nlev),
        grid=(bsz // nrow,),
        in_specs=[pl.BlockSpec(lt3.shape, lambda b: (0, 0)), pl.BlockSpec(lev.shape, lambda b: (0, 0)),
                  tok(kw), tok(kw), tok(vw), tok(kw), st],
        out_specs=[tok(vw), st],
        out_shape=[jax.ShapeDtypeStruct((bsz, c, vw), F32), jax.ShapeDtypeStruct(s0.shape, F32)],
        compiler_params=_cparams(("arbitrary",)),
        name="gla",
    )(jnp.asarray(lt3, BF16), jnp.asarray(lev), q, k, v, la, s0)


def _head_rmsnorm(x, g, heads):
    outs = []
    for h in range(heads):
        blk = x[:, h * LANES:(h + 1) * LANES]
        ms = jnp.mean(blk * blk, axis=-1, keepdims=True)
        outs.append((blk * lax.rsqrt(ms + EPS)) * g)
    return jnp.concatenate(outs, axis=1)


def _merge_kernel(lam_init, final, da_heads, gla_heads, x_ref, oa_ref, ob_ref, dg_ref, gg_ref, mg_ref,
                  ga_ref, gb_ref, wa_ref, wb_ref, wo_ref, gf_ref, y_ref):
    d = x_ref.shape[-1]

    def silu_of_half(h):
        return h * jnp.tanh(h) + h

    oa = _head_rmsnorm(oa_ref[...], ga_ref[...] * (1.0 - lam_init), da_heads) * silu_of_half(dg_ref[...].astype(F32))
    ob = _head_rmsnorm(ob_ref[...], gb_ref[...], gla_heads) * silu_of_half(gg_ref[...].astype(F32))
    ya = jnp.dot(oa.astype(BF16), wa_ref[...], preferred_element_type=F32)
    yb = jnp.dot(ob.astype(BF16), wb_ref[...], preferred_element_type=F32)
    ta = jnp.tanh(mg_ref[:, :d].astype(F32))
    tb = jnp.tanh(mg_ref[:, d:].astype(F32))
    mixed = 0.5 * ((ta * ya + ya) + (tb * yb + yb))
    y = x_ref[...] + jnp.dot(mixed.astype(BF16), wo_ref[...], preferred_element_type=F32)
    if final:
        ms = jnp.mean(y * y, axis=-1, keepdims=True)
        y = (y * lax.rsqrt(ms + EPS)) * gf_ref[...]
    y_ref[...] = y


def _merge(x2, oa, ob, dg, gg, mg, da_g, gla_g, wa, wb, wo, gf, lam_init, final, da_heads, gla_heads, tm):
    tok, d = x2.shape
    row = lambda w: pl.BlockSpec((tm, w), lambda i: (i, 0))
    full = lambda a: pl.BlockSpec(a.shape, lambda i: (0, 0))
    da_g = da_g.reshape(1, -1)
    gla_g = gla_g.reshape(1, -1)
    gf = gf.reshape(1, d)
    return pl.pallas_call(
        functools.partial(_merge_kernel, lam_init, final, da_heads, gla_heads),
        grid=(tok // tm,),
        in_specs=[row(d), row(oa.shape[1]), row(ob.shape[1]), row(dg.shape[1]), row(gg.shape[1]), row(mg.shape[1]),
                  full(da_g), full(gla_g), full(wa), full(wb), full(wo), full(gf)],
        out_specs=row(d),
        out_shape=jax.ShapeDtypeStruct((tok, d), F32),
        compiler_params=_cparams(("arbitrary",)),
        name="merge",
    )(x2, oa, ob, dg, gg, mg, da_g, gla_g, wa, wb, wo, gf)


def _layout(da_heads, da_dk, gla_heads, gla_dk, gla_dv):
    da_qk = da_heads * 2 * da_dk
    da_w = da_heads * 2 * da_dk
    kwid = gla_heads * gla_dk
    vwid = gla_heads * gla_dv
    names = ["dq", "dk", "dv", "dg", "gq", "gk", "gv", "gg"]
    sizes = [da_qk, da_qk, da_w, da_w, kwid, kwid, vwid, vwid]
    segs, off = {}, 0
    for nm, sz in zip(names, sizes):
        segs[nm] = (off, off + sz)
        off += sz
    segs["da_scale"] = da_dk ** -0.5 * math.log2(math.e)
    segs["gla_scale"] = gla_dk ** -0.5
    return segs, off


def kernel(x_prompt, x_sample, cache_k, cache_v, state_gla, norm_in_g, w_in, w_alpha_up, b_alpha,
           lambda_q1, lambda_k1, lambda_q2, lambda_k2, da_norm_g, gla_norm_g,
           w_branch_a, w_branch_b, w_out, norm_final_g):
    depth = w_in.shape[0]
    bsz, seq, d = x_prompt.shape
    dbsz, dseq, _ = x_sample.shape
    past = cache_k.shape[2]
    da_heads, da_dk = cache_k.shape[3], cache_k.shape[5]
    gla_heads, gla_dk, gla_dv = state_gla.shape[2:]
    rank = w_alpha_up.shape[1]
    kw = gla_heads * gla_dk
    assert 2 * da_dk == LANES and cache_v.shape[4] == LANES and gla_dv == LANES
    segs, ga_lo = _layout(da_heads, da_dk, gla_heads, gla_dk, gla_dv)

    xp = x_prompt.reshape(bsz * seq, d)
    xs = x_sample.reshape(dbsz * dseq, d)
    tm_p = min(ROW_TILE, bsz * seq)
    tm_s = min(ROW_TILE, dbsz * dseq)
    outs = [[] for _ in range(6)]
    for l in range(depth):
        lam_init = 0.8 - 0.6 * math.exp(-0.3 * l)
        final = l == depth - 1
        w = w_in[l]
        w_gate = jnp.pad(w[:, ga_lo:ga_lo + rank], ((0, 0), (0, LANES - rank))).astype(BF16)
        wkt = w[:, segs["dk"][0]:segs["dk"][1]].T.astype(BF16)
        wa_pad = jnp.pad(w_alpha_up[l], ((0, LANES - rank), (0, 0))).astype(BF16)
        weights = (w[:, :ga_lo].astype(BF16), w[:, ga_lo + rank:].astype(BF16), w_gate, wkt, wa_pad)
        lamv = jnp.stack([lambda_q1[l], lambda_k1[l], lambda_q2[l], lambda_k2[l]])
        wa, wb, wo = w_branch_a[l].astype(BF16), w_branch_b[l].astype(BF16), w_out[l].astype(BF16)

        p = _proj(xp, norm_in_g[l], weights, b_alpha[l], segs, tm_p, da_heads, seq=seq,
                  gla_state=jnp.zeros((bsz, gla_heads, gla_dk, gla_dv), F32))
        sh = lambda a: a.reshape(bsz, seq, a.shape[-1])
        oa = _attn_prompt(sh(p["q"]), p["kb"], sh(p["vb"]), lamv, lam_init, da_heads)
        s_p = p["sout"]
        xp = _merge(xp, oa.reshape(bsz * seq, -1), p["ob"], p["dg"], p["gg"], p["mg"],
                    da_norm_g[l], gla_norm_g[l], wa, wb, wo, norm_final_g, lam_init, final,
                    da_heads, gla_heads, tm_p)
        outs[0].append(jnp.transpose(p["kf"].reshape(bsz, da_heads, 2, da_dk, seq), (0, 4, 1, 2, 3)))
        outs[1].append(p["vf"].reshape(bsz, seq, da_heads, 2 * da_dk))
        outs[2].append(s_p)

        p = _proj(xs, norm_in_g[l], weights, b_alpha[l], segs, tm_s, da_heads)
        sh = lambda a: a.reshape(dbsz, dseq, a.shape[-1])
        ckt = jnp.transpose(cache_k[l], (0, 2, 3, 4, 1)).reshape(dbsz, da_heads * 2 * da_dk, past)
        oa = _attn_sample(sh(p["q"]), ckt, cache_v[l].reshape(dbsz, past * da_heads, 2 * da_dk),
                          sh(p["kb"]), sh(p["vb"]), lamv, lam_init, da_heads)
        ob, s_s = _gla_rows(sh(p["gq"]), sh(p["gk"]), sh(p["gv"]), sh(p["la"]), state_gla[l])
        xs = _merge(xs, oa.reshape(dbsz * dseq, -1), ob.reshape(dbsz * dseq, -1), p["dg"], p["gg"], p["mg"],
                    da_norm_g[l], gla_norm_g[l], wa, wb, wo, norm_final_g, lam_init, final,
                    da_heads, gla_heads, tm_s)
        outs[3].append(p["kf"].reshape(dbsz, dseq, da_heads, 2, da_dk))
        outs[4].append(p["vf"].reshape(dbsz, dseq, da_heads, 2 * da_dk))
        outs[5].append(s_s)

    return (xp.reshape(bsz, seq, d), xs.reshape(dbsz, dseq, d),
            jnp.stack(outs[0]), jnp.stack(outs[1]), jnp.stack(outs[2]),
            jnp.stack(outs[3]), jnp.stack(outs[4]), jnp.stack(outs[5]))
```

```python
import functools
import math

import numpy as np
import jax
import jax.numpy as jnp
from jax import lax
from jax.experimental import pallas as pl
from jax.experimental.pallas import tpu as pltpu

F32 = jnp.float32
BF16 = jnp.bfloat16

EPS = 1e-6
CHUNK = 64
GLA_TAU = 16.0
LANES = 128
VMEM_LIMIT_BYTES = 56 * 1024 * 1024

ROW_TILE = 512
PROJ_COL_STEP = 512
ATTN_TILE = 1024
ATTN_MAX_ROWS = 128
ATTN_PV_ROWS = 256
ATTN_ROWS_PER_STEP = 2

_NT = (((1,), (1,)), ((), ()))


def _log_sigmoid(x):
    return jnp.minimum(x, 0.0) - jnp.log1p(jnp.exp(-jnp.abs(x)))


def _cparams(sem):
    return pltpu.CompilerParams(dimension_semantics=sem, vmem_limit_bytes=VMEM_LIMIT_BYTES)


def _proj_kernel(segs, heads, seq_tiles, gla, in_names, out_names, *refs):
    n_in, n_out = len(in_names), len(out_names)
    r = dict(zip(in_names + out_names, refs[:n_in + n_out]))
    x = r["x"][...]
    tm = x.shape[0]
    ms = jnp.mean(x * x, axis=-1, keepdims=True)
    xn = ((x * lax.rsqrt(ms + EPS)) * r["g"][...]).astype(BF16)

    def seg(name):
        lo, hi = segs[name]
        return jnp.dot(xn, r["w_main"][:, lo:hi], preferred_element_type=F32)

    def put_q():
        r["q"][...] = (seg("dq") * segs["da_scale"]).astype(BF16)

    def put_k():
        if seq_tiles is not None:
            kt = lax.dot_general(r["wkt"][...], xn, _NT, preferred_element_type=F32)
            r["kf"][0] = kt
            r["kb"][0] = kt.astype(BF16)
        else:
            k = seg("dk")
            r["kf"][...] = k
            r["kb"][...] = k.astype(BF16)

    def put_v():
        v = seg("dv")
        r["vb"][...] = v.astype(BF16)
        if seq_tiles is not None:
            for h in range(heads):
                r["vf"][pl.ds(h, tm, stride=heads), :] = v[:, h * LANES:(h + 1) * LANES]
        else:
            r["vf"][...] = v

    def put_gate(name):
        r[name][...] = (0.5 * seg(name)).astype(BF16)

    def put_merge(c0, step):
        r["mg"][:, c0:c0 + step] = (0.5 * jnp.dot(xn, r["w_merge"][:, c0:c0 + step],
                                                  preferred_element_type=F32)).astype(BF16)

    step = PROJ_COL_STEP
    rest = [put_q, put_k, put_v, functools.partial(put_gate, "dg"), functools.partial(put_gate, "gg")]
    rest += [functools.partial(put_merge, c0, step) for c0 in range(0, r["w_merge"].shape[1], step)]

    ga = jnp.dot(xn, r["w_gate"][...], preferred_element_type=F32).astype(BF16)
    z = jnp.dot(ga, r["wa"][...], preferred_element_type=F32) + r["ba"][...]
    la = _log_sigmoid(z) / GLA_TAU
    gq = seg("gq") * segs["gla_scale"]
    gk = seg("gk")
    gv = seg("gv").astype(BF16)
    if seq_tiles is None:
        c, nchunk, gheads, nlev = gla
        states = [[r["s0"][n, h] for h in range(gheads)] for n in range(nchunk)]
        outs, s = _gla_block(c, nchunk, gheads, nlev, r["lt"][...], r["lev"][...], gq, gk, gv, la,
                             states, fillers=rest, independent=True)
        for n, o in enumerate(outs):
            r["ob"][n * c:(n + 1) * c, :] = o
            for h in range(gheads):
                r["sout"][n, h] = s[n][h]
    else:
        c, nchunk, gheads, nlev = gla
        s_ref = refs[n_in + n_out]
        ti = pl.program_id(0) % seq_tiles

        @pl.when(ti == 0)
        def _():
            s_ref[...] = r["s0"][0]

        outs, s = _gla_block(c, nchunk, gheads, nlev, r["lt"][...], r["lev"][...], gq, gk, gv, la,
                             [s_ref[h] for h in range(gheads)], fillers=rest)
        for ci, o in enumerate(outs):
            r["ob"][ci * c:(ci + 1) * c, :] = o
        for h in range(gheads):
            s_ref[h] = s[h]

        @pl.when(ti == seq_tiles - 1)
        def _():
            r["sout"][0] = s_ref[...]


def _proj(x2, g_in, weights, b_alpha, segs, tm, heads, gla_state, seq=None):
    tok, d = x2.shape
    w_main, w_merge, w_gate, wkt, wa_pad = weights
    widths = {k: v[1] - v[0] for k, v in segs.items() if isinstance(v, tuple)}
    widths["mg"] = w_merge.shape[1]
    kw = wa_pad.shape[1]
    gheads, dk, dv = gla_state.shape[1:]
    seq_tiles = None if seq is None else seq // tm
    chunk = CHUNK if seq is not None else tok // gla_state.shape[0]
    nchunk = tm // chunk
    lt3, lev, nlev = _gla_tables(chunk, gheads, nchunk)

    def row_spec(w):
        return pl.BlockSpec((tm, w), lambda i: (i, 0))

    def full_spec(shape):
        return pl.BlockSpec(shape, lambda i: (0,) * len(shape))

    def const_spec(shape):
        return pl.BlockSpec(shape, lambda i: (0,) * len(shape), pipeline_mode=pl.Buffered(1))

    if seq is not None:
        st = pl.BlockSpec((1, gheads, dk, dv), lambda i: (i // seq_tiles, 0, 0, 0))
        scratch = [pltpu.VMEM((gheads, dk, dv), F32)]
    else:
        st = pl.BlockSpec((nchunk, gheads, dk, dv), lambda i: (i, 0, 0, 0))
        scratch = []
    ins = [("x", x2, row_spec(d)), ("g", g_in.reshape(1, d), full_spec((1, d))),
           ("w_main", w_main, const_spec(w_main.shape)), ("w_merge", w_merge, const_spec(w_merge.shape)),
           ("w_gate", w_gate, const_spec(w_gate.shape)), ("wkt", wkt, const_spec(wkt.shape)),
           ("wa", wa_pad, full_spec(wa_pad.shape)), ("ba", b_alpha.reshape(1, kw), full_spec((1, kw))),
           ("lt", jnp.asarray(lt3, BF16), full_spec(lt3.shape)), ("lev", jnp.asarray(lev), full_spec(lev.shape)),
           ("s0", gla_state, st)]
    out_defs = [
        ("q", widths["dq"], BF16), ("kf", widths["dk"], F32), ("kb", widths["dk"], BF16),
        ("vf", widths["dv"], F32), ("vb", widths["dv"], BF16), ("dg", widths["dg"], BF16),
        ("gg", widths["gg"], BF16), ("mg", widths["mg"], BF16), ("ob", widths["gv"], F32),
    ]
    out_specs = {nm: row_spec(w) for nm, w, _ in out_defs}
    out_shape = {nm: jax.ShapeDtypeStruct((tok, w), dt) for nm, w, dt in out_defs}
    names = [nm for nm, _, _ in out_defs] + ["sout"]
    out_specs["sout"] = st
    out_shape["sout"] = jax.ShapeDtypeStruct(gla_state.shape, F32)
    if seq is not None:
        for nm, dt in (("kf", F32), ("kb", BF16)):
            out_specs[nm] = pl.BlockSpec((1, widths["dk"], tm), lambda i: (i // seq_tiles, 0, i % seq_tiles))
            out_shape[nm] = jax.ShapeDtypeStruct((tok // seq, widths["dk"], seq), dt)
        out_specs["vf"] = pl.BlockSpec((tm * heads, LANES), lambda i: (i, 0))
        out_shape["vf"] = jax.ShapeDtypeStruct((tok * heads, LANES), F32)
    in_names = [nm for nm, _, _ in ins]
    outs = pl.pallas_call(
        functools.partial(_proj_kernel, segs, heads, seq_tiles, (chunk, nchunk, gheads, nlev), in_names, names),
        grid=(tok // tm,),
        in_specs=[sp for _, _, sp in ins],
        out_specs=[out_specs[nm] for nm in names],
        out_shape=[out_shape[nm] for nm in names],
        scratch_shapes=scratch,
        compiler_params=_cparams(("arbitrary",)),
        name="proj",
    )(*[a for _, a, _ in ins])
    return dict(zip(names, outs))


def _diff_lambda(lamv, lam_init):
    a = jnp.sum(lamv[0:1, :] * lamv[1:2, :], axis=1, keepdims=True)
    b = jnp.sum(lamv[2:3, :] * lamv[3:4, :], axis=1, keepdims=True)
    return jnp.exp(a) - jnp.exp(b) + lam_init


def _stack_maps(q):
    lane = lax.broadcasted_iota(jnp.int32, q.shape, 1)
    half = q.shape[1] // 2
    zero = jnp.zeros_like(q)
    return jnp.concatenate([jnp.where(lane < half, q, zero), jnp.where(lane >= half, q, zero)], axis=0)


def _attn_prompt_kernel(lam_init, tq, ra, rb, lamv_ref, q_ref, kt_ref, v_ref, o_ref,
                        qx_ref, acc_ref, s0_ref, s1_ref, m0_ref, m1_ref, a0_ref, a1_ref):
    ones = jnp.ones((tq, LANES), BF16)
    sbuf, mbuf, abuf = (s0_ref, s1_ref), (m0_ref, m1_ref), (a0_ref, a1_ref)
    nq = q_ref.shape[1] // tq
    hq = tq // 2
    assert hq % max(ra, rb) == 0

    def visible(row0):
        return hq if (row0 % tq) < hq else tq

    def values(j):
        start = pl.multiple_of(j * tq, tq)
        return jnp.concatenate([v_ref[0, pl.ds(start, tq), :], ones], axis=1)

    def own_a(i):
        kt = kt_ref[0, :, pl.ds(pl.multiple_of(i * tq, tq), tq)]
        for part in range(4):
            r0 = part * hq
            ncol = visible(r0)
            s = jnp.dot(qx_ref[r0:r0 + hq, :], kt[:, :ncol], preferred_element_type=F32)
            qc = (r0 % tq + lax.broadcasted_iota(jnp.int32, s.shape, 0)) // CHUNK
            kc = lax.broadcasted_iota(jnp.int32, s.shape, 1) // CHUNK
            s0_ref[r0:r0 + hq, :ncol] = jnp.where(kc <= qc, s, -jnp.inf)
        for r in range(2 * tq // ra):
            rows = slice(r * ra, (r + 1) * ra)
            m_cur = jnp.max(s0_ref[rows, :visible(r * ra)], axis=1, keepdims=True)
            m0_ref[rows, :] = jnp.broadcast_to(m_cur, (ra, LANES))

    def own_b(i):
        vx = values(i)
        for r in range(2 * tq // rb):
            rows = slice(r * rb, (r + 1) * rb)
            ncol = visible(r * rb)
            p = jnp.exp2(s0_ref[rows, :ncol] - jnp.tile(m0_ref[rows, :], (1, ncol // LANES)))
            acc_ref[rows, :] = jnp.dot(p.astype(BF16), vx[:ncol], preferred_element_type=F32)

    def stage_a(j, slot):
        kt = kt_ref[0, :, pl.ds(pl.multiple_of(j * tq, tq), tq)]
        sbuf[slot][...] = jnp.dot(qx_ref[...], kt, preferred_element_type=F32)
        for r in range(2 * tq // ra):
            rows = slice(r * ra, (r + 1) * ra)
            m_prev = mbuf[1 - slot][rows, :]
            m_new = jnp.maximum(m_prev, jnp.max(sbuf[slot][rows, :], axis=1, keepdims=True))
            abuf[slot][rows, :] = jnp.exp2(m_prev - m_new)
            mbuf[slot][rows, :] = m_new

    def stage_b(j, slot):
        vx = values(j)
        for r in range(2 * tq // rb):
            rows = slice(r * rb, (r + 1) * rb)
            p = jnp.exp2(sbuf[slot][rows, :] - jnp.tile(mbuf[slot][rows, :], (1, tq // LANES)))
            pv = jnp.dot(p.astype(BF16), vx, preferred_element_type=F32)
            acc_ref[rows, :] = acc_ref[rows, :] * jnp.tile(abuf[slot][rows, :], (1, 2)) + pv

    def q_tile(i, carry):
        qrows = pl.ds(pl.multiple_of(i * tq, tq), tq)
        qx_ref[...] = _stack_maps(q_ref[0, qrows, :])
        own_a(i)

        @pl.when(i == 0)
        def _():
            own_b(i)

        @pl.when(i > 0)
        def _():
            stage_a(0, 1)
            own_b(i)

            def pair(u, c):
                stage_a(2 * u + 1, 0)
                stage_b(2 * u, 1)
                stage_a(2 * u + 2, 1)
                stage_b(2 * u + 1, 0)
                return c

            lax.fori_loop(0, (i - 1) // 2, pair, 0)

            @pl.when(i % 2 == 1)
            def _():
                stage_b(i - 1, 1)

            @pl.when(i % 2 == 0)
            def _():
                stage_a(i - 1, 0)
                stage_b(i - 2, 1)
                stage_b(i - 1, 0)

        acc = acc_ref[...]
        o1 = acc[:tq, :LANES] / acc[:tq, LANES:]
        o2 = acc[tq:, :LANES] / acc[tq:, LANES:]
        o_ref[0, qrows, :] = o1 - _diff_lambda(lamv_ref[...], lam_init) * o2
        return carry

    lax.fori_loop(0, nq, q_tile, 0)


def _attn_prompt(q, kt, v, lamv, lam_init, heads):
    bsz, t, w = q.shape
    tq = min(ATTN_TILE, t)
    ra = min(ATTN_MAX_ROWS, tq)
    rb = min(ATTN_PV_ROWS, tq)
    rowbuf = pltpu.VMEM((2 * tq, LANES), F32)
    return pl.pallas_call(
        functools.partial(_attn_prompt_kernel, lam_init, tq, ra, rb),
        grid=(bsz, heads),
        in_specs=[pl.BlockSpec(lamv.shape, lambda b, h: (0, 0)),
                  pl.BlockSpec((1, t, LANES), lambda b, h: (b, 0, h)),
                  pl.BlockSpec((1, LANES, t), lambda b, h: (b, h, 0)),
                  pl.BlockSpec((1, t, LANES), lambda b, h: (b, 0, h))],
        out_specs=pl.BlockSpec((1, t, LANES), lambda b, h: (b, 0, h)),
        out_shape=jax.ShapeDtypeStruct((bsz, t, w), F32),
        scratch_shapes=[pltpu.VMEM((2 * tq, LANES), BF16), pltpu.VMEM((2 * tq, 2 * LANES), F32),
                        pltpu.VMEM((2 * tq, tq), F32), pltpu.VMEM((2 * tq, tq), F32),
                        rowbuf, rowbuf, rowbuf, rowbuf],
        compiler_params=_cparams(("arbitrary", "arbitrary")),
        name="attn_prompt",
    )(lamv, q, kt, v)


def _attn_sample_kernel(lam_init, heads, lamv_ref, q_ref, ckt_ref, cv_ref, kn_ref, vn_ref, o_ref):
    nrow, tq = q_ref.shape[0], q_ref.shape[1]
    past = ckt_ref.shape[2]
    lam = _diff_lambda(lamv_ref[...], lam_init)
    for n in range(nrow):
        for h in range(heads):
            sl = slice(h * LANES, (h + 1) * LANES)
            qx = _stack_maps(q_ref[n, :, sl])
            kc = ckt_ref[n, sl, :].astype(BF16)
            vc = cv_ref[n, pl.ds(h, past, stride=heads), :].astype(BF16)
            kn = kn_ref[n, :, sl]
            vn = vn_ref[n, :, sl]
            sc = jnp.dot(qx, kc, preferred_element_type=F32)
            sn = lax.dot_general(qx, kn, _NT, preferred_element_type=F32)
            m = jnp.maximum(jnp.max(sc, axis=1, keepdims=True), jnp.max(sn, axis=1, keepdims=True))
            ec = jnp.exp2(sc - m)
            en = jnp.exp2(sn - m)
            l = jnp.sum(ec, axis=1, keepdims=True) + jnp.sum(en, axis=1, keepdims=True)
            o = (jnp.dot(ec.astype(BF16), vc, preferred_element_type=F32)
                 + jnp.dot(en.astype(BF16), vn, preferred_element_type=F32)) / l
            o_ref[n, :, sl] = o[:tq] - lam * o[tq:]


def _attn_sample(q, ckt, cv, kn, vn, lamv, lam_init, heads):
    bsz, tq, w = q.shape
    nrow = math.gcd(bsz, ATTN_ROWS_PER_STEP)
    rows = lambda a: pl.BlockSpec((nrow,) + a.shape[1:], lambda b: (b, 0, 0))
    return pl.pallas_call(
        functools.partial(_attn_sample_kernel, lam_init, heads),
        grid=(bsz // nrow,),
        in_specs=[pl.BlockSpec(lamv.shape, lambda b: (0, 0)), rows(q), rows(ckt), rows(cv), rows(kn), rows(vn)],
        out_specs=rows(q),
        out_shape=jax.ShapeDtypeStruct((bsz, tq, w), F32),
        compiler_params=_cparams(("arbitrary",)),
        name="attn_sample",
    )(lamv, q, ckt, cv, kn, vn)


def _gla_tables(c, heads, nchunk):
    nlev = int(math.log2(c))
    assert 1 << nlev == c and c % 8 == 0
    t = np.arange(c)[:, None]
    u = np.arange(c)[None, :]
    lev = np.where(t == u, 0, -1)
    for l in range(1, nlev + 1):
        n = c >> l
        same_block = (t // (2 * n)) == (u // (2 * n))
        lev = np.where(same_block & ((t % (2 * n)) >= n) & ((u % (2 * n)) < n), l, lev)
    lt = (u <= t).astype(np.float32)
    return np.concatenate([lt, lt, lt], axis=1), np.tile(lev, (nchunk, heads)).astype(np.int32), nlev


def _block_reference(b, n):
    rows, w = b.shape
    if n >= 8:
        parts = [jnp.broadcast_to(b[g + n - 1:g + n, :], (2 * n, w)) for g in range(0, rows, 2 * n)]
        return parts[0] if len(parts) == 1 else jnp.concatenate(parts, axis=0)
    b3 = b.reshape(rows // 8, 8, w)
    sub = lax.broadcasted_iota(jnp.int32, b3.shape, 1)
    pick = lambda j: jnp.broadcast_to(b3[:, j:j + 1, :], b3.shape)
    if n == 4:
        ref = pick(3)
    elif n == 2:
        ref = jnp.where(sub < 4, pick(1), pick(5))
    else:
        ref = jnp.where(sub < 2, pick(0), jnp.where(sub < 4, pick(2), jnp.where(sub < 6, pick(4), pick(6))))
    return ref.reshape(rows, w)


def _gla_block(c, nchunk, heads, nlev, lt, lev, q, k, v, la, s, fillers=(), independent=False):
    s_in = s
    fillers = list(fillers)
    fill = lambda: fillers.pop(0)() if fillers else None
    kw = q.shape[-1]
    vw = v.shape[-1]
    dk = kw // heads
    dv = vw // heads
    pad = LANES - c
    assert dv == LANES and pad >= 0
    chunk = lambda a, ci: a[ci * c:(ci + 1) * c]

    la2 = la * math.log2(math.e)
    p0 = la2.astype(BF16)
    r0 = la2 - p0.astype(F32)
    p1 = r0.astype(BF16)
    p2 = (r0 - p1.astype(F32)).astype(BF16)
    b = jnp.concatenate(
        [jnp.dot(lt, jnp.concatenate([chunk(p0, ci), chunk(p1, ci), chunk(p2, ci)], axis=0),
                 preferred_element_type=F32) for ci in range(nchunk)], axis=0)

    t = lax.broadcasted_iota(jnp.int32, b.shape, 0) % c
    klane_h = lax.broadcasted_iota(jnp.int32, (c, kw), 1) // dk
    vlane_h = lax.broadcasted_iota(jnp.int32, (c, vw), 1) // dv

    att = None
    for l in range(nlev + 1):
        if l == 0:
            qt, kt = q.astype(BF16), k.astype(BF16)
        else:
            e = jnp.exp2(-jnp.abs(b - _block_reference(b, c >> l)))
            qt = (q * e).astype(BF16)
            kt = (k * e).astype(BF16)
        parts = []
        for ci in range(nchunk):
            ktc = chunk(kt, ci)
            kstack = jnp.concatenate(
                [jnp.where(klane_h == h, ktc, jnp.zeros_like(ktc)) for h in range(heads)], axis=0)
            parts.append(lax.dot_general(chunk(qt, ci), kstack, _NT, preferred_element_type=F32))
        a = parts[0] if nchunk == 1 else jnp.concatenate(parts, axis=0)
        att = jnp.where(lev == l, a, 0.0 if att is None else att)
        fill()
    att = att.astype(BF16)
    qe = (q * jnp.exp2(b)).astype(BF16)

    o_intra, upd, decay = [], [], []
    for ci in range(nchunk):
        rows = slice(ci * c, (ci + 1) * c)
        vc = v[rows]
        vbd = jnp.concatenate(
            [jnp.where(vlane_h == h, vc, jnp.zeros_like(vc)) for h in range(heads)], axis=0)
        o_intra.append(jnp.dot(att[rows], vbd, preferred_element_type=F32))
        b_last = b[ci * c + c - 1:ci * c + c, :]
        kd = k[rows] * jnp.exp2(b_last - b[rows])
        x = jnp.concatenate([kd, jnp.zeros((pad, kw), F32), jnp.broadcast_to(b_last, (LANES, kw))], axis=0)
        xt = x.T
        for h in range(heads):
            hk = slice(h * dk, (h + 1) * dk)
            vpad = jnp.concatenate([vc[:, h * dv:(h + 1) * dv], jnp.zeros((pad, dv), BF16)], axis=0)
            upd.append(jnp.dot(xt[hk, :LANES].astype(BF16), vpad, preferred_element_type=F32))
            decay.append(jnp.exp2(xt[hk, LANES:]))
        fill()

    zero = jnp.zeros((dk, dv), BF16)
    outs, s_out = [], []
    for ci in range(nchunk):
        if independent:
            s = s_in[ci]
        sbd = jnp.concatenate(
            [jnp.concatenate([s[h].astype(BF16) if g == h else zero for g in range(heads)], axis=1)
             for h in range(heads)], axis=0)
        outs.append(o_intra[ci] + jnp.dot(chunk(qe, ci), sbd, preferred_element_type=F32))
        s = [decay[ci * heads + h] * s[h] + upd[ci * heads + h] for h in range(heads)]
        s_out.append(s)
    while fillers:
        fill()
    return outs, (s_out if independent else s)


def _head_rmsnorm(x, g, heads):
    outs = []
    for h in range(heads):
        blk = x[:, h * LANES:(h + 1) * LANES]
        ms = jnp.mean(blk * blk, axis=-1, keepdims=True)
        outs.append((blk * lax.rsqrt(ms + EPS)) * g)
    return jnp.concatenate(outs, axis=1)


def _merge_kernel(lam_init, final, da_heads, gla_heads, x_ref, oa_ref, ob_ref, dg_ref, gg_ref, mg_ref,
                  ga_ref, gb_ref, wa_ref, wb_ref, wo_ref, gf_ref, y_ref):
    d = x_ref.shape[-1]

    def silu_of_half(h):
        return h * jnp.tanh(h) + h

    oa = _head_rmsnorm(oa_ref[...], ga_ref[...] * (1.0 - lam_init), da_heads) * silu_of_half(dg_ref[...].astype(F32))
    ob = _head_rmsnorm(ob_ref[...], gb_ref[...], gla_heads) * silu_of_half(gg_ref[...].astype(F32))
    ya = jnp.dot(oa.astype(BF16), wa_ref[...], preferred_element_type=F32)
    yb = jnp.dot(ob.astype(BF16), wb_ref[...], preferred_element_type=F32)
    ta = jnp.tanh(mg_ref[:, :d].astype(F32))
    tb = jnp.tanh(mg_ref[:, d:].astype(F32))
    mixed = 0.5 * ((ta * ya + ya) + (tb * yb + yb))
    y = x_ref[...] + jnp.dot(mixed.astype(BF16), wo_ref[...], preferred_element_type=F32)
    if final:
        ms = jnp.mean(y * y, axis=-1, keepdims=True)
        y = (y * lax.rsqrt(ms + EPS)) * gf_ref[...]
    y_ref[...] = y


def _merge(x2, oa, ob, dg, gg, mg, da_g, gla_g, wa, wb, wo, gf, lam_init, final, da_heads, gla_heads, tm):
    tok, d = x2.shape
    row = lambda w: pl.BlockSpec((tm, w), lambda i: (i, 0))
    full = lambda a: pl.BlockSpec(a.shape, lambda i: (0, 0))
    da_g = da_g.reshape(1, -1)
    gla_g = gla_g.reshape(1, -1)
    gf = gf.reshape(1, d)
    return pl.pallas_call(
        functools.partial(_merge_kernel, lam_init, final, da_heads, gla_heads),
        grid=(tok // tm,),
        in_specs=[row(d), row(oa.shape[1]), row(ob.shape[1]), row(dg.shape[1]), row(gg.shape[1]), row(mg.shape[1]),
                  full(da_g), full(gla_g), full(wa), full(wb), full(wo), full(gf)],
        out_specs=row(d),
        out_shape=jax.ShapeDtypeStruct((tok, d), F32),
        compiler_params=_cparams(("arbitrary",)),
        name="merge",
    )(x2, oa, ob, dg, gg, mg, da_g, gla_g, wa, wb, wo, gf)


def _layout(da_heads, da_dk, gla_heads, gla_dk, gla_dv):
    da_qk = da_heads * 2 * da_dk
    da_w = da_heads * 2 * da_dk
    kwid = gla_heads * gla_dk
    vwid = gla_heads * gla_dv
    names = ["dq", "dk", "dv", "dg", "gq", "gk", "gv", "gg"]
    sizes = [da_qk, da_qk, da_w, da_w, kwid, kwid, vwid, vwid]
    segs, off = {}, 0
    for nm, sz in zip(names, sizes):
        segs[nm] = (off, off + sz)
        off += sz
    segs["da_scale"] = da_dk ** -0.5 * math.log2(math.e)
    segs["gla_scale"] = gla_dk ** -0.5
    return segs, off


def kernel(x_prompt, x_sample, cache_k, cache_v, state_gla, norm_in_g, w_in, w_alpha_up, b_alpha,
           lambda_q1, lambda_k1, lambda_q2, lambda_k2, da_norm_g, gla_norm_g,
           w_branch_a, w_branch_b, w_out, norm_final_g):
    depth = w_in.shape[0]
    bsz, seq, d = x_prompt.shape
    dbsz, dseq, _ = x_sample.shape
    past = cache_k.shape[2]
    da_heads, da_dk = cache_k.shape[3], cache_k.shape[5]
    gla_heads, gla_dk, gla_dv = state_gla.shape[2:]
    rank = w_alpha_up.shape[1]
    kw = gla_heads * gla_dk
    assert 2 * da_dk == LANES and cache_v.shape[4] == LANES and gla_dv == LANES
    segs, ga_lo = _layout(da_heads, da_dk, gla_heads, gla_dk, gla_dv)

    xp = x_prompt.reshape(bsz * seq, d)
    xs = x_sample.reshape(dbsz * dseq, d)
    tm_p = min(ROW_TILE, bsz * seq)
    tm_s = min(ROW_TILE, dbsz * dseq)
    outs = [[] for _ in range(6)]
    for l in range(depth):
        lam_init = 0.8 - 0.6 * math.exp(-0.3 * l)
        final = l == depth - 1
        w = w_in[l]
        w_gate = jnp.pad(w[:, ga_lo:ga_lo + rank], ((0, 0), (0, LANES - rank))).astype(BF16)
        wkt = w[:, segs["dk"][0]:segs["dk"][1]].T.astype(BF16)
        wa_pad = jnp.pad(w_alpha_up[l], ((0, LANES - rank), (0, 0))).astype(BF16)
        weights = (w[:, :ga_lo].astype(BF16), w[:, ga_lo + rank:].astype(BF16), w_gate, wkt, wa_pad)
        lamv = jnp.stack([lambda_q1[l], lambda_k1[l], lambda_q2[l], lambda_k2[l]])
        wa, wb, wo = w_branch_a[l].astype(BF16), w_branch_b[l].astype(BF16), w_out[l].astype(BF16)

        p = _proj(xp, norm_in_g[l], weights, b_alpha[l], segs, tm_p, da_heads,
                  jnp.zeros((bsz, gla_heads, gla_dk, gla_dv), F32), seq=seq)
        sh = lambda a: a.reshape(bsz, seq, a.shape[-1])
        oa = _attn_prompt(sh(p["q"]), p["kb"], sh(p["vb"]), lamv, lam_init, da_heads)
        s_p = p["sout"]
        xp = _merge(xp, oa.reshape(bsz * seq, -1), p["ob"], p["dg"], p["gg"], p["mg"],
                    da_norm_g[l], gla_norm_g[l], wa, wb, wo, norm_final_g, lam_init, final,
                    da_heads, gla_heads, tm_p)
        outs[0].append(jnp.transpose(p["kf"].reshape(bsz, da_heads, 2, da_dk, seq), (0, 4, 1, 2, 3)))
        outs[1].append(p["vf"].reshape(bsz, seq, da_heads, 2 * da_dk))
        outs[2].append(s_p)

        p = _proj(xs, norm_in_g[l], weights, b_alpha[l], segs, tm_s, da_heads, state_gla[l])
        sh = lambda a: a.reshape(dbsz, dseq, a.shape[-1])
        ckt = jnp.transpose(cache_k[l], (0, 2, 3, 4, 1)).reshape(dbsz, da_heads * 2 * da_dk, past)
        oa = _attn_sample(sh(p["q"]), ckt, cache_v[l].reshape(dbsz, past * da_heads, 2 * da_dk),
                          sh(p["kb"]), sh(p["vb"]), lamv, lam_init, da_heads)
        s_s = p["sout"]
        xs = _merge(xs, oa.reshape(dbsz * dseq, -1), p["ob"], p["dg"], p["gg"], p["mg"],
                    da_norm_g[l], gla_norm_g[l], wa, wb, wo, norm_final_g, lam_init, final,
                    da_heads, gla_heads, tm_s)
        outs[3].append(p["kf"].reshape(dbsz, dseq, da_heads, 2, da_dk))
        outs[4].append(p["vf"].reshape(dbsz, dseq, da_heads, 2 * da_dk))
        outs[5].append(s_s)

    return (xp.reshape(bsz, seq, d), xs.reshape(dbsz, dseq, d),
            jnp.stack(outs[0]), jnp.stack(outs[1]), jnp.stack(outs[2]),
            jnp.stack(outs[3]), jnp.stack(outs[4]), jnp.stack(outs[5]))
```

```python
import functools
import math

import numpy as np
import jax
import jax.numpy as jnp
from jax import lax
from jax.experimental import pallas as pl
from jax.experimental.pallas import tpu as pltpu

F32 = jnp.float32
BF16 = jnp.bfloat16

EPS = 1e-6
CHUNK = 64
GLA_TAU = 16.0
LANES = 128
VMEM_LIMIT_BYTES = 56 * 1024 * 1024

ROW_TILE = 512
PROJ_COL_STEP = 512
ATTN_TILE = 1024
ATTN_MAX_ROWS = 128
ATTN_PV_ROWS = 256
ATTN_ROWS_PER_STEP = 2

_NT = (((1,), (1,)), ((), ()))


def _log_sigmoid(x):
    return jnp.minimum(x, 0.0) - jnp.log1p(jnp.exp(-jnp.abs(x)))


def _cparams(sem):
    return pltpu.CompilerParams(dimension_semantics=sem, vmem_limit_bytes=VMEM_LIMIT_BYTES)


def _proj_kernel(segs, heads, seq_tiles, gla, in_names, out_names, *refs):
    n_in, n_out = len(in_names), len(out_names)
    r = dict(zip(in_names + out_names, refs[:n_in + n_out]))
    x = r["x"][...]
    tm = x.shape[0]
    ms = jnp.mean(x * x, axis=-1, keepdims=True)
    xn = ((x * lax.rsqrt(ms + EPS)) * r["g"][...]).astype(BF16)

    def seg(name):
        lo, hi = segs[name]
        return jnp.dot(xn, r["w_main"][:, lo:hi], preferred_element_type=F32)

    def put_q():
        r["q"][...] = (seg("dq") * segs["da_scale"]).astype(BF16)

    def put_k():
        if seq_tiles is not None:
            kt = lax.dot_general(r["wkt"][...], xn, _NT, preferred_element_type=F32)
            r["kf"][0] = kt
            r["kb"][0] = kt.astype(BF16)
        else:
            k = seg("dk")
            r["kf"][...] = k
            r["kb"][...] = k.astype(BF16)

    def put_v():
        v = seg("dv")
        r["vb"][...] = v.astype(BF16)
        if seq_tiles is not None:
            for h in range(heads):
                r["vf"][pl.ds(h, tm, stride=heads), :] = v[:, h * LANES:(h + 1) * LANES]
        else:
            r["vf"][...] = v

    def put_gate(name):
        r[name][...] = (0.5 * seg(name)).astype(BF16)

    def put_merge(c0, step):
        r["mg"][:, c0:c0 + step] = (0.5 * jnp.dot(xn, r["w_merge"][:, c0:c0 + step],
                                                  preferred_element_type=F32)).astype(BF16)

    step = PROJ_COL_STEP
    rest = [put_q, put_k, put_v, functools.partial(put_gate, "dg"), functools.partial(put_gate, "gg")]
    rest += [functools.partial(put_merge, c0, step) for c0 in range(0, r["w_merge"].shape[1], step)]

    ga = jnp.dot(xn, r["w_gate"][...], preferred_element_type=F32).astype(BF16)
    z = jnp.dot(ga, r["wa"][...], preferred_element_type=F32) + r["ba"][...]
    la = _log_sigmoid(z) / GLA_TAU
    gq = seg("gq") * segs["gla_scale"]
    gk = seg("gk")
    gv = seg("gv").astype(BF16)
    if seq_tiles is None:
        c, nchunk, gheads, nlev = gla
        states = [[r["s0"][n, h] for h in range(gheads)] for n in range(nchunk)]
        outs, s = _gla_block(c, nchunk, gheads, nlev, r["lt"][...], r["lev"][...], gq, gk, gv, la,
                             states, fillers=rest, independent=True)
        for n, o in enumerate(outs):
            r["ob"][n * c:(n + 1) * c, :] = o
            for h in range(gheads):
                r["sout"][n, h] = s[n][h]
    else:
        c, nchunk, gheads, nlev = gla
        s_ref = refs[n_in + n_out]
        ti = pl.program_id(0) % seq_tiles

        @pl.when(ti == 0)
        def _():
            s_ref[...] = r["s0"][0]

        outs, s = _gla_block(c, nchunk, gheads, nlev, r["lt"][...], r["lev"][...], gq, gk, gv, la,
                             [s_ref[h] for h in range(gheads)], fillers=rest)
        for ci, o in enumerate(outs):
            r["ob"][ci * c:(ci + 1) * c, :] = o
        for h in range(gheads):
            s_ref[h] = s[h]

        @pl.when(ti == seq_tiles - 1)
        def _():
            r["sout"][0] = s_ref[...]


def _proj(x2, g_in, weights, b_alpha, segs, tm, heads, gla_state, seq=None):
    tok, d = x2.shape
    w_main, w_merge, w_gate, wkt, wa_pad = weights
    widths = {k: v[1] - v[0] for k, v in segs.items() if isinstance(v, tuple)}
    widths["mg"] = w_merge.shape[1]
    kw = wa_pad.shape[1]
    gheads, dk, dv = gla_state.shape[1:]
    seq_tiles = None if seq is None else seq // tm
    chunk = CHUNK if seq is not None else tok // gla_state.shape[0]
    nchunk = tm // chunk
    lt3, lev, nlev = _gla_tables(chunk, gheads, nchunk)

    def row_spec(w):
        return pl.BlockSpec((tm, w), lambda i: (i, 0))

    def full_spec(shape):
        return pl.BlockSpec(shape, lambda i: (0,) * len(shape))

    def const_spec(shape):
        return pl.BlockSpec(shape, lambda i: (0,) * len(shape), pipeline_mode=pl.Buffered(1))

    if seq is not None:
        st = pl.BlockSpec((1, gheads, dk, dv), lambda i: (i // seq_tiles, 0, 0, 0))
        scratch = [pltpu.VMEM((gheads, dk, dv), F32)]
    else:
        st = pl.BlockSpec((nchunk, gheads, dk, dv), lambda i: (i, 0, 0, 0))
        scratch = []
    ins = [("x", x2, row_spec(d)), ("g", g_in.reshape(1, d), full_spec((1, d))),
           ("w_main", w_main, const_spec(w_main.shape)), ("w_merge", w_merge, const_spec(w_merge.shape)),
           ("w_gate", w_gate, const_spec(w_gate.shape)), ("wkt", wkt, const_spec(wkt.shape)),
           ("wa", wa_pad, full_spec(wa_pad.shape)), ("ba", b_alpha.reshape(1, kw), full_spec((1, kw))),
           ("lt", jnp.asarray(lt3, BF16), full_spec(lt3.shape)), ("lev", jnp.asarray(lev), full_spec(lev.shape)),
           ("s0", gla_state, st)]
    out_defs = [
        ("q", widths["dq"], BF16), ("kf", widths["dk"], F32), ("kb", widths["dk"], BF16),
        ("vf", widths["dv"], F32), ("vb", widths["dv"], BF16), ("dg", widths["dg"], BF16),
        ("gg", widths["gg"], BF16), ("mg", widths["mg"], BF16), ("ob", widths["gv"], F32),
    ]
    out_specs = {nm: row_spec(w) for nm, w, _ in out_defs}
    out_shape = {nm: jax.ShapeDtypeStruct((tok, w), dt) for nm, w, dt in out_defs}
    names = [nm for nm, _, _ in out_defs] + ["sout"]
    out_specs["sout"] = st
    out_shape["sout"] = jax.ShapeDtypeStruct(gla_state.shape, F32)
    if seq is not None:
        for nm, dt in (("kf", F32), ("kb", BF16)):
            out_specs[nm] = pl.BlockSpec((1, widths["dk"], tm), lambda i: (i // seq_tiles, 0, i % seq_tiles))
            out_shape[nm] = jax.ShapeDtypeStruct((tok // seq, widths["dk"], seq), dt)
        out_specs["vf"] = pl.BlockSpec((tm * heads, LANES), lambda i: (i, 0))
        out_shape["vf"] = jax.ShapeDtypeStruct((tok * heads, LANES), F32)
    in_names = [nm for nm, _, _ in ins]
    outs = pl.pallas_call(
        functools.partial(_proj_kernel, segs, heads, seq_tiles, (chunk, nchunk, gheads, nlev), in_names, names),
        grid=(tok // tm,),
        in_specs=[sp for _, _, sp in ins],
        out_specs=[out_specs[nm] for nm in names],
        out_shape=[out_shape[nm] for nm in names],
        scratch_shapes=scratch,
        compiler_params=_cparams(("arbitrary",)),
        name="proj",
    )(*[a for _, a, _ in ins])
    return dict(zip(names, outs))


def _diff_lambda(lamv, lam_init):
    a = jnp.sum(lamv[0:1, :] * lamv[1:2, :], axis=1, keepdims=True)
    b = jnp.sum(lamv[2:3, :] * lamv[3:4, :], axis=1, keepdims=True)
    return jnp.exp(a) - jnp.exp(b) + lam_init


def _stack_maps(q):
    lane = lax.broadcasted_iota(jnp.int32, q.shape, 1)
    half = q.shape[1] // 2
    zero = jnp.zeros_like(q)
    return jnp.concatenate([jnp.where(lane < half, q, zero), jnp.where(lane >= half, q, zero)], axis=0)


def _attn_prompt_kernel(lam_init, tq, ra, rb, lamv_ref, q_ref, kt_ref, v_ref, o_ref,
                        qx_ref, acc_ref, s0_ref, s1_ref, m0_ref, m1_ref, a0_ref, a1_ref):
    ones = jnp.ones((tq, LANES), BF16)
    sbuf, mbuf, abuf = (s0_ref, s1_ref), (m0_ref, m1_ref), (a0_ref, a1_ref)
    nq = q_ref.shape[1] // tq
    hq = tq // 2
    assert hq % max(ra, rb) == 0

    def visible(row0):
        return hq if (row0 % tq) < hq else tq

    def values(j):
        start = pl.multiple_of(j * tq, tq)
        return jnp.concatenate([v_ref[0, pl.ds(start, tq), :], ones], axis=1)

    def own_a(i):
        kt = kt_ref[0, :, pl.ds(pl.multiple_of(i * tq, tq), tq)]
        for part in range(4):
            r0 = part * hq
            ncol = visible(r0)
            s = jnp.dot(qx_ref[r0:r0 + hq, :], kt[:, :ncol], preferred_element_type=F32)
            qc = (r0 % tq + lax.broadcasted_iota(jnp.int32, s.shape, 0)) // CHUNK
            kc = lax.broadcasted_iota(jnp.int32, s.shape, 1) // CHUNK
            s0_ref[r0:r0 + hq, :ncol] = jnp.where(kc <= qc, s, -jnp.inf)
        for r in range(2 * tq // ra):
            rows = slice(r * ra, (r + 1) * ra)
            m_cur = jnp.max(s0_ref[rows, :visible(r * ra)], axis=1, keepdims=True)
            m0_ref[rows, :] = jnp.broadcast_to(m_cur, (ra, LANES))

    def own_b(i):
        vx = values(i)
        for r in range(2 * tq // rb):
            rows = slice(r * rb, (r + 1) * rb)
            ncol = visible(r * rb)
            p = jnp.exp2(s0_ref[rows, :ncol] - jnp.tile(m0_ref[rows, :], (1, ncol // LANES)))
            acc_ref[rows, :] = jnp.dot(p.astype(BF16), vx[:ncol], preferred_element_type=F32)

    def stage_a(j, slot):
        kt = kt_ref[0, :, pl.ds(pl.multiple_of(j * tq, tq), tq)]
        sbuf[slot][...] = jnp.dot(qx_ref[...], kt, preferred_element_type=F32)
        for r in range(2 * tq // ra):
            rows = slice(r * ra, (r + 1) * ra)
            m_prev = mbuf[1 - slot][rows, :]
            m_new = jnp.maximum(m_prev, jnp.max(sbuf[slot][rows, :], axis=1, keepdims=True))
            abuf[slot][rows, :] = jnp.exp2(m_prev - m_new)
            mbuf[slot][rows, :] = m_new

    def stage_b(j, slot):
        vx = values(j)
        for r in range(2 * tq // rb):
            rows = slice(r * rb, (r + 1) * rb)
            p = jnp.exp2(sbuf[slot][rows, :] - jnp.tile(mbuf[slot][rows, :], (1, tq // LANES)))
            pv = jnp.dot(p.astype(BF16), vx, preferred_element_type=F32)
            acc_ref[rows, :] = acc_ref[rows, :] * jnp.tile(abuf[slot][rows, :], (1, 2)) + pv

    def finish(i):
        acc = acc_ref[...]
        o1 = acc[:tq, :LANES] / acc[:tq, LANES:]
        o2 = acc[tq:, :LANES] / acc[tq:, LANES:]
        o_ref[0, pl.ds(pl.multiple_of(i * tq, tq), tq), :] = o1 - _diff_lambda(lamv_ref[...], lam_init) * o2

    def q_tile(i, carry):
        qrows = pl.ds(pl.multiple_of(i * tq, tq), tq)
        qx_ref[...] = _stack_maps(q_ref[0, qrows, :])
        finish(jnp.maximum(i - 1, 0))
        own_a(i)

        @pl.when(i == 0)
        def _():
            own_b(i)

        @pl.when(i > 0)
        def _():
            stage_a(0, 1)
            own_b(i)

            def pair(u, c):
                stage_a(2 * u + 1, 0)
                stage_b(2 * u, 1)
                stage_a(2 * u + 2, 1)
                stage_b(2 * u + 1, 0)
                return c

            lax.fori_loop(0, (i - 1) // 2, pair, 0)

            @pl.when(i % 2 == 1)
            def _():
                stage_b(i - 1, 1)

            @pl.when(i % 2 == 0)
            def _():
                stage_a(i - 1, 0)
                stage_b(i - 2, 1)
                stage_b(i - 1, 0)

        return carry

    acc_ref[...] = jnp.ones(acc_ref.shape, F32)
    lax.fori_loop(0, nq, q_tile, 0)
    finish(nq - 1)


def _attn_prompt(q, kt, v, lamv, lam_init, heads):
    bsz, t, w = q.shape
    tq = min(ATTN_TILE, t)
    ra = min(ATTN_MAX_ROWS, tq)
    rb = min(ATTN_PV_ROWS, tq)
    rowbuf = pltpu.VMEM((2 * tq, LANES), F32)
    return pl.pallas_call(
        functools.partial(_attn_prompt_kernel, lam_init, tq, ra, rb),
        grid=(bsz, heads),
        in_specs=[pl.BlockSpec(lamv.shape, lambda b, h: (0, 0)),
                  pl.BlockSpec((1, t, LANES), lambda b, h: (b, 0, h)),
                  pl.BlockSpec((1, LANES, t), lambda b, h: (b, h, 0)),
                  pl.BlockSpec((1, t, LANES), lambda b, h: (b, 0, h))],
        out_specs=pl.BlockSpec((1, t, LANES), lambda b, h: (b, 0, h)),
        out_shape=jax.ShapeDtypeStruct((bsz, t, w), F32),
        scratch_shapes=[pltpu.VMEM((2 * tq, LANES), BF16), pltpu.VMEM((2 * tq, 2 * LANES), F32),
                        pltpu.VMEM((2 * tq, tq), F32), pltpu.VMEM((2 * tq, tq), F32),
                        rowbuf, rowbuf, rowbuf, rowbuf],
        compiler_params=_cparams(("arbitrary", "arbitrary")),
        name="attn_prompt",
    )(lamv, q, kt, v)


def _attn_sample_kernel(lam_init, heads, lamv_ref, q_ref, ckt_ref, cv_ref, kn_ref, vn_ref, o_ref):
    nrow, tq = q_ref.shape[0], q_ref.shape[1]
    past = ckt_ref.shape[2]
    lam = _diff_lambda(lamv_ref[...], lam_init)
    for n in range(nrow):
        for h in range(heads):
            sl = slice(h * LANES, (h + 1) * LANES)
            qx = _stack_maps(q_ref[n, :, sl])
            kc = ckt_ref[n, sl, :].astype(BF16)
            vc = cv_ref[n, pl.ds(h, past, stride=heads), :].astype(BF16)
            kn = kn_ref[n, :, sl]
            vn = vn_ref[n, :, sl]
            sc = jnp.dot(qx, kc, preferred_element_type=F32)
            sn = lax.dot_general(qx, kn, _NT, preferred_element_type=F32)
            m = jnp.maximum(jnp.max(sc, axis=1, keepdims=True), jnp.max(sn, axis=1, keepdims=True))
            ec = jnp.exp2(sc - m)
            en = jnp.exp2(sn - m)
            l = jnp.sum(ec, axis=1, keepdims=True) + jnp.sum(en, axis=1, keepdims=True)
            o = (jnp.dot(ec.astype(BF16), vc, preferred_element_type=F32)
                 + jnp.dot(en.astype(BF16), vn, preferred_element_type=F32)) / l
            o_ref[n, :, sl] = o[:tq] - lam * o[tq:]


def _attn_sample(q, ckt, cv, kn, vn, lamv, lam_init, heads):
    bsz, tq, w = q.shape
    nrow = math.gcd(bsz, ATTN_ROWS_PER_STEP)
    rows = lambda a: pl.BlockSpec((nrow,) + a.shape[1:], lambda b: (b, 0, 0))
    return pl.pallas_call(
        functools.partial(_attn_sample_kernel, lam_init, heads),
        grid=(bsz // nrow,),
        in_specs=[pl.BlockSpec(lamv.shape, lambda b: (0, 0)), rows(q), rows(ckt), rows(cv), rows(kn), rows(vn)],
        out_specs=rows(q),
        out_shape=jax.ShapeDtypeStruct((bsz, tq, w), F32),
        compiler_params=_cparams(("arbitrary",)),
        name="attn_sample",
    )(lamv, q, ckt, cv, kn, vn)


def _gla_tables(c, heads, nchunk):
    nlev = int(math.log2(c))
    assert 1 << nlev == c and c % 8 == 0
    t = np.arange(c)[:, None]
    u = np.arange(c)[None, :]
    lev = np.where(t == u, 0, -1)
    for l in range(1, nlev + 1):
        n = c >> l
        same_block = (t // (2 * n)) == (u // (2 * n))
        lev = np.where(same_block & ((t % (2 * n)) >= n) & ((u % (2 * n)) < n), l, lev)
    lt = (u <= t).astype(np.float32)
    return np.concatenate([lt, lt, lt], axis=1), np.tile(lev, (nchunk, heads)).astype(np.int32), nlev


def _block_reference(b, n):
    rows, w = b.shape
    if n >= 8:
        parts = [jnp.broadcast_to(b[g + n - 1:g + n, :], (2 * n, w)) for g in range(0, rows, 2 * n)]
        return parts[0] if len(parts) == 1 else jnp.concatenate(parts, axis=0)
    b3 = b.reshape(rows // 8, 8, w)
    sub = lax.broadcasted_iota(jnp.int32, b3.shape, 1)
    pick = lambda j: jnp.broadcast_to(b3[:, j:j + 1, :], b3.shape)
    if n == 4:
        ref = pick(3)
    elif n == 2:
        ref = jnp.where(sub < 4, pick(1), pick(5))
    else:
        ref = jnp.where(sub < 2, pick(0), jnp.where(sub < 4, pick(2), jnp.where(sub < 6, pick(4), pick(6))))
    return ref.reshape(rows, w)


def _gla_block(c, nchunk, heads, nlev, lt, lev, q, k, v, la, s, fillers=(), independent=False):
    s_in = s
    fillers = list(fillers)
    fill = lambda: fillers.pop(0)() if fillers else None
    kw = q.shape[-1]
    vw = v.shape[-1]
    dk = kw // heads
    dv = vw // heads
    pad = LANES - c
    assert dv == LANES and pad >= 0
    chunk = lambda a, ci: a[ci * c:(ci + 1) * c]

    la2 = la * math.log2(math.e)
    p0 = la2.astype(BF16)
    r0 = la2 - p0.astype(F32)
    p1 = r0.astype(BF16)
    p2 = (r0 - p1.astype(F32)).astype(BF16)
    b = jnp.concatenate(
        [jnp.dot(lt, jnp.concatenate([chunk(p0, ci), chunk(p1, ci), chunk(p2, ci)], axis=0),
                 preferred_element_type=F32) for ci in range(nchunk)], axis=0)

    t = lax.broadcasted_iota(jnp.int32, b.shape, 0) % c
    klane_h = lax.broadcasted_iota(jnp.int32, (c, kw), 1) // dk
    vlane_h = lax.broadcasted_iota(jnp.int32, (c, vw), 1) // dv

    att = None
    for l in range(nlev + 1):
        if l == 0:
            qt, kt = q.astype(BF16), k.astype(BF16)
        else:
            e = jnp.exp2(-jnp.abs(b - _block_reference(b, c >> l)))
            qt = (q * e).astype(BF16)
            kt = (k * e).astype(BF16)
        parts = []
        for ci in range(nchunk):
            ktc = chunk(kt, ci)
            kstack = jnp.concatenate(
                [jnp.where(klane_h == h, ktc, jnp.zeros_like(ktc)) for h in range(heads)], axis=0)
            parts.append(lax.dot_general(chunk(qt, ci), kstack, _NT, preferred_element_type=F32))
        a = parts[0] if nchunk == 1 else jnp.concatenate(parts, axis=0)
        att = jnp.where(lev == l, a, 0.0 if att is None else att)
        fill()
    att = att.astype(BF16)
    qe = (q * jnp.exp2(b)).astype(BF16)

    o_intra, upd, decay = [], [], []
    for ci in range(nchunk):
        rows = slice(ci * c, (ci + 1) * c)
        vc = v[rows]
        vbd = jnp.concatenate(
            [jnp.where(vlane_h == h, vc, jnp.zeros_like(vc)) for h in range(heads)], axis=0)
        o_intra.append(jnp.dot(att[rows], vbd, preferred_element_type=F32))
        b_last = b[ci * c + c - 1:ci * c + c, :]
        kd = k[rows] * jnp.exp2(b_last - b[rows])
        x = jnp.concatenate([kd, jnp.zeros((pad, kw), F32), jnp.broadcast_to(b_last, (LANES, kw))], axis=0)
        xt = x.T
        for h in range(heads):
            hk = slice(h * dk, (h + 1) * dk)
            vpad = jnp.concatenate([vc[:, h * dv:(h + 1) * dv], jnp.zeros((pad, dv), BF16)], axis=0)
            upd.append(jnp.dot(xt[hk, :LANES].astype(BF16), vpad, preferred_element_type=F32))
            decay.append(jnp.exp2(xt[hk, LANES:]))
        fill()

    zero = jnp.zeros((dk, dv), BF16)
    outs, s_out = [], []
    for ci in range(nchunk):
        if independent:
            s = s_in[ci]
        sbd = jnp.concatenate(
            [jnp.concatenate([s[h].astype(BF16) if g == h else zero for g in range(heads)], axis=1)
             for h in range(heads)], axis=0)
        outs.append(o_intra[ci] + jnp.dot(chunk(qe, ci), sbd, preferred_element_type=F32))
        s = [decay[ci * heads + h] * s[h] + upd[ci * heads + h] for h in range(heads)]
        s_out.append(s)
    while fillers:
        fill()
    return outs, (s_out if independent else s)


def _head_rmsnorm(x, g, heads):
    outs = []
    for h in range(heads):
        blk = x[:, h * LANES:(h + 1) * LANES]
        ms = jnp.mean(blk * blk, axis=-1, keepdims=True)
        outs.append((blk * lax.rsqrt(ms + EPS)) * g)
    return jnp.concatenate(outs, axis=1)


def _merge_kernel(lam_init, final, da_heads, gla_heads, x_ref, oa_ref, ob_ref, dg_ref, gg_ref, mg_ref,
                  ga_ref, gb_ref, wa_ref, wb_ref, wo_ref, gf_ref, y_ref):
    d = x_ref.shape[-1]

    def silu_of_half(h):
        return h * jnp.tanh(h) + h

    oa = _head_rmsnorm(oa_ref[...], ga_ref[...] * (1.0 - lam_init), da_heads) * silu_of_half(dg_ref[...].astype(F32))
    ob = _head_rmsnorm(ob_ref[...], gb_ref[...], gla_heads) * silu_of_half(gg_ref[...].astype(F32))
    ya = jnp.dot(oa.astype(BF16), wa_ref[...], preferred_element_type=F32)
    yb = jnp.dot(ob.astype(BF16), wb_ref[...], preferred_element_type=F32)
    ta = jnp.tanh(mg_ref[:, :d].astype(F32))
    tb = jnp.tanh(mg_ref[:, d:].astype(F32))
    mixed = 0.5 * ((ta * ya + ya) + (tb * yb + yb))
    y = x_ref[...] + jnp.dot(mixed.astype(BF16), wo_ref[...], preferred_element_type=F32)
    if final:
        ms = jnp.mean(y * y, axis=-1, keepdims=True)
        y = (y * lax.rsqrt(ms + EPS)) * gf_ref[...]
    y_ref[...] = y


def _merge(x2, oa, ob, dg, gg, mg, da_g, gla_g, wa, wb, wo, gf, lam_init, final, da_heads, gla_heads, tm):
    tok, d = x2.shape
    row = lambda w: pl.BlockSpec((tm, w), lambda i: (i, 0))
    full = lambda a: pl.BlockSpec(a.shape, lambda i: (0, 0))
    da_g = da_g.reshape(1, -1)
    gla_g = gla_g.reshape(1, -1)
    gf = gf.reshape(1, d)
    return pl.pallas_call(
        functools.partial(_merge_kernel, lam_init, final, da_heads, gla_heads),
        grid=(tok // tm,),
        in_specs=[row(d), row(oa.shape[1]), row(ob.shape[1]), row(dg.shape[1]), row(gg.shape[1]), row(mg.shape[1]),
                  full(da_g), full(gla_g), full(wa), full(wb), full(wo), full(gf)],
        out_specs=row(d),
        out_shape=jax.ShapeDtypeStruct((tok, d), F32),
        compiler_params=_cparams(("arbitrary",)),
        name="merge",
    )(x2, oa, ob, dg, gg, mg, da_g, gla_g, wa, wb, wo, gf)


def _layout(da_heads, da_dk, gla_heads, gla_dk, gla_dv):
    da_qk = da_heads * 2 * da_dk
    da_w = da_heads * 2 * da_dk
    kwid = gla_heads * gla_dk
    vwid = gla_heads * gla_dv
    names = ["dq", "dk", "dv", "dg", "gq", "gk", "gv", "gg"]
    sizes = [da_qk, da_qk, da_w, da_w, kwid, kwid, vwid, vwid]
    segs, off = {}, 0
    for nm, sz in zip(names, sizes):
        segs[nm] = (off, off + sz)
        off += sz
    segs["da_scale"] = da_dk ** -0.5 * math.log2(math.e)
    segs["gla_scale"] = gla_dk ** -0.5
    return segs, off


def kernel(x_prompt, x_sample, cache_k, cache_v, state_gla, norm_in_g, w_in, w_alpha_up, b_alpha,
           lambda_q1, lambda_k1, lambda_q2, lambda_k2, da_norm_g, gla_norm_g,
           w_branch_a, w_branch_b, w_out, norm_final_g):
    depth = w_in.shape[0]
    bsz, seq, d = x_prompt.shape
    dbsz, dseq, _ = x_sample.shape
    past = cache_k.shape[2]
    da_heads, da_dk = cache_k.shape[3], cache_k.shape[5]
    gla_heads, gla_dk, gla_dv = state_gla.shape[2:]
    rank = w_alpha_up.shape[1]
    kw = gla_heads * gla_dk
    assert 2 * da_dk == LANES and cache_v.shape[4] == LANES and gla_dv == LANES
    segs, ga_lo = _layout(da_heads, da_dk, gla_heads, gla_dk, gla_dv)

    xp = x_prompt.reshape(bsz * seq, d)
    xs = x_sample.reshape(dbsz * dseq, d)
    tm_p = min(ROW_TILE, bsz * seq)
    tm_s = min(ROW_TILE, dbsz * dseq)
    outs = [[] for _ in range(6)]
    for l in range(depth):
        lam_init = 0.8 - 0.6 * math.exp(-0.3 * l)
        final = l == depth - 1
        w = w_in[l]
        w_gate = jnp.pad(w[:, ga_lo:ga_lo + rank], ((0, 0), (0, LANES - rank))).astype(BF16)
        wkt = w[:, segs["dk"][0]:segs["dk"][1]].T.astype(BF16)
        wa_pad = jnp.pad(w_alpha_up[l], ((0, LANES - rank), (0, 0))).astype(BF16)
        weights = (w[:, :ga_lo].astype(BF16), w[:, ga_lo + rank:].astype(BF16), w_gate, wkt, wa_pad)
        lamv = jnp.stack([lambda_q1[l], lambda_k1[l], lambda_q2[l], lambda_k2[l]])
        wa, wb, wo = w_branch_a[l].astype(BF16), w_branch_b[l].astype(BF16), w_out[l].astype(BF16)

        p = _proj(xp, norm_in_g[l], weights, b_alpha[l], segs, tm_p, da_heads,
                  jnp.zeros((bsz, gla_heads, gla_dk, gla_dv), F32), seq=seq)
        sh = lambda a: a.reshape(bsz, seq, a.shape[-1])
        oa = _attn_prompt(sh(p["q"]), p["kb"], sh(p["vb"]), lamv, lam_init, da_heads)
        s_p = p["sout"]
        xp = _merge(xp, oa.reshape(bsz * seq, -1), p["ob"], p["dg"], p["gg"], p["mg"],
                    da_norm_g[l], gla_norm_g[l], wa, wb, wo, norm_final_g, lam_init, final,
                    da_heads, gla_heads, tm_p)
        outs[0].append(jnp.transpose(p["kf"].reshape(bsz, da_heads, 2, da_dk, seq), (0, 4, 1, 2, 3)))
        outs[1].append(p["vf"].reshape(bsz, seq, da_heads, 2 * da_dk))
        outs[2].append(s_p)

        p = _proj(xs, norm_in_g[l], weights, b_alpha[l], segs, tm_s, da_heads, state_gla[l])
        sh = lambda a: a.reshape(dbsz, dseq, a.shape[-1])
        ckt = jnp.transpose(cache_k[l], (0, 2, 3, 4, 1)).reshape(dbsz, da_heads * 2 * da_dk, past)
        oa = _attn_sample(sh(p["q"]), ckt, cache_v[l].reshape(dbsz, past * da_heads, 2 * da_dk),
                          sh(p["kb"]), sh(p["vb"]), lamv, lam_init, da_heads)
        s_s = p["sout"]
        xs = _merge(xs, oa.reshape(dbsz * dseq, -1), p["ob"], p["dg"], p["gg"], p["mg"],
                    da_norm_g[l], gla_norm_g[l], wa, wb, wo, norm_final_g, lam_init, final,
                    da_heads, gla_heads, tm_s)
        outs[3].append(p["kf"].reshape(dbsz, dseq, da_heads, 2, da_dk))
        outs[4].append(p["vf"].reshape(dbsz, dseq, da_heads, 2 * da_dk))
        outs[5].append(s_s)

    return (xp.reshape(bsz, seq, d), xs.reshape(dbsz, dseq, d),
            jnp.stack(outs[0]), jnp.stack(outs[1]), jnp.stack(outs[2]),
            jnp.stack(outs[3]), jnp.stack(outs[4]), jnp.stack(outs[5]))
```

```python
import functools
import math

import numpy as np
import jax
import jax.numpy as jnp
from jax import lax
from jax.experimental import pallas as pl
from jax.experimental.pallas import tpu as pltpu

F32 = jnp.float32
BF16 = jnp.bfloat16

EPS = 1e-6
CHUNK = 64
GLA_TAU = 16.0
LANES = 128
VMEM_LIMIT_BYTES = 56 * 1024 * 1024

ROW_TILE = 512
PROJ_COL_STEP = 512
ATTN_TILE = 1024
ATTN_MAX_ROWS = 128
ATTN_PV_ROWS = 256
ATTN_ROWS_PER_STEP = 2

_NT = (((1,), (1,)), ((), ()))


def _log_sigmoid(x):
    return jnp.minimum(x, 0.0) - jnp.log1p(jnp.exp(-jnp.abs(x)))


def _cparams(sem):
    return pltpu.CompilerParams(dimension_semantics=sem, vmem_limit_bytes=VMEM_LIMIT_BYTES)


def _proj_kernel(segs, heads, seq_tiles, gla, in_names, out_names, *refs):
    n_in, n_out = len(in_names), len(out_names)
    r = dict(zip(in_names + out_names, refs[:n_in + n_out]))
    x = r["x"][...]
    tm = x.shape[0]
    ms = jnp.mean(x * x, axis=-1, keepdims=True)
    xn = ((x * lax.rsqrt(ms + EPS)) * r["g"][...]).astype(BF16)

    def seg(name):
        lo, hi = segs[name]
        return jnp.dot(xn, r["w_main"][:, lo:hi], preferred_element_type=F32)

    def put_q():
        r["q"][...] = (seg("dq") * segs["da_scale"]).astype(BF16)

    def put_k():
        if seq_tiles is not None:
            kt = lax.dot_general(r["wkt"][...], xn, _NT, preferred_element_type=F32)
            r["kf"][0] = kt
            r["kb"][0] = kt.astype(BF16)
        else:
            k = seg("dk")
            r["kf"][...] = k
            r["kb"][...] = k.astype(BF16)

    def put_v():
        v = seg("dv")
        r["vb"][...] = v.astype(BF16)
        if seq_tiles is not None:
            for h in range(heads):
                r["vf"][pl.ds(h, tm, stride=heads), :] = v[:, h * LANES:(h + 1) * LANES]
        else:
            r["vf"][...] = v

    def put_gate(name):
        r[name][...] = (0.5 * seg(name)).astype(BF16)

    def put_merge(c0, step):
        r["mg"][:, c0:c0 + step] = (0.5 * jnp.dot(xn, r["w_merge"][:, c0:c0 + step],
                                                  preferred_element_type=F32)).astype(BF16)

    step = PROJ_COL_STEP
    rest = [put_q, put_k, put_v, functools.partial(put_gate, "dg"), functools.partial(put_gate, "gg")]
    rest += [functools.partial(put_merge, c0, step) for c0 in range(0, r["w_merge"].shape[1], step)]

    ga = jnp.dot(xn, r["w_gate"][...], preferred_element_type=F32).astype(BF16)
    z = jnp.dot(ga, r["wa"][...], preferred_element_type=F32) + r["ba"][...]
    la = _log_sigmoid(z) / GLA_TAU
    gq = seg("gq") * segs["gla_scale"]
    gk = seg("gk")
    gv = seg("gv").astype(BF16)
    if seq_tiles is None:
        c, nchunk, gheads, nlev = gla
        states = [[r["s0"][n, h] for h in range(gheads)] for n in range(nchunk)]
        outs, s = _gla_block(c, nchunk, gheads, nlev, r["lt"][...], r["lev"][...], gq, gk, gv, la,
                             states, fillers=rest, independent=True)
        for n, o in enumerate(outs):
            r["ob"][n * c:(n + 1) * c, :] = o
            for h in range(gheads):
                r["sout"][n, h] = s[n][h]
    else:
        c, nchunk, gheads, nlev = gla
        s_ref = refs[n_in + n_out]
        ti = pl.program_id(0) % seq_tiles

        @pl.when(ti == 0)
        def _():
            s_ref[...] = r["s0"][0]

        outs, s = _gla_block(c, nchunk, gheads, nlev, r["lt"][...], r["lev"][...], gq, gk, gv, la,
                             [s_ref[h] for h in range(gheads)], fillers=rest)
        for ci, o in enumerate(outs):
            r["ob"][ci * c:(ci + 1) * c, :] = o
        for h in range(gheads):
            s_ref[h] = s[h]

        @pl.when(ti == seq_tiles - 1)
        def _():
            r["sout"][0] = s_ref[...]


def _proj(x2, g_in, weights, b_alpha, segs, tm, heads, gla_state, seq=None):
    tok, d = x2.shape
    w_main, w_merge, w_gate, wkt, wa_pad = weights
    widths = {k: v[1] - v[0] for k, v in segs.items() if isinstance(v, tuple)}
    widths["mg"] = w_merge.shape[1]
    kw = wa_pad.shape[1]
    gheads, dk, dv = gla_state.shape[1:]
    seq_tiles = None if seq is None else seq // tm
    chunk = CHUNK if seq is not None else tok // gla_state.shape[0]
    nchunk = tm // chunk
    lt3, lev, nlev = _gla_tables(chunk, gheads, nchunk)

    def row_spec(w):
        return pl.BlockSpec((tm, w), lambda i: (i, 0))

    def full_spec(shape):
        return pl.BlockSpec(shape, lambda i: (0,) * len(shape))

    def const_spec(shape):
        return pl.BlockSpec(shape, lambda i: (0,) * len(shape), pipeline_mode=pl.Buffered(1))

    if seq is not None:
        st = pl.BlockSpec((1, gheads, dk, dv), lambda i: (i // seq_tiles, 0, 0, 0))
        scratch = [pltpu.VMEM((gheads, dk, dv), F32)]
    else:
        st = pl.BlockSpec((nchunk, gheads, dk, dv), lambda i: (i, 0, 0, 0))
        scratch = []
    ins = [("x", x2, row_spec(d)), ("g", g_in.reshape(1, d), full_spec((1, d))),
           ("w_main", w_main, const_spec(w_main.shape)), ("w_merge", w_merge, const_spec(w_merge.shape)),
           ("w_gate", w_gate, const_spec(w_gate.shape)), ("wkt", wkt, const_spec(wkt.shape)),
           ("wa", wa_pad, full_spec(wa_pad.shape)), ("ba", b_alpha.reshape(1, kw), full_spec((1, kw))),
           ("lt", jnp.asarray(lt3, BF16), full_spec(lt3.shape)), ("lev", jnp.asarray(lev), full_spec(lev.shape)),
           ("s0", gla_state, st)]
    out_defs = [
        ("q", widths["dq"], BF16), ("kf", widths["dk"], F32), ("kb", widths["dk"], BF16),
        ("vf", widths["dv"], F32), ("vb", widths["dv"], BF16), ("dg", widths["dg"], BF16),
        ("gg", widths["gg"], BF16), ("mg", widths["mg"], BF16), ("ob", widths["gv"], F32),
    ]
    out_specs = {nm: row_spec(w) for nm, w, _ in out_defs}
    out_shape = {nm: jax.ShapeDtypeStruct((tok, w), dt) for nm, w, dt in out_defs}
    names = [nm for nm, _, _ in out_defs] + ["sout"]
    out_specs["sout"] = st
    out_shape["sout"] = jax.ShapeDtypeStruct(gla_state.shape, F32)
    if seq is not None:
        for nm, dt in (("kf", F32), ("kb", BF16)):
            out_specs[nm] = pl.BlockSpec((1, widths["dk"], tm), lambda i: (i // seq_tiles, 0, i % seq_tiles))
            out_shape[nm] = jax.ShapeDtypeStruct((tok // seq, widths["dk"], seq), dt)
        out_specs["vf"] = pl.BlockSpec((tm * heads, LANES), lambda i: (i, 0))
        out_shape["vf"] = jax.ShapeDtypeStruct((tok * heads, LANES), F32)
    in_names = [nm for nm, _, _ in ins]
    outs = pl.pallas_call(
        functools.partial(_proj_kernel, segs, heads, seq_tiles, (chunk, nchunk, gheads, nlev), in_names, names),
        grid=(tok // tm,),
        in_specs=[sp for _, _, sp in ins],
        out_specs=[out_specs[nm] for nm in names],
        out_shape=[out_shape[nm] for nm in names],
        scratch_shapes=scratch,
        compiler_params=_cparams(("arbitrary",)),
        name="proj",
    )(*[a for _, a, _ in ins])
    return dict(zip(names, outs))


def _diff_lambda(lamv, lam_init):
    a = jnp.sum(lamv[0:1, :] * lamv[1:2, :], axis=1, keepdims=True)
    b = jnp.sum(lamv[2:3, :] * lamv[3:4, :], axis=1, keepdims=True)
    return jnp.exp(a) - jnp.exp(b) + lam_init


def _stack_maps(q):
    lane = lax.broadcasted_iota(jnp.int32, q.shape, 1)
    half = q.shape[1] // 2
    zero = jnp.zeros_like(q)
    return jnp.concatenate([jnp.where(lane < half, q, zero), jnp.where(lane >= half, q, zero)], axis=0)


def _attn_prompt_kernel(lam_init, tq, ra, rb, lamv_ref, q_ref, kt_ref, v_ref, o_ref,
                        qx_ref, acc_ref, s0_ref, s1_ref, m0_ref, m1_ref, a0_ref, a1_ref):
    ones = jnp.ones((tq, LANES), BF16)
    sbuf, mbuf, abuf = (s0_ref, s1_ref), (m0_ref, m1_ref), (a0_ref, a1_ref)
    nq = q_ref.shape[1] // tq
    hq = tq // 2
    assert hq % max(ra, rb) == 0

    def visible(row0):
        return hq if (row0 % tq) < hq else tq

    def values(j):
        start = pl.multiple_of(j * tq, tq)
        return jnp.concatenate([v_ref[0, pl.ds(start, tq), :], ones], axis=1)

    def own_a(i):
        kt = kt_ref[0, :, pl.ds(pl.multiple_of(i * tq, tq), tq)]
        for part in range(4):
            r0 = part * hq
            ncol = visible(r0)
            s = jnp.dot(qx_ref[r0:r0 + hq, :], kt[:, :ncol], preferred_element_type=F32)
            qc = (r0 % tq + lax.broadcasted_iota(jnp.int32, s.shape, 0)) // CHUNK
            kc = lax.broadcasted_iota(jnp.int32, s.shape, 1) // CHUNK
            s0_ref[r0:r0 + hq, :ncol] = jnp.where(kc <= qc, s, -jnp.inf)
        for r in range(2 * tq // ra):
            rows = slice(r * ra, (r + 1) * ra)
            m_cur = jnp.max(s0_ref[rows, :visible(r * ra)], axis=1, keepdims=True)
            m0_ref[rows, :] = jnp.broadcast_to(m_cur, (ra, LANES))

    def own_b(i):
        vx = values(i)
        for r in range(2 * tq // rb):
            rows = slice(r * rb, (r + 1) * rb)
            ncol = visible(r * rb)
            p = jnp.exp2(s0_ref[rows, :ncol] - jnp.tile(m0_ref[rows, :], (1, ncol // LANES)))
            acc_ref[rows, :] = jnp.dot(p.astype(BF16), vx[:ncol], preferred_element_type=F32)

    def stage_a(j, slot):
        kt = kt_ref[0, :, pl.ds(pl.multiple_of(j * tq, tq), tq)]
        sbuf[slot][...] = jnp.dot(qx_ref[...], kt, preferred_element_type=F32)
        for r in range(2 * tq // ra):
            rows = slice(r * ra, (r + 1) * ra)
            m_prev = mbuf[1 - slot][rows, :]
            m_new = jnp.maximum(m_prev, jnp.max(sbuf[slot][rows, :], axis=1, keepdims=True))
            abuf[slot][rows, :] = jnp.exp2(m_prev - m_new)
            mbuf[slot][rows, :] = m_new

    def stage_b(j, slot):
        vx = values(j)
        for r in range(2 * tq // rb):
            rows = slice(r * rb, (r + 1) * rb)
            p = jnp.exp2(sbuf[slot][rows, :] - jnp.tile(mbuf[slot][rows, :], (1, tq // LANES)))
            pv = jnp.dot(p.astype(BF16), vx, preferred_element_type=F32)
            acc_ref[rows, :] = acc_ref[rows, :] * jnp.tile(abuf[slot][rows, :], (1, 2)) + pv

    def finish(i):
        acc = acc_ref[...]
        o1 = acc[:tq, :LANES] / acc[:tq, LANES:]
        o2 = acc[tq:, :LANES] / acc[tq:, LANES:]
        o_ref[0, pl.ds(pl.multiple_of(i * tq, tq), tq), :] = o1 - _diff_lambda(lamv_ref[...], lam_init) * o2

    def start(i):
        qx_ref[...] = _stack_maps(q_ref[0, pl.ds(pl.multiple_of(i * tq, tq), tq), :])
        finish(jnp.maximum(i - 1, 0))
        own_a(i)

    def q_tile(i, carry):
        @pl.when(i == 0)
        def _():
            start(i)
            own_b(i)

        @pl.when(i > 0)
        def _():
            start(i)
            stage_a(0, 1)
            own_b(i)

            def pair(u, c):
                stage_a(2 * u + 1, 0)
                stage_b(2 * u, 1)
                stage_a(2 * u + 2, 1)
                stage_b(2 * u + 1, 0)
                return c

            lax.fori_loop(0, (i - 1) // 2, pair, 0)

            @pl.when(i % 2 == 1)
            def _():
                stage_b(i - 1, 1)

            @pl.when(i % 2 == 0)
            def _():
                stage_a(i - 1, 0)
                stage_b(i - 2, 1)
                stage_b(i - 1, 0)

        return carry

    acc_ref[...] = jnp.ones(acc_ref.shape, F32)
    lax.fori_loop(0, nq, q_tile, 0)
    finish(nq - 1)


def _attn_prompt(q, kt, v, lamv, lam_init, heads):
    bsz, t, w = q.shape
    tq = min(ATTN_TILE, t)
    ra = min(ATTN_MAX_ROWS, tq)
    rb = min(ATTN_PV_ROWS, tq)
    rowbuf = pltpu.VMEM((2 * tq, LANES), F32)
    return pl.pallas_call(
        functools.partial(_attn_prompt_kernel, lam_init, tq, ra, rb),
        grid=(bsz, heads),
        in_specs=[pl.BlockSpec(lamv.shape, lambda b, h: (0, 0)),
                  pl.BlockSpec((1, t, LANES), lambda b, h: (b, 0, h)),
                  pl.BlockSpec((1, LANES, t), lambda b, h: (b, h, 0)),
                  pl.BlockSpec((1, t, LANES), lambda b, h: (b, 0, h))],
        out_specs=pl.BlockSpec((1, t, LANES), lambda b, h: (b, 0, h)),
        out_shape=jax.ShapeDtypeStruct((bsz, t, w), F32),
        scratch_shapes=[pltpu.VMEM((2 * tq, LANES), BF16), pltpu.VMEM((2 * tq, 2 * LANES), F32),
                        pltpu.VMEM((2 * tq, tq), F32), pltpu.VMEM((2 * tq, tq), F32),
                        rowbuf, rowbuf, rowbuf, rowbuf],
        compiler_params=_cparams(("arbitrary", "arbitrary")),
        name="attn_prompt",
    )(lamv, q, kt, v)


def _attn_sample_kernel(lam_init, heads, lamv_ref, q_ref, ckt_ref, cv_ref, kn_ref, vn_ref, o_ref):
    nrow, tq = q_ref.shape[0], q_ref.shape[1]
    past = ckt_ref.shape[2]
    lam = _diff_lambda(lamv_ref[...], lam_init)
    for n in range(nrow):
        for h in range(heads):
            sl = slice(h * LANES, (h + 1) * LANES)
            qx = _stack_maps(q_ref[n, :, sl])
            kc = ckt_ref[n, sl, :].astype(BF16)
            vc = cv_ref[n, pl.ds(h, past, stride=heads), :].astype(BF16)
            kn = kn_ref[n, :, sl]
            vn = vn_ref[n, :, sl]
            sc = jnp.dot(qx, kc, preferred_element_type=F32)
            sn = lax.dot_general(qx, kn, _NT, preferred_element_type=F32)
            m = jnp.maximum(jnp.max(sc, axis=1, keepdims=True), jnp.max(sn, axis=1, keepdims=True))
            ec = jnp.exp2(sc - m)
            en = jnp.exp2(sn - m)
            l = jnp.sum(ec, axis=1, keepdims=True) + jnp.sum(en, axis=1, keepdims=True)
            o = (jnp.dot(ec.astype(BF16), vc, preferred_element_type=F32)
                 + jnp.dot(en.astype(BF16), vn, preferred_element_type=F32)) / l
            o_ref[n, :, sl] = o[:tq] - lam * o[tq:]


def _attn_sample(q, ckt, cv, kn, vn, lamv, lam_init, heads):
    bsz, tq, w = q.shape
    nrow = math.gcd(bsz, ATTN_ROWS_PER_STEP)
    rows = lambda a: pl.BlockSpec((nrow,) + a.shape[1:], lambda b: (b, 0, 0))
    return pl.pallas_call(
        functools.partial(_attn_sample_kernel, lam_init, heads),
        grid=(bsz // nrow,),
        in_specs=[pl.BlockSpec(lamv.shape, lambda b: (0, 0)), rows(q), rows(ckt), rows(cv), rows(kn), rows(vn)],
        out_specs=rows(q),
        out_shape=jax.ShapeDtypeStruct((bsz, tq, w), F32),
        compiler_params=_cparams(("arbitrary",)),
        name="attn_sample",
    )(lamv, q, ckt, cv, kn, vn)


def _gla_tables(c, heads, nchunk):
    nlev = int(math.log2(c))
    assert 1 << nlev == c and c % 8 == 0
    t = np.arange(c)[:, None]
    u = np.arange(c)[None, :]
    lev = np.where(t == u, 0, -1)
    for l in range(1, nlev + 1):
        n = c >> l
        same_block = (t // (2 * n)) == (u // (2 * n))
        lev = np.where(same_block & ((t % (2 * n)) >= n) & ((u % (2 * n)) < n), l, lev)
    lt = (u <= t).astype(np.float32)
    return np.concatenate([lt, lt, lt], axis=1), np.tile(lev, (nchunk, heads)).astype(np.int32), nlev


def _block_reference(b, n):
    rows, w = b.shape
    if n >= 8:
        parts = [jnp.broadcast_to(b[g + n - 1:g + n, :], (2 * n, w)) for g in range(0, rows, 2 * n)]
        return parts[0] if len(parts) == 1 else jnp.concatenate(parts, axis=0)
    b3 = b.reshape(rows // 8, 8, w)
    sub = lax.broadcasted_iota(jnp.int32, b3.shape, 1)
    pick = lambda j: jnp.broadcast_to(b3[:, j:j + 1, :], b3.shape)
    if n == 4:
        ref = pick(3)
    elif n == 2:
        ref = jnp.where(sub < 4, pick(1), pick(5))
    else:
        ref = jnp.where(sub < 2, pick(0), jnp.where(sub < 4, pick(2), jnp.where(sub < 6, pick(4), pick(6))))
    return ref.reshape(rows, w)


def _gla_block(c, nchunk, heads, nlev, lt, lev, q, k, v, la, s, fillers=(), independent=False):
    s_in = s
    fillers = list(fillers)
    fill = lambda: fillers.pop(0)() if fillers else None
    kw = q.shape[-1]
    vw = v.shape[-1]
    dk = kw // heads
    dv = vw // heads
    pad = LANES - c
    assert dv == LANES and pad >= 0
    chunk = lambda a, ci: a[ci * c:(ci + 1) * c]

    la2 = la * math.log2(math.e)
    p0 = la2.astype(BF16)
    r0 = la2 - p0.astype(F32)
    p1 = r0.astype(BF16)
    p2 = (r0 - p1.astype(F32)).astype(BF16)
    b = jnp.concatenate(
        [jnp.dot(lt, jnp.concatenate([chunk(p0, ci), chunk(p1, ci), chunk(p2, ci)], axis=0),
                 preferred_element_type=F32) for ci in range(nchunk)], axis=0)

    t = lax.broadcasted_iota(jnp.int32, b.shape, 0) % c
    klane_h = lax.broadcasted_iota(jnp.int32, (c, kw), 1) // dk
    vlane_h = lax.broadcasted_iota(jnp.int32, (c, vw), 1) // dv

    att = None
    for l in range(nlev + 1):
        if l == 0:
            qt, kt = q.astype(BF16), k.astype(BF16)
        else:
            e = jnp.exp2(-jnp.abs(b - _block_reference(b, c >> l)))
            qt = (q * e).astype(BF16)
            kt = (k * e).astype(BF16)
        parts = []
        for ci in range(nchunk):
            ktc = chunk(kt, ci)
            kstack = jnp.concatenate(
                [jnp.where(klane_h == h, ktc, jnp.zeros_like(ktc)) for h in range(heads)], axis=0)
            parts.append(lax.dot_general(chunk(qt, ci), kstack, _NT, preferred_element_type=F32))
        a = parts[0] if nchunk == 1 else jnp.concatenate(parts, axis=0)
        att = jnp.where(lev == l, a, 0.0 if att is None else att)
        fill()
    att = att.astype(BF16)
    qe = (q * jnp.exp2(b)).astype(BF16)

    o_intra, upd, decay = [], [], []
    for ci in range(nchunk):
        rows = slice(ci * c, (ci + 1) * c)
        vc = v[rows]
        vbd = jnp.concatenate(
            [jnp.where(vlane_h == h, vc, jnp.zeros_like(vc)) for h in range(heads)], axis=0)
        o_intra.append(jnp.dot(att[rows], vbd, preferred_element_type=F32))
        b_last = b[ci * c + c - 1:ci * c + c, :]
        kd = k[rows] * jnp.exp2(b_last - b[rows])
        x = jnp.concatenate([kd, jnp.zeros((pad, kw), F32), jnp.broadcast_to(b_last, (LANES, kw))], axis=0)
        xt = x.T
        for h in range(heads):
            hk = slice(h * dk, (h + 1) * dk)
            vpad = jnp.concatenate([vc[:, h * dv:(h + 1) * dv], jnp.zeros((pad, dv), BF16)], axis=0)
            upd.append(jnp.dot(xt[hk, :LANES].astype(BF16), vpad, preferred_element_type=F32))
            decay.append(jnp.exp2(xt[hk, LANES:]))
        fill()

    zero = jnp.zeros((dk, dv), BF16)
    outs, s_out = [], []
    for ci in range(nchunk):
        if independent:
            s = s_in[ci]
        sbd = jnp.concatenate(
            [jnp.concatenate([s[h].astype(BF16) if g == h else zero for g in range(heads)], axis=1)
             for h in range(heads)], axis=0)
        outs.append(o_intra[ci] + jnp.dot(chunk(qe, ci), sbd, preferred_element_type=F32))
        s = [decay[ci * heads + h] * s[h] + upd[ci * heads + h] for h in range(heads)]
        s_out.append(s)
    while fillers:
        fill()
    return outs, (s_out if independent else s)


def _head_rmsnorm(x, g, heads):
    outs = []
    for h in range(heads):
        blk = x[:, h * LANES:(h + 1) * LANES]
        ms = jnp.mean(blk * blk, axis=-1, keepdims=True)
        outs.append((blk * lax.rsqrt(ms + EPS)) * g)
    return jnp.concatenate(outs, axis=1)


def _merge_kernel(lam_init, final, da_heads, gla_heads, x_ref, oa_ref, ob_ref, dg_ref, gg_ref, mg_ref,
                  ga_ref, gb_ref, wa_ref, wb_ref, wo_ref, gf_ref, y_ref):
    d = x_ref.shape[-1]

    def silu_of_half(h):
        return h * jnp.tanh(h) + h

    oa = _head_rmsnorm(oa_ref[...], ga_ref[...] * (1.0 - lam_init), da_heads) * silu_of_half(dg_ref[...].astype(F32))
    ob = _head_rmsnorm(ob_ref[...], gb_ref[...], gla_heads) * silu_of_half(gg_ref[...].astype(F32))
    ya = jnp.dot(oa.astype(BF16), wa_ref[...], preferred_element_type=F32)
    yb = jnp.dot(ob.astype(BF16), wb_ref[...], preferred_element_type=F32)
    ta = jnp.tanh(mg_ref[:, :d].astype(F32))
    tb = jnp.tanh(mg_ref[:, d:].astype(F32))
    mixed = 0.5 * ((ta * ya + ya) + (tb * yb + yb))
    y = x_ref[...] + jnp.dot(mixed.astype(BF16), wo_ref[...], preferred_element_type=F32)
    if final:
        ms = jnp.mean(y * y, axis=-1, keepdims=True)
        y = (y * lax.rsqrt(ms + EPS)) * gf_ref[...]
    y_ref[...] = y


def _merge(x2, oa, ob, dg, gg, mg, da_g, gla_g, wa, wb, wo, gf, lam_init, final, da_heads, gla_heads, tm):
    tok, d = x2.shape
    row = lambda w: pl.BlockSpec((tm, w), lambda i: (i, 0))
    full = lambda a: pl.BlockSpec(a.shape, lambda i: (0, 0))
    da_g = da_g.reshape(1, -1)
    gla_g = gla_g.reshape(1, -1)
    gf = gf.reshape(1, d)
    return pl.pallas_call(
        functools.partial(_merge_kernel, lam_init, final, da_heads, gla_heads),
        grid=(tok // tm,),
        in_specs=[row(d), row(oa.shape[1]), row(ob.shape[1]), row(dg.shape[1]), row(gg.shape[1]), row(mg.shape[1]),
                  full(da_g), full(gla_g), full(wa), full(wb), full(wo), full(gf)],
        out_specs=row(d),
        out_shape=jax.ShapeDtypeStruct((tok, d), F32),
        compiler_params=_cparams(("arbitrary",)),
        name="merge",
    )(x2, oa, ob, dg, gg, mg, da_g, gla_g, wa, wb, wo, gf)


def _layout(da_heads, da_dk, gla_heads, gla_dk, gla_dv):
    da_qk = da_heads * 2 * da_dk
    da_w = da_heads * 2 * da_dk
    kwid = gla_heads * gla_dk
    vwid = gla_heads * gla_dv
    names = ["dq", "dk", "dv", "dg", "gq", "gk", "gv", "gg"]
    sizes = [da_qk, da_qk, da_w, da_w, kwid, kwid, vwid, vwid]
    segs, off = {}, 0
    for nm, sz in zip(names, sizes):
        segs[nm] = (off, off + sz)
        off += sz
    segs["da_scale"] = da_dk ** -0.5 * math.log2(math.e)
    segs["gla_scale"] = gla_dk ** -0.5
    return segs, off


def kernel(x_prompt, x_sample, cache_k, cache_v, state_gla, norm_in_g, w_in, w_alpha_up, b_alpha,
           lambda_q1, lambda_k1, lambda_q2, lambda_k2, da_norm_g, gla_norm_g,
           w_branch_a, w_branch_b, w_out, norm_final_g):
    depth = w_in.shape[0]
    bsz, seq, d = x_prompt.shape
    dbsz, dseq, _ = x_sample.shape
    past = cache_k.shape[2]
    da_heads, da_dk = cache_k.shape[3], cache_k.shape[5]
    gla_heads, gla_dk, gla_dv = state_gla.shape[2:]
    rank = w_alpha_up.shape[1]
    kw = gla_heads * gla_dk
    assert 2 * da_dk == LANES and cache_v.shape[4] == LANES and gla_dv == LANES
    segs, ga_lo = _layout(da_heads, da_dk, gla_heads, gla_dk, gla_dv)

    xp = x_prompt.reshape(bsz * seq, d)
    xs = x_sample.reshape(dbsz * dseq, d)
    tm_p = min(ROW_TILE, bsz * seq)
    tm_s = min(ROW_TILE, dbsz * dseq)
    outs = [[] for _ in range(6)]
    for l in range(depth):
        lam_init = 0.8 - 0.6 * math.exp(-0.3 * l)
        final = l == depth - 1
        w = w_in[l]
        w_gate = jnp.pad(w[:, ga_lo:ga_lo + rank], ((0, 0), (0, LANES - rank))).astype(BF16)
        wkt = w[:, segs["dk"][0]:segs["dk"][1]].T.astype(BF16)
        wa_pad = jnp.pad(w_alpha_up[l], ((0, LANES - rank), (0, 0))).astype(BF16)
        weights = (w[:, :ga_lo].astype(BF16), w[:, ga_lo + rank:].astype(BF16), w_gate, wkt, wa_pad)
        lamv = jnp.stack([lambda_q1[l], lambda_k1[l], lambda_q2[l], lambda_k2[l]])
        wa, wb, wo = w_branch_a[l].astype(BF16), w_branch_b[l].astype(BF16), w_out[l].astype(BF16)

        p = _proj(xp, norm_in_g[l], weights, b_alpha[l], segs, tm_p, da_heads,
                  jnp.zeros((bsz, gla_heads, gla_dk, gla_dv), F32), seq=seq)
        sh = lambda a: a.reshape(bsz, seq, a.shape[-1])
        oa = _attn_prompt(sh(p["q"]), p["kb"], sh(p["vb"]), lamv, lam_init, da_heads)
        s_p = p["sout"]
        xp = _merge(xp, oa.reshape(bsz * seq, -1), p["ob"], p["dg"], p["gg"], p["mg"],
                    da_norm_g[l], gla_norm_g[l], wa, wb, wo, norm_final_g, lam_init, final,
                    da_heads, gla_heads, tm_p)
        outs[0].append(jnp.transpose(p["kf"].reshape(bsz, da_heads, 2, da_dk, seq), (0, 4, 1, 2, 3)))
        outs[1].append(p["vf"].reshape(bsz, seq, da_heads, 2 * da_dk))
        outs[2].append(s_p)

        p = _proj(xs, norm_in_g[l], weights, b_alpha[l], segs, tm_s, da_heads, state_gla[l])
        sh = lambda a: a.reshape(dbsz, dseq, a.shape[-1])
        ckt = jnp.transpose(cache_k[l], (0, 2, 3, 4, 1)).reshape(dbsz, da_heads * 2 * da_dk, past)
        oa = _attn_sample(sh(p["q"]), ckt, cache_v[l].reshape(dbsz, past * da_heads, 2 * da_dk),
                          sh(p["kb"]), sh(p["vb"]), lamv, lam_init, da_heads)
        s_s = p["sout"]
        xs = _merge(xs, oa.reshape(dbsz * dseq, -1), p["ob"], p["dg"], p["gg"], p["mg"],
                    da_norm_g[l], gla_norm_g[l], wa, wb, wo, norm_final_g, lam_init, final,
                    da_heads, gla_heads, tm_s)
        outs[3].append(p["kf"].reshape(dbsz, dseq, da_heads, 2, da_dk))
        outs[4].append(p["vf"].reshape(dbsz, dseq, da_heads, 2 * da_dk))
        outs[5].append(s_s)

    return (xp.reshape(bsz, seq, d), xs.reshape(dbsz, dseq, d),
            jnp.stack(outs[0]), jnp.stack(outs[1]), jnp.stack(outs[2]),
            jnp.stack(outs[3]), jnp.stack(outs[4]), jnp.stack(outs[5]))
```
